```python
import jax
import jax.numpy as jnp
from jax import lax
import numpy as np

D_MODEL = 2048
BATCH = 8
SEQ = 2048
DEPTH = 2
DEC_BATCH = 128
DEC_SEQ = 4
PAST_LEN = 8192
PAGE_SIZE = 128

WINDOW = 128
HEAD_DIM = 64
N_HEADS = 16
N_KV_HEADS = 4
GQA_GROUP = N_HEADS // N_KV_HEADS
ATTN_WIDTH = N_HEADS * HEAD_DIM
KV_WIDTH = N_KV_HEADS * HEAD_DIM
ATTN_SCALE = HEAD_DIM ** -0.5
NEG_INF = -1e30
POOL_WINDOWS = (2, 4, 8, 16)
POOL_GROUPS = 4
POOL_WIDTH = 768
POOL_GW = POOL_WIDTH // POOL_GROUPS
POOL_BUF = max(POOL_WINDOWS) - 1
RWKV_HEAD = 64
RWKV_WIDTH = 768
RWKV_HEADS = RWKV_WIDTH // RWKV_HEAD
DECAY_LORA = 64
AAA_LORA = 64
GATE_LORA = 128
RWKV_SECTIONS = (RWKV_WIDTH, RWKV_WIDTH, RWKV_WIDTH, DECAY_LORA, AAA_LORA, GATE_LORA)
RWKV_PROJ = 3 * RWKV_WIDTH + DECAY_LORA + AAA_LORA + GATE_LORA
RWKV_LN_EPS = 64e-5
CONV_WIDTH = 768
CONV_K = 3
N_BRANCH = 4
BRANCH_SECTIONS = (ATTN_WIDTH, POOL_WIDTH, RWKV_WIDTH, CONV_WIDTH)
MIX_WIDTH = ATTN_WIDTH + POOL_WIDTH + RWKV_WIDTH + CONV_WIDTH
IN_SECTIONS = (ATTN_WIDTH, KV_WIDTH, KV_WIDTH, POOL_WIDTH, RWKV_PROJ,
               CONV_WIDTH, CONV_WIDTH, CONV_WIDTH, N_BRANCH * D_MODEL)
IN_WIDTH = sum(IN_SECTIONS)
N_EXPERTS = 16
N_GROUPS = 4
EXP_PER_GROUP = N_EXPERTS // N_GROUPS
TOP_K = 2
D_EXPERT = 1024
RMS_EPS = 1e-6

kernel_name = "hybrid_swa_pool_rwkv7_conv_moe_adaln_step"


def _split_points(sizes):
    pts, acc = [], 0
    for s in sizes[:-1]:
        acc += s
        pts.append(acc)
    return pts


def rmsnorm(x, g):
    xf = x.astype(jnp.float32)
    y = xf * lax.rsqrt(jnp.mean(xf * xf, axis=-1, keepdims=True) + RMS_EPS)
    return (y * g.astype(jnp.float32)).astype(x.dtype)


def sink_softmax(s, sinks):
    sk = jnp.broadcast_to(sinks.astype(jnp.float32)[:, :, None, None], s.shape[:-1] + (1,))
    m = jnp.maximum(jnp.max(s, axis=-1, keepdims=True), sk)
    e = jnp.exp(s - m)
    return e / (jnp.sum(e, axis=-1, keepdims=True) + jnp.exp(sk - m))


def swa_prompt(q, k, v, sinks):
    B, T = q.shape[:2]
    nb = T // WINDOW
    qb = q.reshape(B, nb, WINDOW, N_KV_HEADS, GQA_GROUP, HEAD_DIM)

    def band(t):
        tp = jnp.pad(t, ((0, 0), (WINDOW, 0), (0, 0), (0, 0))).reshape(B, nb + 1, WINDOW, N_KV_HEADS, HEAD_DIM)
        return jnp.concatenate([tp[:, :-1], tp[:, 1:]], axis=2)

    kb, vb = band(k), band(v)
    s = jnp.einsum("bnqkgd,bnskd->bnkgqs", qb, kb, preferred_element_type=jnp.float32) * ATTN_SCALE
    qi = jnp.arange(WINDOW)[:, None]
    sj = jnp.arange(2 * WINDOW)[None, :]
    dist = qi + WINDOW - sj
    key_pos = (jnp.arange(nb)[:, None, None] - 1) * WINDOW + sj[None]
    valid = ((dist >= 0) & (dist <= WINDOW))[None] & (key_pos >= 0)
    s = jnp.where(valid[None, :, None, None], s, NEG_INF)
    p = sink_softmax(s, sinks.reshape(N_KV_HEADS, GQA_GROUP))
    o = jnp.einsum("bnkgqs,bnskd->bnqkgd", p.astype(vb.dtype), vb)
    return o.reshape(B, T, ATTN_WIDTH)


def swa_sample(q, k, v, k_buf, v_buf, sinks):
    B, S = q.shape[:2]
    wb = k_buf.shape[1]
    kf = jnp.concatenate([k_buf.astype(k.dtype), k], axis=1)
    vf = jnp.concatenate([v_buf.astype(v.dtype), v], axis=1)
    qg = q.reshape(B, S, N_KV_HEADS, GQA_GROUP, HEAD_DIM)
    s = jnp.einsum("bqkgd,bskd->bkgqs", qg, kf, preferred_element_type=jnp.float32) * ATTN_SCALE
    dist = jnp.arange(S)[:, None] + wb - jnp.arange(wb + S)[None, :]
    valid = (dist >= 0) & (dist <= WINDOW)
    s = jnp.where(valid, s, NEG_INF)
    p = sink_softmax(s, sinks.reshape(N_KV_HEADS, GQA_GROUP))
    o = jnp.einsum("bkgqs,bskd->bqkgd", p.astype(vf.dtype), vf).reshape(B, S, ATTN_WIDTH)
    return o, kf[:, -wb:], vf[:, -wb:]


def pool_mixer(u, u_past, pos0, w_grp, ls):
    B, T, _ = u.shape
    P = u_past.shape[1]
    full = jnp.concatenate([u_past.astype(u.dtype), u], axis=1)
    cs = jnp.pad(jnp.cumsum(full.astype(jnp.float32), axis=1), ((0, 0), (1, 0), (0, 0)))
    uf = u.astype(jnp.float32)
    pos = pos0 + jnp.arange(T)
    diffs = []
    for gi, win in enumerate(POOL_WINDOWS):
        sl = slice(gi * POOL_GW, (gi + 1) * POOL_GW)
        wsum = cs[:, P + 1:P + 1 + T, sl] - cs[:, P + 1 - win:P + 1 - win + T, sl]
        cnt = jnp.minimum(win, pos + 1).astype(jnp.float32)[None, :, None]
        diffs.append(wsum / cnt - uf[..., sl])
    d = jnp.stack(diffs, axis=2)
    y = jnp.einsum("btgi,gij->btgj", d, w_grp.astype(jnp.float32)).reshape(B, T, POOL_WIDTH)
    y = y * ls.astype(jnp.float32)
    return y.astype(u.dtype), full[:, -P:]


def wkv7(r, decay, k, v, a, b, s0):
    def step(S, inp):
        r_t, w_t, k_t, v_t, a_t, b_t = inp
        sa = jnp.einsum("bhij,bhj->bhi", S, a_t)
        S = S * w_t[:, :, None, :] + sa[..., None] * b_t[:, :, None, :] + v_t[..., None] * k_t[:, :, None, :]
        y = jnp.einsum("bhij,bhj->bhi", S, r_t)
        return S, y

    xs = tuple(jnp.moveaxis(t, 1, 0) for t in (r, decay, k, v, a, b))
    s_fin, ys = lax.scan(step, s0, xs)
    return jnp.moveaxis(ys, 0, 1), s_fin


def rwkv7(pc, p_prev, s0, mu, w0, w2, a0, a2, g2, k_k, k_a, r_k, ln_g, ln_b):
    B, T, _ = pc.shape
    f32 = jnp.float32
    shifted = jnp.concatenate([p_prev[:, None].astype(pc.dtype), pc[:, :-1]], axis=1)
    xs = pc + (shifted - pc) * mu
    r, k, v, wd, ad, gd = jnp.split(xs, _split_points(RWKV_SECTIONS), axis=-1)
    w_log = -jax.nn.softplus(-(w0 + jnp.tanh(wd) @ w2).astype(f32)) - 0.5
    decay = jnp.exp(-jnp.exp(w_log))
    a = jax.nn.sigmoid((a0 + ad @ a2).astype(f32))
    g = (jax.nn.sigmoid(gd) @ g2).astype(f32)

    def hs(t):
        return t.astype(f32).reshape(B, T, RWKV_HEADS, RWKV_HEAD)

    kk = hs(k.astype(f32) * k_k.astype(f32))
    kk = kk / jnp.maximum(jnp.sqrt(jnp.sum(kk * kk, axis=-1, keepdims=True)), 1e-12)
    kf = hs(k.astype(f32) * (1.0 + (a - 1.0) * k_a.astype(f32)))
    rf, vf, ah = hs(r), hs(v), hs(a)
    y, s_new = wkv7(rf, hs(decay), kf, vf, -kk, kk * ah, s0.astype(f32))
    mean = jnp.mean(y, axis=-1, keepdims=True)
    var = jnp.mean(jnp.square(y - mean), axis=-1, keepdims=True)
    yn = ((y - mean) * lax.rsqrt(var + RWKV_LN_EPS)).reshape(B, T, RWKV_WIDTH)
    yn = yn * ln_g.astype(f32) + ln_b.astype(f32)
    bonus = jnp.sum(rf * kf * r_k.astype(f32), axis=-1, keepdims=True) * vf
    out = (yn + bonus.reshape(B, T, RWKV_WIDTH)) * g
    return out.astype(pc.dtype), pc[:, -1], s_new.astype(s0.dtype)


def short_conv(z, z_past, w):
    T = z.shape[1]
    full = jnp.concatenate([z_past.astype(z.dtype), z], axis=1)
    y = sum(w[j] * full[:, j:j + T] for j in range(CONV_K))
    return y, full[:, -(CONV_K - 1):]


def moe(h, w_router, b_router, w_gate, w_up, w_down):
    B, T, D = h.shape
    hf = h.reshape(B * T, D)
    f32 = jnp.float32
    logits = jnp.dot(hf, w_router, preferred_element_type=f32) + b_router.astype(f32)
    probs = jax.nn.softmax(logits, axis=-1)
    pg = probs.reshape(-1, N_GROUPS, EXP_PER_GROUP)
    grp_score = jnp.sum(lax.top_k(pg, TOP_K)[0], axis=-1)
    g_sel = jnp.argmax(grp_score, axis=-1)
    p_in = jnp.take_along_axis(pg, g_sel[:, None, None], axis=1)[:, 0]
    top_p, top_i = lax.top_k(p_in, TOP_K)
    e_idx = g_sel[:, None] * EXP_PER_GROUP + top_i
    wts = top_p / jnp.sum(top_p, axis=-1, keepdims=True)
    combine = jnp.sum(jax.nn.one_hot(e_idx, N_EXPERTS, dtype=f32) * wts[..., None], axis=1)
    out = jnp.zeros(hf.shape, f32)
    for e in range(N_EXPERTS):
        act = jax.nn.silu(hf @ w_gate[e]) * (hf @ w_up[e])
        out = out + combine[:, e:e + 1] * (act @ w_down[e]).astype(f32)
    return out.astype(h.dtype).reshape(B, T, D)


def trunk_layer(x, c, lp, past, pos0, w_router, b_router):
    B, T, _ = x.shape
    mod = (jax.nn.silu(c) @ lp["w_ada"] + lp["b_ada"])[:, None, :]
    sh1, sc1, gt1, sh2, sc2, gt2 = jnp.split(mod, 6, axis=-1)
    h = rmsnorm(x, lp["g1"]) * (1.0 + sc1) + sh1
    p = h @ lp["w_in"]
    q, k, v, u, pc, cb, cc, cx, gl = jnp.split(p, _split_points(IN_SECTIONS), axis=-1)
    q = q.reshape(B, T, N_HEADS, HEAD_DIM)
    k = k.reshape(B, T, N_KV_HEADS, HEAD_DIM)
    v = v.reshape(B, T, N_KV_HEADS, HEAD_DIM)
    if past is None:
        a_out = swa_prompt(q, k, v, lp["sinks"])
        wb = min(WINDOW, T)
        k_new, v_new = k[:, T - wb:], v[:, T - wb:]
        pool_past = jnp.zeros((B, POOL_BUF, POOL_WIDTH), x.dtype)
        shift_past = jnp.zeros((B, RWKV_PROJ), x.dtype)
        wkv_past = jnp.zeros((B, RWKV_HEADS, RWKV_HEAD, RWKV_HEAD), x.dtype)
        conv_past = jnp.zeros((B, CONV_K - 1, CONV_WIDTH), x.dtype)
    else:
        k_buf, v_buf, pool_past, shift_past, wkv_past, conv_past = past
        a_out, k_new, v_new = swa_sample(q, k, v, k_buf, v_buf, lp["sinks"])
    b_out, pool_new = pool_mixer(u, pool_past, pos0, lp["w_pool"], lp["ls_pool"])
    c_out, shift_new, wkv_new = rwkv7(pc, shift_past, wkv_past, lp["mu"], lp["w0"], lp["w2"], lp["a0"],
                                      lp["a2"], lp["g2r"], lp["kk"], lp["ka"], lp["rk"], lp["ln_g"], lp["ln_b"])
    conv_y, conv_new = short_conv(cc * cx, conv_past, lp["conv_w"])
    d_out = cb * conv_y
    branches = (a_out, b_out, c_out, d_out)
    w_br = jnp.split(lp["w_branch"], _split_points(BRANCH_SECTIONS), axis=0)
    gates = jax.nn.sigmoid(gl.astype(jnp.float32)).reshape(B, T, N_BRANCH, D_MODEL)
    merged = sum(gates[:, :, i] * (branches[i] @ w_br[i]).astype(jnp.float32) for i in range(N_BRANCH))
    x = x + gt1 * (merged.astype(x.dtype) @ lp["w_out"])
    h2 = rmsnorm(x, lp["g2"]) * (1.0 + sc2) + sh2
    x = x + gt2 * moe(h2, w_router, b_router, lp["w_gate"], lp["w_up"], lp["w_down"])
    return x, (k_new, v_new, pool_new, shift_new, wkv_new, conv_new)


def setup_inputs(seed: int = 0) -> dict:
    key = jax.random.key(seed)
    kit = iter(list(jax.random.split(key, 64)))
    f32 = jnp.float32

    def nrm(shape, scale):
        return scale * jax.random.normal(next(kit), shape, f32)

    L, D = DEPTH, D_MODEL
    wbuf = min(WINDOW, PAST_LEN)
    w_branch = jnp.concatenate([nrm((L, w, D), w ** -0.5) for w in BRANCH_SECTIONS], axis=1)
    return {
        "x_prompt": nrm((BATCH, SEQ, D), 1.0),
        "x_sample": nrm((DEC_BATCH, DEC_SEQ, D), 1.0),
        "cache_swa_k": nrm((L, DEC_BATCH, wbuf, N_KV_HEADS, HEAD_DIM), 1.0),
        "cache_swa_v": nrm((L, DEC_BATCH, wbuf, N_KV_HEADS, HEAD_DIM), 1.0),
        "state_pool": nrm((L, DEC_BATCH, POOL_BUF, POOL_WIDTH), 1.0),
        "state_rwkv_shift": nrm((L, DEC_BATCH, RWKV_PROJ), 1.0),
        "state_rwkv_wkv": nrm((L, DEC_BATCH, RWKV_HEADS, RWKV_HEAD, RWKV_HEAD), 0.5),
        "state_conv": nrm((L, DEC_BATCH, CONV_K - 1, CONV_WIDTH), 1.0),
        "c_prompt": nrm((BATCH, D), 1.0),
        "c_sample": nrm((DEC_BATCH, D), 1.0),
        "w_ada": nrm((L, D, 6 * D), 0.5 * D ** -0.5),
        "b_ada": nrm((L, 6 * D), 0.02),
        "g_norm1": 1.0 + nrm((L, D), 0.02),
        "g_norm2": 1.0 + nrm((L, D), 0.02),
        "w_in": nrm((L, D, IN_WIDTH), D ** -0.5),
        "sinks": nrm((L, N_HEADS), 0.5),
        "w_pool": nrm((L, POOL_GROUPS, POOL_GW, POOL_GW), POOL_GW ** -0.5),
        "ls_pool": 1.0 + nrm((L, POOL_WIDTH), 0.1),
        "rwkv_mu": jax.random.uniform(next(kit), (L, RWKV_PROJ), f32, 0.0, 1.0),
        "rwkv_w0": nrm((L, RWKV_WIDTH), 0.5),
        "rwkv_w2": nrm((L, DECAY_LORA, RWKV_WIDTH), 0.5 * DECAY_LORA ** -0.5),
        "rwkv_a0": nrm((L, RWKV_WIDTH), 0.5),
        "rwkv_a2": nrm((L, AAA_LORA, RWKV_WIDTH), AAA_LORA ** -0.5),
        "rwkv_g2": nrm((L, GATE_LORA, RWKV_WIDTH), GATE_LORA ** -0.5),
        "rwkv_kk": 0.85 + nrm((L, RWKV_WIDTH), 0.05),
        "rwkv_ka": 1.0 + nrm((L, RWKV_WIDTH), 0.05),
        "rwkv_rk": nrm((L, RWKV_HEADS, RWKV_HEAD), 0.1),
        "rwkv_ln_g": 1.0 + nrm((L, RWKV_WIDTH), 0.02),
        "rwkv_ln_b": nrm((L, RWKV_WIDTH), 0.02),
        "conv_w": nrm((L, CONV_K, CONV_WIDTH), CONV_K ** -0.5),
        "w_branch": w_branch,
        "w_out": nrm((L, D, D), D ** -0.5),
        "w_router": nrm((D, N_EXPERTS), D ** -0.5),
        "b_router": nrm((N_EXPERTS,), 0.01),
        "w_gate": nrm((L, N_EXPERTS, D, D_EXPERT), D ** -0.5),
        "w_up": nrm((L, N_EXPERTS, D, D_EXPERT), D ** -0.5),
        "w_down": nrm((L, N_EXPERTS, D_EXPERT, D), D_EXPERT ** -0.5),
        "g_final": 1.0 + nrm((D,), 0.02),
    }


def reference(x_prompt, x_sample, cache_swa_k, cache_swa_v, state_pool, state_rwkv_shift, state_rwkv_wkv,
              state_conv, c_prompt, c_sample, w_ada, b_ada, g_norm1, g_norm2, w_in, sinks, w_pool, ls_pool,
              rwkv_mu, rwkv_w0, rwkv_w2, rwkv_a0, rwkv_a2, rwkv_g2, rwkv_kk, rwkv_ka, rwkv_rk, rwkv_ln_g,
              rwkv_ln_b, conv_w, w_branch, w_out, w_router, b_router, w_gate, w_up, w_down, g_final):
    xp, xs = x_prompt, x_sample
    prompt_states, sample_states = [], []
    for l in range(DEPTH):
        lp = {
            "w_ada": w_ada[l], "b_ada": b_ada[l], "g1": g_norm1[l], "g2": g_norm2[l], "w_in": w_in[l],
            "sinks": sinks[l], "w_pool": w_pool[l], "ls_pool": ls_pool[l], "mu": rwkv_mu[l],
            "w0": rwkv_w0[l], "w2": rwkv_w2[l], "a0": rwkv_a0[l], "a2": rwkv_a2[l], "g2r": rwkv_g2[l],
            "kk": rwkv_kk[l], "ka": rwkv_ka[l], "rk": rwkv_rk[l], "ln_g": rwkv_ln_g[l], "ln_b": rwkv_ln_b[l],
            "conv_w": conv_w[l], "w_branch": w_branch[l], "w_out": w_out[l],
            "w_gate": w_gate[l], "w_up": w_up[l], "w_down": w_down[l],
        }
        xp, st_p = trunk_layer(xp, c_prompt, lp, None, 0, w_router, b_router)
        past = (cache_swa_k[l], cache_swa_v[l], state_pool[l], state_rwkv_shift[l], state_rwkv_wkv[l], state_conv[l])
        xs, st_s = trunk_layer(xs, c_sample, lp, past, PAST_LEN, w_router, b_router)
        prompt_states.append(st_p)
        sample_states.append(st_s)
    y_prompt = rmsnorm(xp, g_final)
    y_sample = rmsnorm(xs, g_final)
    swa_k_prompt = jnp.stack([s[0] for s in prompt_states])
    swa_v_prompt = jnp.stack([s[1] for s in prompt_states])
    pool_prompt = jnp.stack([s[2] for s in prompt_states])
    rwkv_shift_prompt = jnp.stack([s[3] for s in prompt_states])
    rwkv_wkv_prompt = jnp.stack([s[4] for s in prompt_states])
    conv_prompt = jnp.stack([s[5] for s in prompt_states])
    swa_k_sample = jnp.stack([s[0] for s in sample_states])
    swa_v_sample = jnp.stack([s[1] for s in sample_states])
    pool_sample = jnp.stack([s[2] for s in sample_states])
    rwkv_shift_sample = jnp.stack([s[3] for s in sample_states])
    rwkv_wkv_sample = jnp.stack([s[4] for s in sample_states])
    conv_sample = jnp.stack([s[5] for s in sample_states])
    return (y_prompt, y_sample, swa_k_prompt, swa_v_prompt, pool_prompt, rwkv_shift_prompt, rwkv_wkv_prompt,
            conv_prompt, swa_k_sample, swa_v_sample, pool_sample, rwkv_shift_sample, rwkv_wkv_sample, conv_sample)
```

```python
import functools

import jax
import jax.numpy as jnp
from jax import lax
from jax.experimental import pallas as pl
from jax.experimental.pallas import tpu as pltpu

F32 = jnp.float32
BF16 = jnp.bfloat16

D_MODEL = 2048
PAST_LEN = 8192
WINDOW = 128
HEAD_DIM = 64
N_HEADS = 16
N_KV_HEADS = 4
GQA_GROUP = N_HEADS // N_KV_HEADS
ATTN_WIDTH = N_HEADS * HEAD_DIM
KV_WIDTH = N_KV_HEADS * HEAD_DIM
ATTN_SCALE = HEAD_DIM ** -0.5
NEG_INF = -1e30
POOL_WINDOWS = (2, 4, 8, 16)
POOL_WIDTH = 768
POOL_GW = POOL_WIDTH // len(POOL_WINDOWS)
POOL_BUF = max(POOL_WINDOWS) - 1
RWKV_HEAD = 64
RWKV_WIDTH = 768
RWKV_HEADS = RWKV_WIDTH // RWKV_HEAD
DECAY_LORA = 64
AAA_LORA = 64
GATE_LORA = 128
RWKV_PROJ = 3 * RWKV_WIDTH + DECAY_LORA + AAA_LORA + GATE_LORA
RWKV_LN_EPS = 64e-5
CONV_WIDTH = 768
CONV_K = 3
N_BRANCH = 4
BRANCH_SECTIONS = (ATTN_WIDTH, POOL_WIDTH, RWKV_WIDTH, CONV_WIDTH)
MIX_WIDTH = sum(BRANCH_SECTIONS)
N_EXPERTS = 16
N_GROUPS = 4
EXP_PER_GROUP = N_EXPERTS // N_GROUPS
D_EXPERT = 1024
RMS_EPS = 1e-6

OFF_Q = 0
OFF_K = OFF_Q + ATTN_WIDTH
OFF_V = OFF_K + KV_WIDTH
OFF_U = OFF_V + KV_WIDTH
OFF_PC = OFF_U + POOL_WIDTH
OFF_CB = OFF_PC + RWKV_PROJ
OFF_CC = OFF_CB + CONV_WIDTH
OFF_CX = OFF_CC + CONV_WIDTH
OFF_GL = OFF_CX + CONV_WIDTH
IN_WIDTH = OFF_GL + N_BRANCH * D_MODEL

LANES = 128
ROUTER_PAD = LANES
VMEM_LIMIT = 60 * 1024 * 1024


def _cp(*sem):
    return pltpu.CompilerParams(dimension_semantics=sem, vmem_limit_bytes=VMEM_LIMIT)


def _sigmoid(x):
    return 1.0 / (1.0 + jnp.exp(-x))


def _split(x):
    hi = x.astype(BF16)
    return hi, (x.astype(F32) - hi.astype(F32)).astype(BF16)


def _mxu(contract, a, b, precise):
    if not precise:
        return contract(a.astype(BF16), b.astype(BF16))
    ah, al = _split(a)
    bh, bl = _split(b)
    return contract(ah, bh) + (contract(ah, bl) + contract(al, bh))


def _dot2(a, b):
    return jnp.dot(a, b, preferred_element_type=F32)


def _bdot(a, b, precise=False):
    return _mxu(_dot2, a, b, precise)


def _act_dtype(precise):
    return F32 if precise else BF16


def _ada_kernel(c_ref, w_ref, b_ref, o_ref, *, precise):
    c = c_ref[...]
    o_ref[...] = _bdot(c * _sigmoid(c), w_ref[...], precise) + b_ref[...]


def _ada(c_all, w_ada, b_ada, layer, precise):
    depth, d, n = w_ada.shape
    nb = c_all.shape[0]
    tn = 1024
    return pl.pallas_call(
        functools.partial(_ada_kernel, precise=precise),
        grid=(n // tn,),
        in_specs=[pl.BlockSpec((nb, d), lambda j: (0, 0)),
                  pl.BlockSpec((None, d, tn), lambda j: (layer, 0, j)),
                  pl.BlockSpec((None, 1, tn), lambda j: (layer, 0, j))],
        out_specs=pl.BlockSpec((nb, tn), lambda j: (0, j)),
        out_shape=jax.ShapeDtypeStruct((nb, n), F32),
        compiler_params=_cp("parallel"),
        name="ada",
    )(c_all, w_ada, b_ada.reshape(depth, 1, n))


class _Group:
    def __init__(self, rows, rpm, mod_rows):
        self.rows, self.rpm, self.mod_rows = rows, rpm, mod_rows

    def mod_spec(self, tm, width, col_of, row_axis=0):
        if self.mod_rows == 1:
            per = self.rpm // tm
            return pl.BlockSpec((None, 1, width), lambda *idx: (idx[row_axis] // per, 0, col_of(idx)))
        assert tm == self.rpm == self.mod_rows
        return pl.BlockSpec((None, tm, width), lambda *idx: (idx[row_axis], 0, col_of(idx)))


def _route(y, wr, br):
    y_hi = y.astype(BF16)
    y_lo = (y - y_hi.astype(F32)).astype(BF16)
    w_hi = wr.astype(BF16)
    w_lo = (wr - w_hi.astype(F32)).astype(BF16)
    logits = (jnp.dot(y_hi, w_hi, preferred_element_type=F32)
              + (jnp.dot(y_hi, w_lo, preferred_element_type=F32) + jnp.dot(y_lo, w_hi, preferred_element_type=F32)))
    logits = logits[:, :N_EXPERTS] + br
    tm = logits.shape[0]
    e = jnp.exp(logits - jnp.max(logits, axis=-1, keepdims=True))
    probs = e / jnp.sum(e, axis=-1, keepdims=True)
    iota_g = lax.broadcasted_iota(jnp.int32, (tm, EXP_PER_GROUP), 1)
    best = None
    for g in range(N_GROUPS):
        pg = probs[:, g * EXP_PER_GROUP:(g + 1) * EXP_PER_GROUP]
        m1 = jnp.max(pg, axis=-1, keepdims=True)
        i1 = jnp.min(jnp.where(pg == m1, iota_g, EXP_PER_GROUP), axis=-1, keepdims=True)
        rest = jnp.where(iota_g == i1, -1.0, pg)
        m2 = jnp.max(rest, axis=-1, keepdims=True)
        i2 = jnp.min(jnp.where(rest == m2, iota_g, EXP_PER_GROUP), axis=-1, keepdims=True)
        cand = (m1 + m2, m1, m2, i1 + g * EXP_PER_GROUP, i2 + g * EXP_PER_GROUP)
        if best is None:
            best = cand
        else:
            take = cand[0] > best[0]
            best = tuple(jnp.where(take, c, b) for c, b in zip(cand, best))
    _, m1, m2, e1, e2 = best
    den = m1 + m2
    iota_e = lax.broadcasted_iota(jnp.int32, (tm, N_EXPERTS), 1)
    return jnp.where(iota_e == e1, m1 / den, 0.0) + jnp.where(iota_e == e2, m2 / den, 0.0)


def _normx_kernel(*refs, has_add, has_mod, has_route, emit_x):
    it = iter(refs)
    x_ref = next(it)
    if has_add:
        m_ref, gate_ref = next(it), next(it)
    g_ref = next(it)
    if has_mod:
        sc_ref, sh_ref = next(it), next(it)
    if has_route:
        wr_ref, br_ref = next(it), next(it)
    if emit_x:
        xo_ref = next(it)
    h_ref = next(it)
    if has_route:
        comb_ref = next(it)
    x = x_ref[...]
    if has_add:
        x = x + gate_ref[...] * m_ref[...]
    if emit_x:
        xo_ref[...] = x
    y = x * lax.rsqrt(jnp.mean(x * x, axis=-1, keepdims=True) + RMS_EPS) * g_ref[...]
    if has_mod:
        y = y * (1.0 + sc_ref[...]) + sh_ref[...]
    h_ref[...] = y.astype(h_ref.dtype)
    if has_route:
        comb_ref[...] = _route(y, wr_ref[...], br_ref[...])


def _normx(grp, x, g, *, add=None, mod=None, route=None, emit_x=False, out_dtype=BF16, tm=512):
    n, d = x.shape
    row = pl.BlockSpec((tm, d), lambda i: (i, 0))
    vec = pl.BlockSpec((1, d), lambda i: (0, 0))
    args, specs = [x], [row]
    if add is not None:
        m, mod3, chunk = add
        args += [m, mod3]
        specs += [row, grp.mod_spec(tm, d, lambda idx, c=chunk: c)]
    args.append(g.reshape(1, d))
    specs.append(vec)
    if mod is not None:
        mod3, c_sc, c_sh = mod
        args += [mod3, mod3]
        specs += [grp.mod_spec(tm, d, lambda idx, c=c_sc: c), grp.mod_spec(tm, d, lambda idx, c=c_sh: c)]
    if route is not None:
        wr, br = route
        args += [wr, br.reshape(1, N_EXPERTS)]
        specs += [pl.BlockSpec((d, ROUTER_PAD), lambda i: (0, 0)), pl.BlockSpec((1, N_EXPERTS), lambda i: (0, 0))]
    out_shape, out_specs = [], []
    if emit_x:
        out_shape.append(jax.ShapeDtypeStruct((n, d), F32))
        out_specs.append(row)
    out_shape.append(jax.ShapeDtypeStruct((n, d), out_dtype))
    out_specs.append(row)
    if route is not None:
        out_shape.append(jax.ShapeDtypeStruct((n, N_EXPERTS), F32))
        out_specs.append(pl.BlockSpec((tm, N_EXPERTS), lambda i: (i, 0)))
    return pl.pallas_call(
        functools.partial(_normx_kernel, has_add=add is not None, has_mod=mod is not None,
                          has_route=route is not None, emit_x=emit_x),
        grid=(n // tm,), in_specs=specs, out_specs=out_specs, out_shape=out_shape,
        compiler_params=_cp("parallel"), name="normx",
    )(*args)


def _mm_kernel(a_ref, w_ref, o_ref, *, precise):
    o_ref[...] = _bdot(a_ref[...], w_ref[...], precise).astype(o_ref.dtype)


def _mm(a, w3, layer, *, tm, tn, precise, out_dtype=F32):
    m, k = a.shape
    n = w3.shape[-1]
    return pl.pallas_call(
        functools.partial(_mm_kernel, precise=precise),
        grid=(m // tm, n // tn),
        in_specs=[pl.BlockSpec((tm, k), lambda i, j: (i, 0)),
                  pl.BlockSpec((None, k, tn), lambda i, j: (layer, 0, j))],
        out_specs=pl.BlockSpec((tm, tn), lambda i, j: (i, j)),
        out_shape=jax.ShapeDtypeStruct((m, n), out_dtype),
        compiler_params=_cp("parallel", "parallel"), name="mm_in",
    )(a, w3)


def _mm_res_kernel(a_ref, w_ref, x_ref, gate_ref, o_ref, *, precise):
    o_ref[...] = x_ref[...] + gate_ref[...] * _bdot(a_ref[...], w_ref[...], precise)


def _mm_res(grp, a, w3, layer, x, mod3, gate_chunk, *, tm, tn, precise):
    m, k = a.shape
    n = w3.shape[-1]
    per_chunk = n // tn
    return pl.pallas_call(
        functools.partial(_mm_res_kernel, precise=precise),
        grid=(m // tm, n // tn),
        in_specs=[pl.BlockSpec((tm, k), lambda i, j: (i, 0)),
                  pl.BlockSpec((None, k, tn), lambda i, j: (layer, 0, j)),
                  pl.BlockSpec((tm, tn), lambda i, j: (i, j)),
                  grp.mod_spec(tm, tn, lambda idx: gate_chunk * per_chunk + idx[1])],
        out_specs=pl.BlockSpec((tm, tn), lambda i, j: (i, j)),
        out_shape=jax.ShapeDtypeStruct((m, n), F32),
        compiler_params=_cp("parallel", "parallel"), name="mm_out",
    )(a, w3, x, mod3)


def _merge_kernel(a_ref, b_ref, c_ref, d_ref, g0_ref, g1_ref, g2_ref, g3_ref, w_ref, o_ref, *, precise):
    acc = None
    lo = 0
    for br_ref, g_ref, width in zip((a_ref, b_ref, c_ref, d_ref), (g0_ref, g1_ref, g2_ref, g3_ref), BRANCH_SECTIONS):
        t = _sigmoid(g_ref[...]) * _bdot(br_ref[...], w_ref[lo:lo + width, :], precise)
        acc = t if acc is None else acc + t
        lo += width
    o_ref[...] = acc.astype(o_ref.dtype)


def _merge(branches, p, w_branch, layer, *, tm, tn, precise):
    m = p.shape[0]
    gl_blk = OFF_GL // tn
    per = D_MODEL // tn
    br_specs = [pl.BlockSpec((tm, w), lambda i, j: (i, 0)) for w in BRANCH_SECTIONS]
    gl_specs = [pl.BlockSpec((tm, tn), lambda i, j, b=b: (i, gl_blk + b * per + j)) for b in range(N_BRANCH)]
    return pl.pallas_call(
        functools.partial(_merge_kernel, precise=precise),
        grid=(m // tm, D_MODEL // tn),
        in_specs=br_specs + gl_specs + [pl.BlockSpec((None, MIX_WIDTH, tn), lambda i, j: (layer, 0, j))],
        out_specs=pl.BlockSpec((tm, tn), lambda i, j: (i, j)),
        out_shape=jax.ShapeDtypeStruct((m, D_MODEL), _act_dtype(precise)),
        compiler_params=_cp("parallel", "parallel"), name="merge",
    )(*branches, p, p, p, p, w_branch)


def _sink_col(sink_ref, layer, kh, rows_per_head):
    return jnp.concatenate([jnp.full((rows_per_head, 1), sink_ref[layer, kh * GQA_GROUP + g], F32)
                            for g in range(GQA_GROUP)], axis=0)


def _dot_nt(a, b):
    return lax.dot_general(a, b, (((1,), (1,)), ((), ())), preferred_element_type=F32)


def _swa_prompt_kernel(sink_ref, q_ref, kc_ref, kp_ref, vc_ref, vp_ref, o_ref, *, layer, precise):
    n = pl.program_id(1)
    w = WINDOW
    q = q_ref[...]
    qi = jnp.bitwise_and(lax.broadcasted_iota(jnp.int32, (GQA_GROUP * w, 2 * w), 0), w - 1)
    sj = lax.broadcasted_iota(jnp.int32, (GQA_GROUP * w, 2 * w), 1)
    valid = (sj >= qi) & (sj <= qi + w) & ((sj >= w) | (n > 0))
    outs = []
    for kh in range(N_KV_HEADS):
        sl = slice(kh * HEAD_DIM, (kh + 1) * HEAD_DIM)
        k2 = jnp.concatenate([kp_ref[:, sl], kc_ref[:, sl]], axis=0)
        v2 = jnp.concatenate([vp_ref[:, sl], vc_ref[:, sl]], axis=0)
        q4 = jnp.concatenate([q[:, (kh * GQA_GROUP + g) * HEAD_DIM:(kh * GQA_GROUP + g + 1) * HEAD_DIM]
                              for g in range(GQA_GROUP)], axis=0)
        s = _mxu(_dot_nt, q4, k2, precise) * ATTN_SCALE
        s = jnp.where(valid, s, NEG_INF)
        sk = _sink_col(sink_ref, layer, kh, w)
        m = jnp.maximum(jnp.max(s, axis=-1, keepdims=True), sk)
        e = jnp.exp(s - m)
        p = e / (jnp.sum(e, axis=-1, keepdims=True) + jnp.exp(sk - m))
        o4 = _bdot(p, v2, precise)
        outs += [o4[g * w:(g + 1) * w] for g in range(GQA_GROUP)]
    o_ref[...] = jnp.concatenate(outs, axis=1).astype(o_ref.dtype)


def _swa_prompt(p, sinks, layer, batch, seq, precise):
    nb = seq // WINDOW
    kblk, vblk = OFF_K // KV_WIDTH, OFF_V // KV_WIDTH

    def cur(col):
        return lambda b, n: (b * nb + n, col)

    def prev(col):
        return lambda b, n: (b * nb + jnp.maximum(n - 1, 0), col)

    return pl.pallas_call(
        functools.partial(_swa_prompt_kernel, layer=layer, precise=precise),
        grid=(batch, nb),
        in_specs=[pl.BlockSpec(memory_space=pltpu.SMEM),
                  pl.BlockSpec((WINDOW, ATTN_WIDTH), cur(0)),
                  pl.BlockSpec((WINDOW, KV_WIDTH), cur(kblk)), pl.BlockSpec((WINDOW, KV_WIDTH), prev(kblk)),
                  pl.BlockSpec((WINDOW, KV_WIDTH), cur(vblk)), pl.BlockSpec((WINDOW, KV_WIDTH), prev(vblk))],
        out_specs=pl.BlockSpec((WINDOW, ATTN_WIDTH), lambda b, n: (b * nb + n, 0)),
        out_shape=jax.ShapeDtypeStruct((batch * seq, ATTN_WIDTH), _act_dtype(precise)),
        compiler_params=_cp("parallel", "parallel"), name="swa_prompt",
    )(sinks, p, p, p, p, p)


def _qk(a, b):
    return jnp.einsum("bqd,bkd->bqk", a, b, preferred_element_type=F32)


def _pv(a, b):
    return jnp.einsum("bqk,bkd->bqd", a, b, preferred_element_type=F32)


def _swa_sample_kernel(sink_ref, q_ref, kn_ref, vn_ref, kc_ref, vc_ref, o_ref, ko_ref, vo_ref, *, layer, steps, wbuf,
                       precise):
    q = q_ref[...]
    kn, vn = kn_ref[...], vn_ref[...]
    kc, vc = kc_ref[...], vc_ref[...]
    ko_ref[:, :wbuf - steps, :] = kc[:, steps:, :]
    ko_ref[:, wbuf - steps:, :] = kn
    vo_ref[:, :wbuf - steps, :] = vc[:, steps:, :]
    vo_ref[:, wbuf - steps:, :] = vn
    bb = q.shape[0]
    rows = GQA_GROUP * steps
    t_c = lax.rem(lax.broadcasted_iota(jnp.int32, (bb, rows, wbuf), 1), steps)
    j_c = lax.broadcasted_iota(jnp.int32, (bb, rows, wbuf), 2)
    dist_c = t_c + wbuf - j_c
    valid_c = (dist_c >= 0) & (dist_c <= WINDOW)
    t_n = lax.rem(lax.broadcasted_iota(jnp.int32, (bb, rows, steps), 1), steps)
    j_n = lax.broadcasted_iota(jnp.int32, (bb, rows, steps), 2)
    valid_n = (t_n - j_n >= 0) & (t_n - j_n <= WINDOW)
    outs = [None] * N_HEADS
    for kh in range(N_KV_HEADS):
        sl = slice(kh * HEAD_DIM, (kh + 1) * HEAD_DIM)
        qg = jnp.concatenate([q[:, :, (kh * GQA_GROUP + g) * HEAD_DIM:(kh * GQA_GROUP + g + 1) * HEAD_DIM]
                              for g in range(GQA_GROUP)], axis=1)
        s_c = _mxu(_qk, qg, kc[:, :, sl], precise) * ATTN_SCALE
        s_n = _mxu(_qk, qg, kn[:, :, sl], precise) * ATTN_SCALE
        s_c = jnp.where(valid_c, s_c, NEG_INF)
        s_n = jnp.where(valid_n, s_n, NEG_INF)
        sk = _sink_col(sink_ref, layer, kh, steps)[None]
        m = jnp.maximum(jnp.maximum(jnp.max(s_c, axis=-1, keepdims=True), jnp.max(s_n, axis=-1, keepdims=True)), sk)
        e_c, e_n = jnp.exp(s_c - m), jnp.exp(s_n - m)
        den = jnp.sum(e_c, axis=-1, keepdims=True) + jnp.sum(e_n, axis=-1, keepdims=True) + jnp.exp(sk - m)
        o = _mxu(_pv, e_c / den, vc[:, :, sl], precise) + _mxu(_pv, e_n / den, vn[:, :, sl], precise)
        for g in range(GQA_GROUP):
            outs[kh * GQA_GROUP + g] = o[:, g * steps:(g + 1) * steps, :]
    o_ref[...] = jnp.concatenate(outs, axis=2).astype(o_ref.dtype)


def _swa_sample(p3, cache_k, cache_v, sinks, layer, precise, *, bb=8):
    batch, steps, _ = p3.shape
    wbuf = cache_k.shape[2]
    kblk, vblk = OFF_K // KV_WIDTH, OFF_V // KV_WIDTH
    cache_spec = pl.BlockSpec((None, bb, wbuf, KV_WIDTH), lambda i: (layer, i, 0, 0))
    new_spec = pl.BlockSpec((bb, wbuf, KV_WIDTH), lambda i: (i, 0, 0))
    return pl.pallas_call(
        functools.partial(_swa_sample_kernel, layer=layer, steps=steps, wbuf=wbuf, precise=precise),
        grid=(batch // bb,),
        in_specs=[pl.BlockSpec(memory_space=pltpu.SMEM),
                  pl.BlockSpec((bb, steps, ATTN_WIDTH), lambda i: (i, 0, 0)),
                  pl.BlockSpec((bb, steps, KV_WIDTH), lambda i: (i, 0, kblk)),
                  pl.BlockSpec((bb, steps, KV_WIDTH), lambda i: (i, 0, vblk)),
                  cache_spec, cache_spec],
        out_specs=[pl.BlockSpec((bb, steps, ATTN_WIDTH), lambda i: (i, 0, 0)), new_spec, new_spec],
        out_shape=[jax.ShapeDtypeStruct((batch, steps, ATTN_WIDTH), _act_dtype(precise)),
                   jax.ShapeDtypeStruct((batch, wbuf, KV_WIDTH), F32),
                   jax.ShapeDtypeStruct((batch, wbuf, KV_WIDTH), F32)],
        compiler_params=_cp("parallel"), name="swa_sample",
    )(sinks, p3, p3, p3, cache_k, cache_v)


def _shift_rows(x, k):
    rows = lax.broadcasted_iota(jnp.int32, x.shape, 0)
    return jnp.where(rows >= k, pltpu.roll(x, k, axis=0), 0.0)


def _pool_prompt_kernel(u_ref, w_ref, ls_ref, o_ref, *, precise):
    u = u_ref[...]
    t = u.shape[0]
    pos1 = (lax.broadcasted_iota(jnp.int32, (t, 1), 0) + 1).astype(F32)
    sums = {1: u}
    win = 1
    while win < max(POOL_WINDOWS):
        sums[2 * win] = sums[win] + _shift_rows(sums[win], win)
        win *= 2
    outs = []
    for gi, win in enumerate(POOL_WINDOWS):
        sl = slice(gi * POOL_GW, (gi + 1) * POOL_GW)
        cnt = jnp.minimum(float(win), pos1)
        d = sums[win][:, sl] / cnt - u[:, sl]
        outs.append(_bdot(d, w_ref[gi], precise))
    o_ref[...] = (jnp.concatenate(outs, axis=1) * ls_ref[...]).astype(o_ref.dtype)


def _pool_prompt(p, w_pool, ls_pool, layer, batch, seq, precise):
    gw = POOL_GW
    return pl.pallas_call(
        functools.partial(_pool_prompt_kernel, precise=precise),
        grid=(batch,),
        in_specs=[pl.BlockSpec((seq, POOL_WIDTH), lambda b: (b, OFF_U // POOL_WIDTH)),
                  pl.BlockSpec((None, len(POOL_WINDOWS), gw, gw), lambda b: (layer, 0, 0, 0)),
                  pl.BlockSpec((None, 1, POOL_WIDTH), lambda b: (layer, 0, 0))],
        out_specs=pl.BlockSpec((seq, POOL_WIDTH), lambda b: (b, 0)),
        out_shape=jax.ShapeDtypeStruct((batch * seq, POOL_WIDTH), _act_dtype(precise)),
        compiler_params=_cp("parallel"), name="pool_prompt",
    )(p, w_pool, ls_pool.reshape(ls_pool.shape[0], 1, POOL_WIDTH))


def _pool_sample_kernel(u_ref, past_ref, w_ref, ls_ref, o_ref, new_ref, *, pos0, precise):
    steps, hist = u_ref.shape[0], past_ref.shape[0]
    full = [past_ref[i] for i in range(hist)] + [u_ref[i] for i in range(steps)]
    for i in range(hist):
        new_ref[i] = full[steps + i]
    ds = [[] for _ in POOL_WINDOWS]
    for t in range(steps):
        for gi, win in enumerate(POOL_WINDOWS):
            sl = slice(gi * POOL_GW, (gi + 1) * POOL_GW)
            wsum = full[hist + t][:, sl]
            for s in range(1, win):
                wsum = wsum + full[hist + t - s][:, sl]
            cnt = float(min(win, pos0 + t + 1))
            ds[gi].append(wsum / cnt - full[hist + t][:, sl])
    ys = [_bdot(jnp.concatenate(ds[gi], axis=0), w_ref[gi], precise) for gi in range(len(POOL_WINDOWS))]
    y = jnp.concatenate(ys, axis=1) * ls_ref[...]
    nb = u_ref.shape[1]
    for t in range(steps):
        o_ref[t] = y[t * nb:(t + 1) * nb].astype(o_ref.dtype)


def _pool_sample(u_t, past_t, w_pool, ls_pool, layer, pos0, precise):
    steps, nb, _ = u_t.shape
    return pl.pallas_call(
        functools.partial(_pool_sample_kernel, pos0=pos0, precise=precise),
        grid=(1,),
        in_specs=[pl.BlockSpec(u_t.shape, lambda i: (0, 0, 0)),
                  pl.BlockSpec(past_t.shape, lambda i: (0, 0, 0)),
                  pl.BlockSpec((None, len(POOL_WINDOWS), POOL_GW, POOL_GW), lambda i: (layer, 0, 0, 0)),
                  pl.BlockSpec((None, 1, POOL_WIDTH), lambda i: (layer, 0, 0))],
        out_specs=[pl.BlockSpec(u_t.shape, lambda i: (0, 0, 0)), pl.BlockSpec(past_t.shape, lambda i: (0, 0, 0))],
        out_shape=[jax.ShapeDtypeStruct(u_t.shape, _act_dtype(precise)), jax.ShapeDtypeStruct(past_t.shape, F32)],
        compiler_params=_cp("arbitrary"), name="pool_sample",
    )(u_t, past_t, w_pool, ls_pool.reshape(ls_pool.shape[0], 1, POOL_WIDTH))


def _conv_prompt_kernel(cb_ref, cc_ref, cx_ref, w_ref, o_ref, new_ref):
    z = cc_ref[...] * cx_ref[...]
    w = w_ref[...]
    y = w[CONV_K - 1:CONV_K] * z
    for j in range(1, CONV_K):
        y = y + w[CONV_K - 1 - j:CONV_K - j] * _shift_rows(z, j)
    o_ref[...] = (cb_ref[...] * y).astype(o_ref.dtype)
    new_ref[...] = z[z.shape[0] - (CONV_K - 1):]


def _conv_prompt(p, conv_w, layer, batch, seq, precise, *, tc=256):
    nc = CONV_WIDTH // tc

    def col(off):
        return lambda b, c: (b, off // tc + c)

    return pl.pallas_call(
        _conv_prompt_kernel,
        grid=(batch, nc),
        in_specs=[pl.BlockSpec((seq, tc), col(OFF_CB)), pl.BlockSpec((seq, tc), col(OFF_CC)),
                  pl.BlockSpec((seq, tc), col(OFF_CX)),
                  pl.BlockSpec((None, CONV_K, tc), lambda b, c: (layer, 0, c))],
        out_specs=[pl.BlockSpec((seq, tc), lambda b, c: (b, c)),
                   pl.BlockSpec((None, CONV_K - 1, tc), lambda b, c: (b, 0, c))],
        out_shape=[jax.ShapeDtypeStruct((batch * seq, CONV_WIDTH), _act_dtype(precise)),
                   jax.ShapeDtypeStruct((batch, CONV_K - 1, CONV_WIDTH), F32)],
        compiler_params=_cp("parallel", "parallel"), name="conv_prompt",
    )(p, p, p, conv_w)


def _conv_sample_kernel(cb_ref, cc_ref, cx_ref, past_ref, w_ref, o_ref, new_ref):
    steps, hist = cb_ref.shape[0], past_ref.shape[0]
    w = w_ref[...]
    full = [past_ref[i] for i in range(hist)] + [cc_ref[t] * cx_ref[t] for t in range(steps)]
    for t in range(steps):
        y = w[0:1] * full[t]
        for j in range(1, CONV_K):
            y = y + w[j:j + 1] * full[t + j]
        o_ref[t] = (cb_ref[t] * y).astype(o_ref.dtype)
    for i in range(hist):
        new_ref[i] = full[steps + i]


def _conv_sample(cb_t, cc_t, cx_t, past_t, conv_w, layer, precise):
    full3 = lambda shape: pl.BlockSpec(shape, lambda i: (0, 0, 0))
    return pl.pallas_call(
        _conv_sample_kernel,
        grid=(1,),
        in_specs=[full3(cb_t.shape), full3(cc_t.shape), full3(cx_t.shape), full3(past_t.shape),
                  pl.BlockSpec((None, CONV_K, CONV_WIDTH), lambda i: (layer, 0, 0))],
        out_specs=[full3(cb_t.shape), full3(past_t.shape)],
        out_shape=[jax.ShapeDtypeStruct(cb_t.shape, _act_dtype(precise)), jax.ShapeDtypeStruct(past_t.shape, F32)],
        compiler_params=_cp("arbitrary"), name="conv_sample",
    )(cb_t, cc_t, cx_t, past_t, conv_w)


def _head_sum(x):
    rows = x.shape[0]
    return jnp.concatenate(
        [jnp.broadcast_to(jnp.sum(x[:, h * RWKV_HEAD:(h + 1) * RWKV_HEAD], axis=-1, keepdims=True), (rows, RWKV_HEAD))
         for h in range(RWKV_HEADS)], axis=1)


def _softplus(x):
    return jnp.maximum(x, 0.0) + jnp.log(1.0 + jnp.exp(-jnp.abs(x)))


def _rwkv_pre_core(cur, sh, mu, w0, w2, a0, a2, g2, k_k, k_a, outs, precise):
    xr, xk, xv, xwa, xg = [c + (s - c) * m for c, s, m in zip(cur, sh, mu)]
    wd, ad = xwa[:, :DECAY_LORA], xwa[:, DECAY_LORA:]
    w_log = -_softplus(-(w0 + _bdot(jnp.tanh(wd), w2, precise))) - 0.5
    decay = jnp.exp(-jnp.exp(w_log))
    a = _sigmoid(a0 + _bdot(ad, a2, precise))
    g = _bdot(_sigmoid(xg), g2, precise)
    kk = xk * k_k
    kk = kk / jnp.maximum(jnp.sqrt(_head_sum(kk * kk)), 1e-12)
    kf = xk * (1.0 + (a - 1.0) * k_a)
    r_ref, w_ref, k_ref, v_ref, a_ref, b_ref, g_ref = outs
    r_ref[...] = xr
    w_ref[...] = decay
    k_ref[...] = kf
    v_ref[...] = xv
    a_ref[...] = -kk
    b_ref[...] = kk * a
    g_ref[...] = g


_PRE_WIDTHS = (RWKV_WIDTH, RWKV_WIDTH, RWKV_WIDTH, DECAY_LORA + AAA_LORA, GATE_LORA)
_PRE_OFFS = (0, RWKV_WIDTH, 2 * RWKV_WIDTH, 3 * RWKV_WIDTH, 3 * RWKV_WIDTH + DECAY_LORA + AAA_LORA)
_HALO = 8


def _rwkv_pre_prompt_kernel(*refs, precise):
    cur_refs, halo_refs, mu_refs = refs[0:5], refs[5:10], refs[10:15]
    w0, w2, a0, a2, g2, k_k, k_a = [r[...] for r in refs[15:22]]
    outs = refs[22:]
    first = pl.program_id(1) == 0
    cur, sh = [], []
    for c_ref, h_ref in zip(cur_refs, halo_refs):
        c = c_ref[...]
        prev_row = jnp.where(first, 0.0, h_ref[_HALO - 1:_HALO, :])
        rows = lax.broadcasted_iota(jnp.int32, c.shape, 0)
        sh.append(jnp.where(rows == 0, prev_row, pltpu.roll(c, 1, axis=0)))
        cur.append(c)
    _rwkv_pre_core(cur, sh, [m[...] for m in mu_refs], w0, w2, a0, a2, g2, k_k, k_a, outs, precise)


def _rwkv_param_specs(layer):
    def spec(shape):
        return pl.BlockSpec((None,) + shape, lambda *idx: (layer,) + (0,) * len(shape))

    return [spec((1, RWKV_WIDTH)), spec((DECAY_LORA, RWKV_WIDTH)), spec((1, RWKV_WIDTH)),
            spec((AAA_LORA, RWKV_WIDTH)), spec((GATE_LORA, RWKV_WIDTH)), spec((1, RWKV_WIDTH)), spec((1, RWKV_WIDTH))]


def _rwkv_params(prm):
    depth = prm["w0"].shape[0]
    r3 = lambda a: a.reshape(depth, 1, RWKV_WIDTH)
    return [r3(prm["w0"]), prm["w2"], r3(prm["a0"]), prm["a2"], prm["g2"], r3(prm["kk"]), r3(prm["ka"])]


def _rwkv_pre_prompt(p, prm, layer, batch, seq, precise, *, tt=512):
    nt = seq // tt
    cur_specs, halo_specs, mu_specs = [], [], []
    for w, off in zip(_PRE_WIDTHS, _PRE_OFFS):
        cb = (OFF_PC + off) // w
        cur_specs.append(pl.BlockSpec((tt, w), lambda b, t, cb=cb: (b * nt + t, cb)))
        halo_specs.append(pl.BlockSpec(
            (_HALO, w), lambda b, t, cb=cb: (jnp.maximum((b * nt + t) * (tt // _HALO) - 1, 0), cb)))
        mu_specs.append(pl.BlockSpec((None, 1, w), lambda b, t, mb=off // w: (layer, 0, mb)))
    out_spec = pl.BlockSpec((tt, RWKV_WIDTH), lambda b, t: (b * nt + t, 0))
    mu3 = prm["mu"].reshape(prm["mu"].shape[0], 1, RWKV_PROJ)
    return pl.pallas_call(
        functools.partial(_rwkv_pre_prompt_kernel, precise=precise),
        grid=(batch, nt),
        in_specs=cur_specs + halo_specs + mu_specs + _rwkv_param_specs(layer),
        out_specs=[out_spec] * 7,
        out_shape=[jax.ShapeDtypeStruct((batch * seq, RWKV_WIDTH), F32)] * 7,
        compiler_params=_cp("parallel", "parallel"), name="rwkv_pre_prompt",
    )(*([p] * 10), *([mu3] * 5), *_rwkv_params(prm))


def _rwkv_pre_sample_kernel(*refs, precise):
    cur = [r[...] for r in refs[0:5]]
    sh = [r[...] for r in refs[5:10]]
    mu = [r[...] for r in refs[10:15]]
    w0, w2, a0, a2, g2, k_k, k_a = [r[...] for r in refs[15:22]]
    _rwkv_pre_core(cur, sh, mu, w0, w2, a0, a2, g2, k_k, k_a, refs[22:], precise)


def _rwkv_pre_sample(pc, pc_shifted, prm, layer, precise):
    rows = pc.shape[0]
    cur_specs, mu_specs = [], []
    for w, off in zip(_PRE_WIDTHS, _PRE_OFFS):
        cur_specs.append(pl.BlockSpec((rows, w), lambda i, cb=off // w: (0, cb)))
        mu_specs.append(pl.BlockSpec((None, 1, w), lambda i, mb=off // w: (layer, 0, mb)))
    out_spec = pl.BlockSpec((rows, RWKV_WIDTH), lambda i: (0, 0))
    mu3 = prm["mu"].reshape(prm["mu"].shape[0], 1, RWKV_PROJ)
    return pl.pallas_call(
        functools.partial(_rwkv_pre_sample_kernel, precise=precise),
        grid=(1,),
        in_specs=cur_specs + cur_specs + mu_specs + _rwkv_param_specs(layer),
        out_specs=[out_spec] * 7,
        out_shape=[jax.ShapeDtypeStruct((rows, RWKV_WIDTH), F32)] * 7,
        compiler_params=_cp("arbitrary"), name="rwkv_pre_sample",
    )(*([pc] * 5), *([pc_shifted] * 5), *([mu3] * 5), *_rwkv_params(prm))


def _wkv_kernel(r_ref, w_ref, k_ref, v_ref, a_ref, b_ref, s0_ref, y_ref, sf_ref, s_scr, yt_scr, *, tt, unroll):
    n = RWKV_HEAD

    @pl.when(pl.program_id(1) == 0)
    def _():
        s_scr[...] = s0_ref[...]

    yt_scr[...] = jnp.zeros_like(yt_scr)
    eye = lax.broadcasted_iota(jnp.int32, (n, n), 0) == lax.broadcasted_iota(jnp.int32, (n, n), 1)
    t_lane = lax.broadcasted_iota(jnp.int32, (n, LANES), 1)

    def group(gi, carry):
        t0 = pl.multiple_of(gi * unroll, unroll)
        rows = [ref[pl.ds(t0, unroll), :] for ref in (r_ref, w_ref, k_ref, v_ref, a_ref, b_ref)]
        for h in range(RWKV_HEADS):
            sl = slice(h * n, (h + 1) * n)
            s = s_scr[h]
            ycols = yt_scr[sl, :]
            for j in range(unroll):
                r, w, k, v, a, b = [x[j:j + 1, sl] for x in rows]
                sa = jnp.sum(s * a, axis=-1, keepdims=True)
                vcol = jnp.sum(jnp.where(eye, v, 0.0), axis=-1, keepdims=True)
                s = s * w + sa * b + vcol * k
                ycol = jnp.sum(s * r, axis=-1, keepdims=True)
                ycols = jnp.where(t_lane == t0 + j, ycol, ycols)
            s_scr[h] = s
            yt_scr[sl, :] = ycols
        return carry

    lax.fori_loop(0, tt // unroll, group, 0)
    y_ref[...] = jnp.transpose(yt_scr[...])[:tt, :]

    @pl.when(pl.program_id(1) == pl.num_programs(1) - 1)
    def _():
        sf_ref[...] = s_scr[...]


def _wkv(seqs, s0, *, tt):
    batch, t, _ = seqs[0].shape
    unroll = min(8, tt)
    assert tt <= LANES and t % tt == 0 and tt % unroll == 0
    seq_spec = pl.BlockSpec((None, tt, RWKV_WIDTH), lambda b, i: (b, i, 0))
    st_spec = pl.BlockSpec((None, RWKV_HEADS, RWKV_HEAD, RWKV_HEAD), lambda b, i: (b, 0, 0, 0))
    return pl.pallas_call(
        functools.partial(_wkv_kernel, tt=tt, unroll=unroll),
        grid=(batch, t // tt),
        in_specs=[seq_spec] * 6 + [st_spec],
        out_specs=[seq_spec, st_spec],
        out_shape=[jax.ShapeDtypeStruct((batch, t, RWKV_WIDTH), F32), jax.ShapeDtypeStruct(s0.shape, F32)],
        scratch_shapes=[pltpu.VMEM((RWKV_HEADS, RWKV_HEAD, RWKV_HEAD), F32), pltpu.VMEM((RWKV_WIDTH, LANES), F32)],
        compiler_params=_cp("parallel", "arbitrary"), name="wkv",
    )(*seqs, s0)


def _rwkv_post_kernel(y_ref, r_ref, k_ref, v_ref, g_ref, rk_ref, lg_ref, lb_ref, o_ref):
    y = y_ref[...]
    inv = 1.0 / RWKV_HEAD
    mean = _head_sum(y) * inv
    yc = y - mean
    var = _head_sum(yc * yc) * inv
    yn = yc * lax.rsqrt(var + RWKV_LN_EPS) * lg_ref[...] + lb_ref[...]
    v = v_ref[...]
    bonus = _head_sum(r_ref[...] * k_ref[...] * rk_ref[...]) * v
    o_ref[...] = ((yn + bonus) * g_ref[...]).astype(o_ref.dtype)


def _rwkv_post(y, r, k, v, g, prm, layer, precise, *, tm):
    rows = y.shape[0]
    depth = prm["rk"].shape[0]
    row = pl.BlockSpec((tm, RWKV_WIDTH), lambda i: (i, 0))
    vec = pl.BlockSpec((None, 1, RWKV_WIDTH), lambda i: (layer, 0, 0))
    r3 = lambda a: a.reshape(depth, 1, RWKV_WIDTH)
    return pl.pallas_call(
        _rwkv_post_kernel,
        grid=(rows // tm,),
        in_specs=[row] * 5 + [vec] * 3,
        out_specs=row,
        out_shape=jax.ShapeDtypeStruct((rows, RWKV_WIDTH), _act_dtype(precise)),
        compiler_params=_cp("parallel"), name="rwkv_post",
    )(y, r, k, v, g, r3(prm["rk"]), r3(prm["ln_g"]), r3(prm["ln_b"]))


def _moe_kernel(h_ref, comb_ref, wg_ref, wu_ref, wd_ref, o_ref):
    e = pl.program_id(1)
    f = pl.program_id(2)

    @pl.when((e == 0) & (f == 0))
    def _():
        o_ref[...] = jnp.zeros_like(o_ref)

    h = h_ref[...]
    gate = _bdot(h, wg_ref[...])
    act = gate * _sigmoid(gate) * _bdot(h, wu_ref[...])
    comb = comb_ref[...]
    lane = lax.broadcasted_iota(jnp.int32, comb.shape, 1)
    ce = jnp.sum(jnp.where(lane == e, comb, 0.0), axis=-1, keepdims=True)
    o_ref[...] += _bdot(act * ce, wd_ref[...])


def _moe(h, comb, w_gate, w_up, w_down, layer, *, tm, tf=256):
    m, d = h.shape
    return pl.pallas_call(
        _moe_kernel,
        grid=(m // tm, N_EXPERTS, D_EXPERT // tf),
        in_specs=[pl.BlockSpec((tm, d), lambda i, e, f: (i, 0)),
                  pl.BlockSpec((tm, N_EXPERTS), lambda i, e, f: (i, 0)),
                  pl.BlockSpec((None, None, d, tf), lambda i, e, f: (layer, e, 0, f)),
                  pl.BlockSpec((None, None, d, tf), lambda i, e, f: (layer, e, 0, f)),
                  pl.BlockSpec((None, None, tf, d), lambda i, e, f: (layer, e, f, 0))],
        out_specs=pl.BlockSpec((tm, d), lambda i, e, f: (i, 0)),
        out_shape=jax.ShapeDtypeStruct((m, d), F32),
        compiler_params=_cp("parallel", "arbitrary", "arbitrary"), name="moe",
    )(h, comb, w_gate, w_up, w_down)


_SH1, _SC1, _GT1, _SH2, _SC2, _GT2 = range(6)


def kernel(x_prompt, x_sample, cache_swa_k, cache_swa_v, state_pool, state_rwkv_shift, state_rwkv_wkv, state_conv, c_prompt, c_sample, w_ada, b_ada, g_norm1, g_norm2, w_in, sinks, w_pool, ls_pool, rwkv_mu, rwkv_w0, rwkv_w2, rwkv_a0, rwkv_a2, rwkv_g2, rwkv_kk, rwkv_ka, rwkv_rk, rwkv_ln_g, rwkv_ln_b, conv_w, w_branch, w_out, w_router, b_router, w_gate, w_up, w_down, g_final):
    depth = w_in.shape[0]
    batch, seq, d = x_prompt.shape
    dbatch, dseq, _ = x_sample.shape
    wbuf = cache_swa_k.shape[2]
    np_rows, ns_rows = batch * seq, dbatch * dseq

    grp_p = _Group(np_rows, seq, 1)
    grp_s = _Group(ns_rows, ns_rows, ns_rows)
    tm_p, tm_s = min(2048, seq), ns_rows
    te_p = min(512, seq)

    c_all = jnp.concatenate([c_prompt, c_sample], axis=0)
    wr_pad = jnp.pad(w_router, ((0, 0), (0, ROUTER_PAD - N_EXPERTS)))
    prm = dict(mu=rwkv_mu, w0=rwkv_w0, w2=rwkv_w2, a0=rwkv_a0, a2=rwkv_a2, g2=rwkv_g2, kk=rwkv_kk, ka=rwkv_ka,
               rk=rwkv_rk.reshape(depth, RWKV_WIDTH), ln_g=rwkv_ln_g, ln_b=rwkv_ln_b)
    cache_k = cache_swa_k.reshape(depth, dbatch, wbuf, KV_WIDTH)
    cache_v = cache_swa_v.reshape(depth, dbatch, wbuf, KV_WIDTH)

    xp = x_prompt.reshape(np_rows, d)
    xs = x_sample.reshape(ns_rows, d)
    st_p, st_s = [], []
    pend_p = pend_s = None
    for l in range(depth):
        precise = l == 0
        hd = _act_dtype(precise)
        mod = _ada(c_all, w_ada, b_ada, l, precise)
        mod_p = mod[:batch].reshape(batch, 1, 6 * d)
        mod_s = jnp.repeat(mod[batch:], dseq, axis=0).reshape(1, ns_rows, 6 * d)

        def first_norm(grp, x, pend, mod3, tm):
            if pend is None:
                return x, _normx(grp, x, g_norm1[l], mod=(mod3, _SC1, _SH1), out_dtype=hd, tm=tm)[0]
            m, mod_prev = pend
            x, h = _normx(grp, x, g_norm1[l], add=(m, mod_prev, _GT2), mod=(mod3, _SC1, _SH1), emit_x=True,
                          out_dtype=hd, tm=tm)
            return x, h

        xp, hp = first_norm(grp_p, xp, pend_p, mod_p, te_p)
        xs, hs = first_norm(grp_s, xs, pend_s, mod_s, tm_s)

        pp = _mm(hp, w_in, l, tm=min(tm_p, 1024) if precise else tm_p, tn=512, precise=precise)
        ps = _mm(hs, w_in, l, tm=tm_s, tn=512, precise=precise)

        a_p = _swa_prompt(pp, sinks, l, batch, seq, precise)
        b_p = _pool_prompt(pp, w_pool, ls_pool, l, batch, seq, precise)
        d_p, conv_new_p = _conv_prompt(pp, conv_w, l, batch, seq, precise)
        pre_p = _rwkv_pre_prompt(pp, prm, l, batch, seq, precise, tt=te_p)
        r_p, w_p, k_p, v_p, ka_p, kb_p, g_p = pre_p
        as3 = lambda t: t.reshape(batch, seq, RWKV_WIDTH)
        y_p, wkv_new_p = _wkv([as3(t) for t in (r_p, w_p, k_p, v_p, ka_p, kb_p)],
                              jnp.zeros((batch, RWKV_HEADS, RWKV_HEAD, RWKV_HEAD), F32), tt=min(LANES, seq))
        c_p = _rwkv_post(y_p.reshape(np_rows, RWKV_WIDTH), r_p, k_p, v_p, g_p, prm, l, precise, tm=te_p)
        pp3 = pp.reshape(batch, seq, IN_WIDTH)
        kw = min(WINDOW, seq)
        st_p.append((pp3[:, seq - kw:, OFF_K:OFF_K + KV_WIDTH].reshape(batch, kw, N_KV_HEADS, HEAD_DIM),
                     pp3[:, seq - kw:, OFF_V:OFF_V + KV_WIDTH].reshape(batch, kw, N_KV_HEADS, HEAD_DIM),
                     pp3[:, seq - POOL_BUF:, OFF_U:OFF_U + POOL_WIDTH],
                     pp3[:, seq - 1, OFF_PC:OFF_PC + RWKV_PROJ],
                     wkv_new_p, conv_new_p))

        ps3 = ps.reshape(dbatch, dseq, IN_WIDTH)
        a_s, k_new_s, v_new_s = _swa_sample(ps3, cache_k, cache_v, sinks, l, precise)
        tmaj = lambda t: jnp.swapaxes(t, 0, 1)
        b_s_t, pool_new_t = _pool_sample(tmaj(ps3[:, :, OFF_U:OFF_U + POOL_WIDTH]), tmaj(state_pool[l]),
                                         w_pool, ls_pool, l, PAST_LEN, precise)
        d_s_t, conv_new_t = _conv_sample(tmaj(ps3[:, :, OFF_CB:OFF_CB + CONV_WIDTH]),
                                         tmaj(ps3[:, :, OFF_CC:OFF_CC + CONV_WIDTH]),
                                         tmaj(ps3[:, :, OFF_CX:OFF_CX + CONV_WIDTH]), tmaj(state_conv[l]), conv_w, l,
                                         precise)
        pc_s3 = ps3[:, :, OFF_PC:OFF_PC + RWKV_PROJ]
        pc_shift = jnp.concatenate([state_rwkv_shift[l][:, None, :], pc_s3[:, :-1, :]], axis=1)
        pre_s = _rwkv_pre_sample(pc_s3.reshape(ns_rows, RWKV_PROJ), pc_shift.reshape(ns_rows, RWKV_PROJ), prm, l,
                                 precise)
        r_s, w_s, k_s, v_s, ka_s, kb_s, g_s = pre_s
        as3s = lambda t: t.reshape(dbatch, dseq, RWKV_WIDTH)
        y_s, wkv_new_s = _wkv([as3s(t) for t in (r_s, w_s, k_s, v_s, ka_s, kb_s)], state_rwkv_wkv[l], tt=dseq)
        c_s = _rwkv_post(y_s.reshape(ns_rows, RWKV_WIDTH), r_s, k_s, v_s, g_s, prm, l, precise, tm=ns_rows)
        st_s.append((k_new_s.reshape(dbatch, wbuf, N_KV_HEADS, HEAD_DIM),
                     v_new_s.reshape(dbatch, wbuf, N_KV_HEADS, HEAD_DIM),
                     tmaj(pool_new_t), pc_s3[:, dseq - 1, :], wkv_new_s, tmaj(conv_new_t)))
        br_s = (a_s.reshape(ns_rows, ATTN_WIDTH), tmaj(b_s_t).reshape(ns_rows, POOL_WIDTH), c_s,
                tmaj(d_s_t).reshape(ns_rows, CONV_WIDTH))

        def tail(grp, x, branches, p, mod3, tm_mm, tm_el, tm_moe):
            if precise:
                merged = _merge(branches, p, w_branch, l, tm=min(tm_mm, 512), tn=256, precise=True)
                x1 = _mm_res(grp, merged, w_out, l, x, mod3, _GT1, tm=min(tm_mm, 512), tn=512, precise=True)
            else:
                merged = _merge(branches, p, w_branch, l, tm=min(tm_mm, 1024), tn=512, precise=False)
                x1 = _mm_res(grp, merged, w_out, l, x, mod3, _GT1, tm=tm_mm, tn=512, precise=False)
            h2, comb = _normx(grp, x1, g_norm2[l], mod=(mod3, _SC2, _SH2), route=(wr_pad, b_router), tm=tm_el)
            return x1, _moe(h2, comb, w_gate, w_up, w_down, l, tm=tm_moe)

        xp, m_p = tail(grp_p, xp, (a_p, b_p, c_p, d_p), pp, mod_p, tm_p, te_p, min(1024, np_rows))
        xs, m_s = tail(grp_s, xs, br_s, ps, mod_s, tm_s, tm_s, tm_s)
        pend_p, pend_s = (m_p, mod_p), (m_s, mod_s)

    y_p = _normx(grp_p, xp, g_final, add=(pend_p[0], pend_p[1], _GT2), out_dtype=F32, tm=te_p)[0]
    y_s = _normx(grp_s, xs, g_final, add=(pend_s[0], pend_s[1], _GT2), out_dtype=F32, tm=tm_s)[0]

    def stack(states, i):
        return jnp.stack([s[i] for s in states])

    return ((y_p.reshape(batch, seq, d), y_s.reshape(dbatch, dseq, d))
            + tuple(stack(st_p, i) for i in range(6)) + tuple(stack(st_s, i) for i in range(6)))
```

```python
import functools

import jax
import jax.numpy as jnp
from jax import lax
from jax.experimental import pallas as pl
from jax.experimental.pallas import tpu as pltpu

F32 = jnp.float32
BF16 = jnp.bfloat16

D_MODEL = 2048
PAST_LEN = 8192
WINDOW = 128
HEAD_DIM = 64
N_HEADS = 16
N_KV_HEADS = 4
GQA_GROUP = N_HEADS // N_KV_HEADS
ATTN_WIDTH = N_HEADS * HEAD_DIM
KV_WIDTH = N_KV_HEADS * HEAD_DIM
ATTN_SCALE = HEAD_DIM ** -0.5
NEG_INF = -1e30
POOL_WINDOWS = (2, 4, 8, 16)
POOL_WIDTH = 768
POOL_GW = POOL_WIDTH // len(POOL_WINDOWS)
POOL_BUF = max(POOL_WINDOWS) - 1
RWKV_HEAD = 64
RWKV_WIDTH = 768
RWKV_HEADS = RWKV_WIDTH // RWKV_HEAD
DECAY_LORA = 64
AAA_LORA = 64
GATE_LORA = 128
RWKV_PROJ = 3 * RWKV_WIDTH + DECAY_LORA + AAA_LORA + GATE_LORA
RWKV_LN_EPS = 64e-5
CONV_WIDTH = 768
CONV_K = 3
N_BRANCH = 4
BRANCH_SECTIONS = (ATTN_WIDTH, POOL_WIDTH, RWKV_WIDTH, CONV_WIDTH)
MIX_WIDTH = sum(BRANCH_SECTIONS)
N_EXPERTS = 16
N_GROUPS = 4
EXP_PER_GROUP = N_EXPERTS // N_GROUPS
D_EXPERT = 1024
RMS_EPS = 1e-6

OFF_Q = 0
OFF_K = OFF_Q + ATTN_WIDTH
OFF_V = OFF_K + KV_WIDTH
OFF_U = OFF_V + KV_WIDTH
OFF_PC = OFF_U + POOL_WIDTH
OFF_CB = OFF_PC + RWKV_PROJ
OFF_CC = OFF_CB + CONV_WIDTH
OFF_CX = OFF_CC + CONV_WIDTH
OFF_GL = OFF_CX + CONV_WIDTH
IN_WIDTH = OFF_GL + N_BRANCH * D_MODEL

LANES = 128
ROUTER_PAD = LANES
VMEM_LIMIT = 60 * 1024 * 1024


def _cp(*sem):
    return pltpu.CompilerParams(dimension_semantics=sem, vmem_limit_bytes=VMEM_LIMIT)


def _sigmoid(x):
    return 1.0 / (1.0 + jnp.exp(-x))


def _split(x):
    hi = x.astype(BF16)
    return hi, (x.astype(F32) - hi.astype(F32)).astype(BF16)


def _mxu(contract, a, b, precise):
    if not precise:
        return contract(a.astype(BF16), b.astype(BF16))
    ah, al = _split(a)
    bh, bl = _split(b)
    return contract(ah, bh) + (contract(ah, bl) + contract(al, bh))


def _dot2(a, b):
    return jnp.dot(a, b, preferred_element_type=F32)


def _bdot(a, b, precise=False):
    return _mxu(_dot2, a, b, precise)


def _act_dtype(precise):
    return F32 if precise else BF16


def _ada_kernel(c_ref, w_ref, b_ref, o_ref, *, precise):
    c = c_ref[...]
    o_ref[...] = _bdot(c * _sigmoid(c), w_ref[...], precise) + b_ref[...]


def _ada(c_all, w_ada, b_ada, layer, precise):
    depth, d, n = w_ada.shape
    nb = c_all.shape[0]
    tn = 1024
    return pl.pallas_call(
        functools.partial(_ada_kernel, precise=precise),
        grid=(n // tn,),
        in_specs=[pl.BlockSpec((nb, d), lambda j: (0, 0)),
                  pl.BlockSpec((None, d, tn), lambda j: (layer, 0, j)),
                  pl.BlockSpec((None, 1, tn), lambda j: (layer, 0, j))],
        out_specs=pl.BlockSpec((nb, tn), lambda j: (0, j)),
        out_shape=jax.ShapeDtypeStruct((nb, n), F32),
        compiler_params=_cp("parallel"),
        name="ada",
    )(c_all, w_ada, b_ada.reshape(depth, 1, n))


class _Group:
    def __init__(self, rows, rpm, mod_rows):
        self.rows, self.rpm, self.mod_rows = rows, rpm, mod_rows

    def mod_spec(self, tm, width, col_of, row_axis=0):
        if self.mod_rows == 1:
            per = self.rpm // tm
            return pl.BlockSpec((None, 1, width), lambda *idx: (idx[row_axis] // per, 0, col_of(idx)))
        assert tm == self.rpm == self.mod_rows
        return pl.BlockSpec((None, tm, width), lambda *idx: (idx[row_axis], 0, col_of(idx)))


def _route(y, wr, br):
    y_hi = y.astype(BF16)
    y_lo = (y - y_hi.astype(F32)).astype(BF16)
    w_hi = wr.astype(BF16)
    w_lo = (wr - w_hi.astype(F32)).astype(BF16)
    logits = (jnp.dot(y_hi, w_hi, preferred_element_type=F32)
              + (jnp.dot(y_hi, w_lo, preferred_element_type=F32) + jnp.dot(y_lo, w_hi, preferred_element_type=F32)))
    logits = logits[:, :N_EXPERTS] + br
    tm = logits.shape[0]
    e = jnp.exp(logits - jnp.max(logits, axis=-1, keepdims=True))
    probs = e / jnp.sum(e, axis=-1, keepdims=True)
    iota_g = lax.broadcasted_iota(jnp.int32, (tm, EXP_PER_GROUP), 1)
    best = None
    for g in range(N_GROUPS):
        pg = probs[:, g * EXP_PER_GROUP:(g + 1) * EXP_PER_GROUP]
        m1 = jnp.max(pg, axis=-1, keepdims=True)
        i1 = jnp.min(jnp.where(pg == m1, iota_g, EXP_PER_GROUP), axis=-1, keepdims=True)
        rest = jnp.where(iota_g == i1, -1.0, pg)
        m2 = jnp.max(rest, axis=-1, keepdims=True)
        i2 = jnp.min(jnp.where(rest == m2, iota_g, EXP_PER_GROUP), axis=-1, keepdims=True)
        cand = (m1 + m2, m1, m2, i1 + g * EXP_PER_GROUP, i2 + g * EXP_PER_GROUP)
        if best is None:
            best = cand
        else:
            take = cand[0] > best[0]
            best = tuple(jnp.where(take, c, b) for c, b in zip(cand, best))
    _, m1, m2, e1, e2 = best
    den = m1 + m2
    iota_e = lax.broadcasted_iota(jnp.int32, (tm, N_EXPERTS), 1)
    return jnp.where(iota_e == e1, m1 / den, 0.0) + jnp.where(iota_e == e2, m2 / den, 0.0)


def _normx_kernel(*refs, has_add, has_mod, has_route, emit_x):
    it = iter(refs)
    x_ref = next(it)
    if has_add:
        m_ref, gate_ref = next(it), next(it)
    g_ref = next(it)
    if has_mod:
        sc_ref, sh_ref = next(it), next(it)
    if has_route:
        wr_ref, br_ref = next(it), next(it)
    if emit_x:
        xo_ref = next(it)
    h_ref = next(it)
    if has_route:
        comb_ref = next(it)
    x = x_ref[...]
    if has_add:
        x = x + gate_ref[...] * m_ref[...]
    if emit_x:
        xo_ref[...] = x
    y = x * lax.rsqrt(jnp.mean(x * x, axis=-1, keepdims=True) + RMS_EPS) * g_ref[...]
    if has_mod:
        y = y * (1.0 + sc_ref[...]) + sh_ref[...]
    h_ref[...] = y.astype(h_ref.dtype)
    if has_route:
        comb_ref[...] = _route(y, wr_ref[...], br_ref[...])


def _normx(grp, x, g, *, add=None, mod=None, route=None, emit_x=False, out_dtype=BF16, tm=512):
    n, d = x.shape
    row = pl.BlockSpec((tm, d), lambda i: (i, 0))
    vec = pl.BlockSpec((1, d), lambda i: (0, 0))
    args, specs = [x], [row]
    if add is not None:
        m, mod3, chunk = add
        args += [m, mod3]
        specs += [row, grp.mod_spec(tm, d, lambda idx, c=chunk: c)]
    args.append(g.reshape(1, d))
    specs.append(vec)
    if mod is not None:
        mod3, c_sc, c_sh = mod
        args += [mod3, mod3]
        specs += [grp.mod_spec(tm, d, lambda idx, c=c_sc: c), grp.mod_spec(tm, d, lambda idx, c=c_sh: c)]
    if route is not None:
        wr, br = route
        args += [wr, br.reshape(1, N_EXPERTS)]
        specs += [pl.BlockSpec((d, ROUTER_PAD), lambda i: (0, 0)), pl.BlockSpec((1, N_EXPERTS), lambda i: (0, 0))]
    out_shape, out_specs = [], []
    if emit_x:
        out_shape.append(jax.ShapeDtypeStruct((n, d), F32))
        out_specs.append(row)
    out_shape.append(jax.ShapeDtypeStruct((n, d), out_dtype))
    out_specs.append(row)
    if route is not None:
        out_shape.append(jax.ShapeDtypeStruct((n, N_EXPERTS), F32))
        out_specs.append(pl.BlockSpec((tm, N_EXPERTS), lambda i: (i, 0)))
    return pl.pallas_call(
        functools.partial(_normx_kernel, has_add=add is not None, has_mod=mod is not None,
                          has_route=route is not None, emit_x=emit_x),
        grid=(n // tm,), in_specs=specs, out_specs=out_specs, out_shape=out_shape,
        compiler_params=_cp("parallel"), name="normx",
    )(*args)


def _mm_kernel(a_ref, w_ref, o_ref, *, precise):
    o_ref[...] = _bdot(a_ref[...], w_ref[...], precise).astype(o_ref.dtype)


def _mm(a, w3, layer, *, tm, tn, precise, out_dtype=F32):
    m, k = a.shape
    n = w3.shape[-1]
    return pl.pallas_call(
        functools.partial(_mm_kernel, precise=precise),
        grid=(m // tm, n // tn),
        in_specs=[pl.BlockSpec((tm, k), lambda i, j: (i, 0)),
                  pl.BlockSpec((None, k, tn), lambda i, j: (layer, 0, j))],
        out_specs=pl.BlockSpec((tm, tn), lambda i, j: (i, j)),
        out_shape=jax.ShapeDtypeStruct((m, n), out_dtype),
        compiler_params=_cp("parallel", "parallel"), name="mm_in",
    )(a, w3)


def _mm_res_kernel(a_ref, w_ref, x_ref, gate_ref, o_ref, *, precise):
    o_ref[...] = x_ref[...] + gate_ref[...] * _bdot(a_ref[...], w_ref[...], precise)


def _mm_res(grp, a, w3, layer, x, mod3, gate_chunk, *, tm, tn, precise):
    m, k = a.shape
    n = w3.shape[-1]
    per_chunk = n // tn
    return pl.pallas_call(
        functools.partial(_mm_res_kernel, precise=precise),
        grid=(m // tm, n // tn),
        in_specs=[pl.BlockSpec((tm, k), lambda i, j: (i, 0)),
                  pl.BlockSpec((None, k, tn), lambda i, j: (layer, 0, j)),
                  pl.BlockSpec((tm, tn), lambda i, j: (i, j)),
                  grp.mod_spec(tm, tn, lambda idx: gate_chunk * per_chunk + idx[1])],
        out_specs=pl.BlockSpec((tm, tn), lambda i, j: (i, j)),
        out_shape=jax.ShapeDtypeStruct((m, n), F32),
        compiler_params=_cp("parallel", "parallel"), name="mm_out",
    )(a, w3, x, mod3)


def _merge_kernel(a_ref, b_ref, c_ref, d_ref, g0_ref, g1_ref, g2_ref, g3_ref, w_ref, o_ref, *, precise):
    acc = None
    lo = 0
    for br_ref, g_ref, width in zip((a_ref, b_ref, c_ref, d_ref), (g0_ref, g1_ref, g2_ref, g3_ref), BRANCH_SECTIONS):
        t = _sigmoid(g_ref[...]) * _bdot(br_ref[...], w_ref[lo:lo + width, :], precise)
        acc = t if acc is None else acc + t
        lo += width
    o_ref[...] = acc.astype(o_ref.dtype)


def _merge(branches, p, w_branch, layer, *, tm, tn, precise):
    m = p.shape[0]
    gl_blk = OFF_GL // tn
    per = D_MODEL // tn
    br_specs = [pl.BlockSpec((tm, w), lambda i, j: (i, 0)) for w in BRANCH_SECTIONS]
    gl_specs = [pl.BlockSpec((tm, tn), lambda i, j, b=b: (i, gl_blk + b * per + j)) for b in range(N_BRANCH)]
    return pl.pallas_call(
        functools.partial(_merge_kernel, precise=precise),
        grid=(m // tm, D_MODEL // tn),
        in_specs=br_specs + gl_specs + [pl.BlockSpec((None, MIX_WIDTH, tn), lambda i, j: (layer, 0, j))],
        out_specs=pl.BlockSpec((tm, tn), lambda i, j: (i, j)),
        out_shape=jax.ShapeDtypeStruct((m, D_MODEL), _act_dtype(precise)),
        compiler_params=_cp("parallel", "parallel"), name="merge",
    )(*branches, p, p, p, p, w_branch)


def _sink_col(sink_ref, layer, kh, rows_per_head):
    return jnp.concatenate([jnp.full((rows_per_head, 1), sink_ref[layer, kh * GQA_GROUP + g], F32)
                            for g in range(GQA_GROUP)], axis=0)


def _dot_nt(a, b):
    return lax.dot_general(a, b, (((1,), (1,)), ((), ())), preferred_element_type=F32)


def _swa_prompt_kernel(sink_ref, q_ref, kc_ref, kp_ref, vc_ref, vp_ref, o_ref, *, layer, precise):
    n = pl.program_id(1)
    w = WINDOW
    q = q_ref[...]
    qi = jnp.bitwise_and(lax.broadcasted_iota(jnp.int32, (GQA_GROUP * w, 2 * w), 0), w - 1)
    sj = lax.broadcasted_iota(jnp.int32, (GQA_GROUP * w, 2 * w), 1)
    valid = (sj >= qi) & (sj <= qi + w) & ((sj >= w) | (n > 0))
    outs = []
    for kh in range(N_KV_HEADS):
        sl = slice(kh * HEAD_DIM, (kh + 1) * HEAD_DIM)
        k2 = jnp.concatenate([kp_ref[:, sl], kc_ref[:, sl]], axis=0)
        v2 = jnp.concatenate([vp_ref[:, sl], vc_ref[:, sl]], axis=0)
        q4 = jnp.concatenate([q[:, (kh * GQA_GROUP + g) * HEAD_DIM:(kh * GQA_GROUP + g + 1) * HEAD_DIM]
                              for g in range(GQA_GROUP)], axis=0)
        s = _mxu(_dot_nt, q4, k2, precise) * ATTN_SCALE
        s = jnp.where(valid, s, NEG_INF)
        sk = _sink_col(sink_ref, layer, kh, w)
        m = jnp.maximum(jnp.max(s, axis=-1, keepdims=True), sk)
        e = jnp.exp(s - m)
        p = e / (jnp.sum(e, axis=-1, keepdims=True) + jnp.exp(sk - m))
        o4 = _bdot(p, v2, precise)
        outs += [o4[g * w:(g + 1) * w] for g in range(GQA_GROUP)]
    o_ref[...] = jnp.concatenate(outs, axis=1).astype(o_ref.dtype)


def _swa_prompt(p, sinks, layer, batch, seq, precise):
    nb = seq // WINDOW
    kblk, vblk = OFF_K // KV_WIDTH, OFF_V // KV_WIDTH

    def cur(col):
        return lambda b, n: (b * nb + n, col)

    def prev(col):
        return lambda b, n: (b * nb + jnp.maximum(n - 1, 0), col)

    return pl.pallas_call(
        functools.partial(_swa_prompt_kernel, layer=layer, precise=precise),
        grid=(batch, nb),
        in_specs=[pl.BlockSpec(memory_space=pltpu.SMEM),
                  pl.BlockSpec((WINDOW, ATTN_WIDTH), cur(0)),
                  pl.BlockSpec((WINDOW, KV_WIDTH), cur(kblk)), pl.BlockSpec((WINDOW, KV_WIDTH), prev(kblk)),
                  pl.BlockSpec((WINDOW, KV_WIDTH), cur(vblk)), pl.BlockSpec((WINDOW, KV_WIDTH), prev(vblk))],
        out_specs=pl.BlockSpec((WINDOW, ATTN_WIDTH), lambda b, n: (b * nb + n, 0)),
        out_shape=jax.ShapeDtypeStruct((batch * seq, ATTN_WIDTH), _act_dtype(precise)),
        compiler_params=_cp("parallel", "parallel"), name="swa_prompt",
    )(sinks, p, p, p, p, p)


def _qk(a, b):
    return jnp.einsum("bqd,bkd->bqk", a, b, preferred_element_type=F32)


def _pv(a, b):
    return jnp.einsum("bqk,bkd->bqd", a, b, preferred_element_type=F32)


def _swa_sample_kernel(sink_ref, q_ref, kn_ref, vn_ref, kc_ref, vc_ref, o_ref, ko_ref, vo_ref, *, layer, steps, wbuf,
                       precise):
    q = q_ref[...]
    kn, vn = kn_ref[...], vn_ref[...]
    kc, vc = kc_ref[...], vc_ref[...]
    ko_ref[:, :wbuf - steps, :] = kc[:, steps:, :]
    ko_ref[:, wbuf - steps:, :] = kn
    vo_ref[:, :wbuf - steps, :] = vc[:, steps:, :]
    vo_ref[:, wbuf - steps:, :] = vn
    bb = q.shape[0]
    rows = GQA_GROUP * steps
    t_c = lax.rem(lax.broadcasted_iota(jnp.int32, (bb, rows, wbuf), 1), steps)
    j_c = lax.broadcasted_iota(jnp.int32, (bb, rows, wbuf), 2)
    dist_c = t_c + wbuf - j_c
    valid_c = (dist_c >= 0) & (dist_c <= WINDOW)
    t_n = lax.rem(lax.broadcasted_iota(jnp.int32, (bb, rows, steps), 1), steps)
    j_n = lax.broadcasted_iota(jnp.int32, (bb, rows, steps), 2)
    valid_n = (t_n - j_n >= 0) & (t_n - j_n <= WINDOW)
    outs = [None] * N_HEADS
    for kh in range(N_KV_HEADS):
        sl = slice(kh * HEAD_DIM, (kh + 1) * HEAD_DIM)
        qg = jnp.concatenate([q[:, :, (kh * GQA_GROUP + g) * HEAD_DIM:(kh * GQA_GROUP + g + 1) * HEAD_DIM]
                              for g in range(GQA_GROUP)], axis=1)
        s_c = _mxu(_qk, qg, kc[:, :, sl], precise) * ATTN_SCALE
        s_n = _mxu(_qk, qg, kn[:, :, sl], precise) * ATTN_SCALE
        s_c = jnp.where(valid_c, s_c, NEG_INF)
        s_n = jnp.where(valid_n, s_n, NEG_INF)
        sk = _sink_col(sink_ref, layer, kh, steps)[None]
        m = jnp.maximum(jnp.maximum(jnp.max(s_c, axis=-1, keepdims=True), jnp.max(s_n, axis=-1, keepdims=True)), sk)
        e_c, e_n = jnp.exp(s_c - m), jnp.exp(s_n - m)
        den = jnp.sum(e_c, axis=-1, keepdims=True) + jnp.sum(e_n, axis=-1, keepdims=True) + jnp.exp(sk - m)
        o = _mxu(_pv, e_c / den, vc[:, :, sl], precise) + _mxu(_pv, e_n / den, vn[:, :, sl], precise)
        for g in range(GQA_GROUP):
            outs[kh * GQA_GROUP + g] = o[:, g * steps:(g + 1) * steps, :]
    o_ref[...] = jnp.concatenate(outs, axis=2).astype(o_ref.dtype)


def _swa_sample(p3, cache_k, cache_v, sinks, layer, precise, *, bb=8):
    batch, steps, _ = p3.shape
    wbuf = cache_k.shape[2]
    kblk, vblk = OFF_K // KV_WIDTH, OFF_V // KV_WIDTH
    cache_spec = pl.BlockSpec((None, bb, wbuf, KV_WIDTH), lambda i: (layer, i, 0, 0))
    new_spec = pl.BlockSpec((bb, wbuf, KV_WIDTH), lambda i: (i, 0, 0))
    return pl.pallas_call(
        functools.partial(_swa_sample_kernel, layer=layer, steps=steps, wbuf=wbuf, precise=precise),
        grid=(batch // bb,),
        in_specs=[pl.BlockSpec(memory_space=pltpu.SMEM),
                  pl.BlockSpec((bb, steps, ATTN_WIDTH), lambda i: (i, 0, 0)),
                  pl.BlockSpec((bb, steps, KV_WIDTH), lambda i: (i, 0, kblk)),
                  pl.BlockSpec((bb, steps, KV_WIDTH), lambda i: (i, 0, vblk)),
                  cache_spec, cache_spec],
        out_specs=[pl.BlockSpec((bb, steps, ATTN_WIDTH), lambda i: (i, 0, 0)), new_spec, new_spec],
        out_shape=[jax.ShapeDtypeStruct((batch, steps, ATTN_WIDTH), _act_dtype(precise)),
                   jax.ShapeDtypeStruct((batch, wbuf, KV_WIDTH), F32),
                   jax.ShapeDtypeStruct((batch, wbuf, KV_WIDTH), F32)],
        compiler_params=_cp("parallel"), name="swa_sample",
    )(sinks, p3, p3, p3, cache_k, cache_v)


def _shift_rows(x, k):
    rows = lax.broadcasted_iota(jnp.int32, x.shape, 0)
    return jnp.where(rows >= k, pltpu.roll(x, k, axis=0), 0.0)


def _pool_prompt_kernel(u_ref, w_ref, ls_ref, o_ref, *, precise):
    u = u_ref[...]
    t = u.shape[0]
    pos1 = (lax.broadcasted_iota(jnp.int32, (t, 1), 0) + 1).astype(F32)
    sums = {1: u}
    win = 1
    while win < max(POOL_WINDOWS):
        sums[2 * win] = sums[win] + _shift_rows(sums[win], win)
        win *= 2
    outs = []
    for gi, win in enumerate(POOL_WINDOWS):
        sl = slice(gi * POOL_GW, (gi + 1) * POOL_GW)
        cnt = jnp.minimum(float(win), pos1)
        d = sums[win][:, sl] / cnt - u[:, sl]
        outs.append(_bdot(d, w_ref[gi], precise))
    o_ref[...] = (jnp.concatenate(outs, axis=1) * ls_ref[...]).astype(o_ref.dtype)


def _pool_prompt(p, w_pool, ls_pool, layer, batch, seq, precise):
    gw = POOL_GW
    return pl.pallas_call(
        functools.partial(_pool_prompt_kernel, precise=precise),
        grid=(batch,),
        in_specs=[pl.BlockSpec((seq, POOL_WIDTH), lambda b: (b, OFF_U // POOL_WIDTH)),
                  pl.BlockSpec((None, len(POOL_WINDOWS), gw, gw), lambda b: (layer, 0, 0, 0)),
                  pl.BlockSpec((None, 1, POOL_WIDTH), lambda b: (layer, 0, 0))],
        out_specs=pl.BlockSpec((seq, POOL_WIDTH), lambda b: (b, 0)),
        out_shape=jax.ShapeDtypeStruct((batch * seq, POOL_WIDTH), _act_dtype(precise)),
        compiler_params=_cp("parallel"), name="pool_prompt",
    )(p, w_pool, ls_pool.reshape(ls_pool.shape[0], 1, POOL_WIDTH))


def _pool_sample_kernel(u_ref, past_ref, w_ref, ls_ref, o_ref, new_ref, *, pos0, precise):
    steps, hist = u_ref.shape[0], past_ref.shape[0]
    full = [past_ref[i] for i in range(hist)] + [u_ref[i] for i in range(steps)]
    for i in range(hist):
        new_ref[i] = full[steps + i]
    ds = [[] for _ in POOL_WINDOWS]
    for t in range(steps):
        for gi, win in enumerate(POOL_WINDOWS):
            sl = slice(gi * POOL_GW, (gi + 1) * POOL_GW)
            wsum = full[hist + t][:, sl]
            for s in range(1, win):
                wsum = wsum + full[hist + t - s][:, sl]
            cnt = float(min(win, pos0 + t + 1))
            ds[gi].append(wsum / cnt - full[hist + t][:, sl])
    ys = [_bdot(jnp.concatenate(ds[gi], axis=0), w_ref[gi], precise) for gi in range(len(POOL_WINDOWS))]
    y = jnp.concatenate(ys, axis=1) * ls_ref[...]
    nb = u_ref.shape[1]
    for t in range(steps):
        o_ref[t] = y[t * nb:(t + 1) * nb].astype(o_ref.dtype)


def _pool_sample(u_t, past_t, w_pool, ls_pool, layer, pos0, precise):
    steps, nb, _ = u_t.shape
    return pl.pallas_call(
        functools.partial(_pool_sample_kernel, pos0=pos0, precise=precise),
        grid=(1,),
        in_specs=[pl.BlockSpec(u_t.shape, lambda i: (0, 0, 0)),
                  pl.BlockSpec(past_t.shape, lambda i: (0, 0, 0)),
                  pl.BlockSpec((None, len(POOL_WINDOWS), POOL_GW, POOL_GW), lambda i: (layer, 0, 0, 0)),
                  pl.BlockSpec((None, 1, POOL_WIDTH), lambda i: (layer, 0, 0))],
        out_specs=[pl.BlockSpec(u_t.shape, lambda i: (0, 0, 0)), pl.BlockSpec(past_t.shape, lambda i: (0, 0, 0))],
        out_shape=[jax.ShapeDtypeStruct(u_t.shape, _act_dtype(precise)), jax.ShapeDtypeStruct(past_t.shape, F32)],
        compiler_params=_cp("arbitrary"), name="pool_sample",
    )(u_t, past_t, w_pool, ls_pool.reshape(ls_pool.shape[0], 1, POOL_WIDTH))


def _conv_prompt_kernel(cb_ref, cc_ref, cx_ref, w_ref, o_ref, new_ref):
    z = cc_ref[...] * cx_ref[...]
    w = w_ref[...]
    y = w[CONV_K - 1:CONV_K] * z
    for j in range(1, CONV_K):
        y = y + w[CONV_K - 1 - j:CONV_K - j] * _shift_rows(z, j)
    o_ref[...] = (cb_ref[...] * y).astype(o_ref.dtype)
    new_ref[...] = z[z.shape[0] - (CONV_K - 1):]


def _conv_prompt(p, conv_w, layer, batch, seq, precise, *, tc=256):
    nc = CONV_WIDTH // tc

    def col(off):
        return lambda b, c: (b, off // tc + c)

    return pl.pallas_call(
        _conv_prompt_kernel,
        grid=(batch, nc),
        in_specs=[pl.BlockSpec((seq, tc), col(OFF_CB)), pl.BlockSpec((seq, tc), col(OFF_CC)),
                  pl.BlockSpec((seq, tc), col(OFF_CX)),
                  pl.BlockSpec((None, CONV_K, tc), lambda b, c: (layer, 0, c))],
        out_specs=[pl.BlockSpec((seq, tc), lambda b, c: (b, c)),
                   pl.BlockSpec((None, CONV_K - 1, tc), lambda b, c: (b, 0, c))],
        out_shape=[jax.ShapeDtypeStruct((batch * seq, CONV_WIDTH), _act_dtype(precise)),
                   jax.ShapeDtypeStruct((batch, CONV_K - 1, CONV_WIDTH), F32)],
        compiler_params=_cp("parallel", "parallel"), name="conv_prompt",
    )(p, p, p, conv_w)


def _conv_sample_kernel(cb_ref, cc_ref, cx_ref, past_ref, w_ref, o_ref, new_ref):
    steps, hist = cb_ref.shape[0], past_ref.shape[0]
    w = w_ref[...]
    full = [past_ref[i] for i in range(hist)] + [cc_ref[t] * cx_ref[t] for t in range(steps)]
    for t in range(steps):
        y = w[0:1] * full[t]
        for j in range(1, CONV_K):
            y = y + w[j:j + 1] * full[t + j]
        o_ref[t] = (cb_ref[t] * y).astype(o_ref.dtype)
    for i in range(hist):
        new_ref[i] = full[steps + i]


def _conv_sample(cb_t, cc_t, cx_t, past_t, conv_w, layer, precise):
    full3 = lambda shape: pl.BlockSpec(shape, lambda i: (0, 0, 0))
    return pl.pallas_call(
        _conv_sample_kernel,
        grid=(1,),
        in_specs=[full3(cb_t.shape), full3(cc_t.shape), full3(cx_t.shape), full3(past_t.shape),
                  pl.BlockSpec((None, CONV_K, CONV_WIDTH), lambda i: (layer, 0, 0))],
        out_specs=[full3(cb_t.shape), full3(past_t.shape)],
        out_shape=[jax.ShapeDtypeStruct(cb_t.shape, _act_dtype(precise)), jax.ShapeDtypeStruct(past_t.shape, F32)],
        compiler_params=_cp("arbitrary"), name="conv_sample",
    )(cb_t, cc_t, cx_t, past_t, conv_w)


def _head_sum(x):
    rows = x.shape[0]
    return jnp.concatenate(
        [jnp.broadcast_to(jnp.sum(x[:, h * RWKV_HEAD:(h + 1) * RWKV_HEAD], axis=-1, keepdims=True), (rows, RWKV_HEAD))
         for h in range(RWKV_HEADS)], axis=1)


def _softplus(x):
    return jnp.maximum(x, 0.0) + jnp.log(1.0 + jnp.exp(-jnp.abs(x)))


def _rwkv_pre_core(cur, sh, mu, w0, w2, a0, a2, g2, k_k, k_a, outs, precise):
    xr, xk, xv, xwa, xg = [c + (s - c) * m for c, s, m in zip(cur, sh, mu)]
    wd, ad = xwa[:, :DECAY_LORA], xwa[:, DECAY_LORA:]
    w_log = -_softplus(-(w0 + _bdot(jnp.tanh(wd), w2, precise))) - 0.5
    log_decay = -jnp.exp(w_log)
    a = _sigmoid(a0 + _bdot(ad, a2, precise))
    g = _bdot(_sigmoid(xg), g2, precise)
    kk = xk * k_k
    kk = kk / jnp.maximum(jnp.sqrt(_head_sum(kk * kk)), 1e-12)
    kf = xk * (1.0 + (a - 1.0) * k_a)
    r_ref, w_ref, k_ref, v_ref, a_ref, b_ref, g_ref = outs
    r_ref[...] = xr
    w_ref[...] = log_decay
    k_ref[...] = kf
    v_ref[...] = xv
    a_ref[...] = -kk
    b_ref[...] = kk * a
    g_ref[...] = g


_PRE_WIDTHS = (RWKV_WIDTH, RWKV_WIDTH, RWKV_WIDTH, DECAY_LORA + AAA_LORA, GATE_LORA)
_PRE_OFFS = (0, RWKV_WIDTH, 2 * RWKV_WIDTH, 3 * RWKV_WIDTH, 3 * RWKV_WIDTH + DECAY_LORA + AAA_LORA)
_HALO = 8


def _rwkv_pre_prompt_kernel(*refs, precise):
    cur_refs, halo_refs, mu_refs = refs[0:5], refs[5:10], refs[10:15]
    w0, w2, a0, a2, g2, k_k, k_a = [r[...] for r in refs[15:22]]
    outs = refs[22:]
    first = pl.program_id(1) == 0
    cur, sh = [], []
    for c_ref, h_ref in zip(cur_refs, halo_refs):
        c = c_ref[...]
        prev_row = jnp.where(first, 0.0, h_ref[_HALO - 1:_HALO, :])
        rows = lax.broadcasted_iota(jnp.int32, c.shape, 0)
        sh.append(jnp.where(rows == 0, prev_row, pltpu.roll(c, 1, axis=0)))
        cur.append(c)
    _rwkv_pre_core(cur, sh, [m[...] for m in mu_refs], w0, w2, a0, a2, g2, k_k, k_a, outs, precise)


def _rwkv_param_specs(layer):
    def spec(shape):
        return pl.BlockSpec((None,) + shape, lambda *idx: (layer,) + (0,) * len(shape))

    return [spec((1, RWKV_WIDTH)), spec((DECAY_LORA, RWKV_WIDTH)), spec((1, RWKV_WIDTH)),
            spec((AAA_LORA, RWKV_WIDTH)), spec((GATE_LORA, RWKV_WIDTH)), spec((1, RWKV_WIDTH)), spec((1, RWKV_WIDTH))]


def _rwkv_params(prm):
    depth = prm["w0"].shape[0]
    r3 = lambda a: a.reshape(depth, 1, RWKV_WIDTH)
    return [r3(prm["w0"]), prm["w2"], r3(prm["a0"]), prm["a2"], prm["g2"], r3(prm["kk"]), r3(prm["ka"])]


def _rwkv_pre_prompt(p, prm, layer, batch, seq, precise, *, tt=512):
    nt = seq // tt
    cur_specs, halo_specs, mu_specs = [], [], []
    for w, off in zip(_PRE_WIDTHS, _PRE_OFFS):
        cb = (OFF_PC + off) // w
        cur_specs.append(pl.BlockSpec((tt, w), lambda b, t, cb=cb: (b * nt + t, cb)))
        halo_specs.append(pl.BlockSpec(
            (_HALO, w), lambda b, t, cb=cb: (jnp.maximum((b * nt + t) * (tt // _HALO) - 1, 0), cb)))
        mu_specs.append(pl.BlockSpec((None, 1, w), lambda b, t, mb=off // w: (layer, 0, mb)))
    out_spec = pl.BlockSpec((tt, RWKV_WIDTH), lambda b, t: (b * nt + t, 0))
    mu3 = prm["mu"].reshape(prm["mu"].shape[0], 1, RWKV_PROJ)
    return pl.pallas_call(
        functools.partial(_rwkv_pre_prompt_kernel, precise=precise),
        grid=(batch, nt),
        in_specs=cur_specs + halo_specs + mu_specs + _rwkv_param_specs(layer),
        out_specs=[out_spec] * 7,
        out_shape=[jax.ShapeDtypeStruct((batch * seq, RWKV_WIDTH), F32)] * 7,
        compiler_params=_cp("parallel", "parallel"), name="rwkv_pre_prompt",
    )(*([p] * 10), *([mu3] * 5), *_rwkv_params(prm))


def _rwkv_pre_sample_kernel(*refs, precise):
    cur = [r[...] for r in refs[0:5]]
    sh = [r[...] for r in refs[5:10]]
    mu = [r[...] for r in refs[10:15]]
    w0, w2, a0, a2, g2, k_k, k_a = [r[...] for r in refs[15:22]]
    _rwkv_pre_core(cur, sh, mu, w0, w2, a0, a2, g2, k_k, k_a, refs[22:], precise)


def _rwkv_pre_sample(pc, pc_shifted, prm, layer, precise):
    rows = pc.shape[0]
    cur_specs, mu_specs = [], []
    for w, off in zip(_PRE_WIDTHS, _PRE_OFFS):
        cur_specs.append(pl.BlockSpec((rows, w), lambda i, cb=off // w: (0, cb)))
        mu_specs.append(pl.BlockSpec((None, 1, w), lambda i, mb=off // w: (layer, 0, mb)))
    out_spec = pl.BlockSpec((rows, RWKV_WIDTH), lambda i: (0, 0))
    mu3 = prm["mu"].reshape(prm["mu"].shape[0], 1, RWKV_PROJ)
    return pl.pallas_call(
        functools.partial(_rwkv_pre_sample_kernel, precise=precise),
        grid=(1,),
        in_specs=cur_specs + cur_specs + mu_specs + _rwkv_param_specs(layer),
        out_specs=[out_spec] * 7,
        out_shape=[jax.ShapeDtypeStruct((rows, RWKV_WIDTH), F32)] * 7,
        compiler_params=_cp("arbitrary"), name="rwkv_pre_sample",
    )(*([pc] * 5), *([pc_shifted] * 5), *([mu3] * 5), *_rwkv_params(prm))


def _split_heads(x):
    return jnp.stack([x[:, h * RWKV_HEAD:(h + 1) * RWKV_HEAD] for h in range(RWKV_HEADS)], axis=0)


def _join_heads(x):
    return jnp.concatenate([x[h] for h in range(RWKV_HEADS)], axis=1)


def _wkv_steps_kernel(r_ref, ld_ref, k_ref, v_ref, a_ref, b_ref, s0_ref, y_ref, sf_ref, *, steps, bb):
    n = RWKV_HEAD
    eye = lax.broadcasted_iota(jnp.int32, (n, n), 0) == lax.broadcasted_iota(jnp.int32, (n, n), 1)
    for i in range(bb):
        s = s0_ref[i]
        seqs = [_split_heads(ref[i]) for ref in (r_ref, ld_ref, k_ref, v_ref, a_ref, b_ref)]
        out_rows = []
        for t in range(steps):
            r, ld, k, v, a, b = [x[:, t:t + 1, :] for x in seqs]
            sa = jnp.sum(s * a, axis=-1, keepdims=True)
            vcol = jnp.sum(jnp.where(eye, v, 0.0), axis=-1, keepdims=True)
            s = s * jnp.exp(ld) + sa * b + vcol * k
            ycol = jnp.sum(s * r, axis=-1, keepdims=True)
            out_rows.append(jnp.sum(jnp.where(eye, ycol, 0.0), axis=1, keepdims=True))
        y_ref[i] = _join_heads(jnp.concatenate(out_rows, axis=1))
        sf_ref[i] = s


def _wkv_steps(seqs, s0, *, bb=4):
    batch, t, _ = seqs[0].shape
    seq_spec = pl.BlockSpec((bb, t, RWKV_WIDTH), lambda i: (i, 0, 0))
    st_spec = pl.BlockSpec((bb, RWKV_HEADS, RWKV_HEAD, RWKV_HEAD), lambda i: (i, 0, 0, 0))
    return pl.pallas_call(
        functools.partial(_wkv_steps_kernel, steps=t, bb=bb),
        grid=(batch // bb,),
        in_specs=[seq_spec] * 6 + [st_spec],
        out_specs=[seq_spec, st_spec],
        out_shape=[jax.ShapeDtypeStruct((batch, t, RWKV_WIDTH), F32), jax.ShapeDtypeStruct(s0.shape, F32)],
        compiler_params=_cp("parallel"), name="wkv_steps",
    )(*seqs, s0)


WKV_CHUNK = 64


def _e_nt(a, b):
    return jnp.einsum("hqd,hkd->hqk", a, b, preferred_element_type=F32)


def _e_nn(a, b):
    return jnp.einsum("hqk,hkd->hqd", a, b, preferred_element_type=F32)


def _wkv_chunk_kernel(r_ref, ld_ref, k_ref, v_ref, a_ref, b_ref, y_ref, sf_ref, s_scr):
    c = WKV_CHUNK
    dot_nt = functools.partial(_mxu, _e_nt, precise=True)
    dot_nn = functools.partial(_mxu, _e_nn, precise=True)

    def dot_tn(x, y):
        return dot_nn(jnp.swapaxes(x, 1, 2), y)

    @pl.when(pl.program_id(1) == 0)
    def _():
        s_scr[...] = jnp.zeros_like(s_scr)

    ld = ld_ref[...]
    cum = ld
    k = 1
    while k < c:
        cum = cum + _shift_rows(cum, k)
        k *= 2
    e_pos, e_prev, e_neg = jnp.exp(cum), jnp.exp(cum - ld), jnp.exp(-cum)
    at = _split_heads(a_ref[...] * e_prev)
    rt = _split_heads(r_ref[...] * e_pos)
    bt = _split_heads(b_ref[...] * e_neg)
    kt = _split_heads(k_ref[...] * e_neg)
    v = _split_heads(v_ref[...])
    lam = _split_heads(e_pos[c - 1:c, :])

    ti = lax.broadcasted_iota(jnp.int32, (c, c), 0)
    si = lax.broadcasted_iota(jnp.int32, (c, c), 1)
    strict = ti > si

    def blockmask(size):
        same = (ti // size) == (si // size)
        return strict & same & ((ti // (size // 2)) != (si // (size // 2)))

    ar = jnp.concatenate([at, rt], axis=1)
    g_b = dot_nt(ar, bt)
    g_k = dot_nt(ar, kt)
    n_ab = jnp.where(strict, g_b[:, :c, :], 0.0)
    n_ak = jnp.where(strict, g_k[:, :c, :], 0.0)
    m_rb = jnp.where(ti >= si, g_b[:, c:, :], 0.0)
    m_rk = jnp.where(ti >= si, g_k[:, c:, :], 0.0)

    base = 8
    n8 = jnp.where((ti // base) == (si // base), n_ab, 0.0)
    eye = (ti == si).astype(F32)
    n8_2 = dot_nn(n8, n8)
    n8_4 = dot_nn(n8_2, n8_2)
    t_inv = eye + n8
    t_inv = t_inv + dot_nn(t_inv, n8_2)
    t_inv = t_inv + dot_nn(t_inv, n8_4)
    size = 2 * base
    while size <= c:
        off = jnp.where(blockmask(size), n_ab, 0.0)
        t_inv = t_inv + dot_nn(dot_nn(t_inv, off), t_inv)
        size *= 2

    wv = dot_nn(n_ak, v)
    a_bar = dot_nn(t_inv, at)
    u_bar = dot_nn(t_inv, wv)
    r_bar = rt + dot_nn(m_rb, a_bar)
    y_bar = dot_nn(m_rb, u_bar) + dot_nn(m_rk, v)
    phi = dot_tn(a_bar, bt)
    psi = dot_tn(jnp.concatenate([u_bar, v], axis=1), jnp.concatenate([bt, kt], axis=1))

    s0 = s_scr[...]
    y_ref[...] = _join_heads(dot_nt(r_bar, s0) + y_bar)
    s_new = (s0 + dot_nn(s0, phi) + psi) * lam
    s_scr[...] = s_new

    @pl.when(pl.program_id(1) == pl.num_programs(1) - 1)
    def _():
        sf_ref[...] = s_new


def _wkv_chunked(seqs):
    batch, t, _ = seqs[0].shape
    c = WKV_CHUNK
    seq_spec = pl.BlockSpec((None, c, RWKV_WIDTH), lambda b, i: (b, i, 0))
    st_shape = (batch, RWKV_HEADS, RWKV_HEAD, RWKV_HEAD)
    st_spec = pl.BlockSpec((None,) + st_shape[1:], lambda b, i: (b, 0, 0, 0))
    return pl.pallas_call(
        _wkv_chunk_kernel,
        grid=(batch, t // c),
        in_specs=[seq_spec] * 6,
        out_specs=[seq_spec, st_spec],
        out_shape=[jax.ShapeDtypeStruct((batch, t, RWKV_WIDTH), F32), jax.ShapeDtypeStruct(st_shape, F32)],
        scratch_shapes=[pltpu.VMEM(st_shape[1:], F32)],
        compiler_params=_cp("parallel", "arbitrary"), name="wkv_chunk",
    )(*seqs)


def _rwkv_post_kernel(y_ref, r_ref, k_ref, v_ref, g_ref, rk_ref, lg_ref, lb_ref, o_ref):
    y = y_ref[...]
    inv = 1.0 / RWKV_HEAD
    mean = _head_sum(y) * inv
    yc = y - mean
    var = _head_sum(yc * yc) * inv
    yn = yc * lax.rsqrt(var + RWKV_LN_EPS) * lg_ref[...] + lb_ref[...]
    v = v_ref[...]
    bonus = _head_sum(r_ref[...] * k_ref[...] * rk_ref[...]) * v
    o_ref[...] = ((yn + bonus) * g_ref[...]).astype(o_ref.dtype)


def _rwkv_post(y, r, k, v, g, prm, layer, precise, *, tm):
    rows = y.shape[0]
    depth = prm["rk"].shape[0]
    row = pl.BlockSpec((tm, RWKV_WIDTH), lambda i: (i, 0))
    vec = pl.BlockSpec((None, 1, RWKV_WIDTH), lambda i: (layer, 0, 0))
    r3 = lambda a: a.reshape(depth, 1, RWKV_WIDTH)
    return pl.pallas_call(
        _rwkv_post_kernel,
        grid=(rows // tm,),
        in_specs=[row] * 5 + [vec] * 3,
        out_specs=row,
        out_shape=jax.ShapeDtypeStruct((rows, RWKV_WIDTH), _act_dtype(precise)),
        compiler_params=_cp("parallel"), name="rwkv_post",
    )(y, r, k, v, g, r3(prm["rk"]), r3(prm["ln_g"]), r3(prm["ln_b"]))


def _moe_kernel(h_ref, comb_ref, wg_ref, wu_ref, wd_ref, o_ref):
    e = pl.program_id(1)
    f = pl.program_id(2)

    @pl.when((e == 0) & (f == 0))
    def _():
        o_ref[...] = jnp.zeros_like(o_ref)

    h = h_ref[...]
    gate = _bdot(h, wg_ref[...])
    act = gate * _sigmoid(gate) * _bdot(h, wu_ref[...])
    comb = comb_ref[...]
    lane = lax.broadcasted_iota(jnp.int32, comb.shape, 1)
    ce = jnp.sum(jnp.where(lane == e, comb, 0.0), axis=-1, keepdims=True)
    o_ref[...] += _bdot(act * ce, wd_ref[...])


def _moe(h, comb, w_gate, w_up, w_down, layer, *, tm, tf=256):
    m, d = h.shape
    return pl.pallas_call(
        _moe_kernel,
        grid=(m // tm, N_EXPERTS, D_EXPERT // tf),
        in_specs=[pl.BlockSpec((tm, d), lambda i, e, f: (i, 0)),
                  pl.BlockSpec((tm, N_EXPERTS), lambda i, e, f: (i, 0)),
                  pl.BlockSpec((None, None, d, tf), lambda i, e, f: (layer, e, 0, f)),
                  pl.BlockSpec((None, None, d, tf), lambda i, e, f: (layer, e, 0, f)),
                  pl.BlockSpec((None, None, tf, d), lambda i, e, f: (layer, e, f, 0))],
        out_specs=pl.BlockSpec((tm, d), lambda i, e, f: (i, 0)),
        out_shape=jax.ShapeDtypeStruct((m, d), F32),
        compiler_params=_cp("parallel", "arbitrary", "arbitrary"), name="moe",
    )(h, comb, w_gate, w_up, w_down)


_SH1, _SC1, _GT1, _SH2, _SC2, _GT2 = range(6)


def kernel(x_prompt, x_sample, cache_swa_k, cache_swa_v, state_pool, state_rwkv_shift, state_rwkv_wkv, state_conv, c_prompt, c_sample, w_ada, b_ada, g_norm1, g_norm2, w_in, sinks, w_pool, ls_pool, rwkv_mu, rwkv_w0, rwkv_w2, rwkv_a0, rwkv_a2, rwkv_g2, rwkv_kk, rwkv_ka, rwkv_rk, rwkv_ln_g, rwkv_ln_b, conv_w, w_branch, w_out, w_router, b_router, w_gate, w_up, w_down, g_final):
    depth = w_in.shape[0]
    batch, seq, d = x_prompt.shape
    dbatch, dseq, _ = x_sample.shape
    wbuf = cache_swa_k.shape[2]
    np_rows, ns_rows = batch * seq, dbatch * dseq

    grp_p = _Group(np_rows, seq, 1)
    grp_s = _Group(ns_rows, ns_rows, ns_rows)
    tm_p, tm_s = min(2048, seq), ns_rows
    te_p = min(512, seq)

    c_all = jnp.concatenate([c_prompt, c_sample], axis=0)
    wr_pad = jnp.pad(w_router, ((0, 0), (0, ROUTER_PAD - N_EXPERTS)))
    prm = dict(mu=rwkv_mu, w0=rwkv_w0, w2=rwkv_w2, a0=rwkv_a0, a2=rwkv_a2, g2=rwkv_g2, kk=rwkv_kk, ka=rwkv_ka,
               rk=rwkv_rk.reshape(depth, RWKV_WIDTH), ln_g=rwkv_ln_g, ln_b=rwkv_ln_b)
    cache_k = cache_swa_k.reshape(depth, dbatch, wbuf, KV_WIDTH)
    cache_v = cache_swa_v.reshape(depth, dbatch, wbuf, KV_WIDTH)

    xp = x_prompt.reshape(np_rows, d)
    xs = x_sample.reshape(ns_rows, d)
    st_p, st_s = [], []
    pend_p = pend_s = None
    for l in range(depth):
        precise = l == 0
        hd = _act_dtype(precise)
        mod = _ada(c_all, w_ada, b_ada, l, precise)
        mod_p = mod[:batch].reshape(batch, 1, 6 * d)
        mod_s = jnp.repeat(mod[batch:], dseq, axis=0).reshape(1, ns_rows, 6 * d)

        def first_norm(grp, x, pend, mod3, tm):
            if pend is None:
                return x, _normx(grp, x, g_norm1[l], mod=(mod3, _SC1, _SH1), out_dtype=hd, tm=tm)[0]
            m, mod_prev = pend
            x, h = _normx(grp, x, g_norm1[l], add=(m, mod_prev, _GT2), mod=(mod3, _SC1, _SH1), emit_x=True,
                          out_dtype=hd, tm=tm)
            return x, h

        xp, hp = first_norm(grp_p, xp, pend_p, mod_p, te_p)
        xs, hs = first_norm(grp_s, xs, pend_s, mod_s, tm_s)

        pp = _mm(hp, w_in, l, tm=min(tm_p, 1024) if precise else tm_p, tn=512, precise=precise)
        ps = _mm(hs, w_in, l, tm=tm_s, tn=512, precise=precise)

        a_p = _swa_prompt(pp, sinks, l, batch, seq, precise)
        b_p = _pool_prompt(pp, w_pool, ls_pool, l, batch, seq, precise)
        d_p, conv_new_p = _conv_prompt(pp, conv_w, l, batch, seq, precise)
        pre_p = _rwkv_pre_prompt(pp, prm, l, batch, seq, precise, tt=te_p)
        r_p, w_p, k_p, v_p, ka_p, kb_p, g_p = pre_p
        as3 = lambda t: t.reshape(batch, seq, RWKV_WIDTH)
        y_p, wkv_new_p = _wkv_chunked([as3(t) for t in (r_p, w_p, k_p, v_p, ka_p, kb_p)])
        c_p = _rwkv_post(y_p.reshape(np_rows, RWKV_WIDTH), r_p, k_p, v_p, g_p, prm, l, precise, tm=te_p)
        pp3 = pp.reshape(batch, seq, IN_WIDTH)
        kw = min(WINDOW, seq)
        st_p.append((pp3[:, seq - kw:, OFF_K:OFF_K + KV_WIDTH].reshape(batch, kw, N_KV_HEADS, HEAD_DIM),
                     pp3[:, seq - kw:, OFF_V:OFF_V + KV_WIDTH].reshape(batch, kw, N_KV_HEADS, HEAD_DIM),
                     pp3[:, seq - POOL_BUF:, OFF_U:OFF_U + POOL_WIDTH],
                     pp3[:, seq - 1, OFF_PC:OFF_PC + RWKV_PROJ],
                     wkv_new_p, conv_new_p))

        ps3 = ps.reshape(dbatch, dseq, IN_WIDTH)
        a_s, k_new_s, v_new_s = _swa_sample(ps3, cache_k, cache_v, sinks, l, precise)
        tmaj = lambda t: jnp.swapaxes(t, 0, 1)
        b_s_t, pool_new_t = _pool_sample(tmaj(ps3[:, :, OFF_U:OFF_U + POOL_WIDTH]), tmaj(state_pool[l]),
                                         w_pool, ls_pool, l, PAST_LEN, precise)
        d_s_t, conv_new_t = _conv_sample(tmaj(ps3[:, :, OFF_CB:OFF_CB + CONV_WIDTH]),
                                         tmaj(ps3[:, :, OFF_CC:OFF_CC + CONV_WIDTH]),
                                         tmaj(ps3[:, :, OFF_CX:OFF_CX + CONV_WIDTH]), tmaj(state_conv[l]), conv_w, l,
                                         precise)
        pc_s3 = ps3[:, :, OFF_PC:OFF_PC + RWKV_PROJ]
        pc_shift = jnp.concatenate([state_rwkv_shift[l][:, None, :], pc_s3[:, :-1, :]], axis=1)
        pre_s = _rwkv_pre_sample(pc_s3.reshape(ns_rows, RWKV_PROJ), pc_shift.reshape(ns_rows, RWKV_PROJ), prm, l,
                                 precise)
        r_s, w_s, k_s, v_s, ka_s, kb_s, g_s = pre_s
        as3s = lambda t: t.reshape(dbatch, dseq, RWKV_WIDTH)
        y_s, wkv_new_s = _wkv_steps([as3s(t) for t in (r_s, w_s, k_s, v_s, ka_s, kb_s)], state_rwkv_wkv[l])
        c_s = _rwkv_post(y_s.reshape(ns_rows, RWKV_WIDTH), r_s, k_s, v_s, g_s, prm, l, precise, tm=ns_rows)
        st_s.append((k_new_s.reshape(dbatch, wbuf, N_KV_HEADS, HEAD_DIM),
                     v_new_s.reshape(dbatch, wbuf, N_KV_HEADS, HEAD_DIM),
                     tmaj(pool_new_t), pc_s3[:, dseq - 1, :], wkv_new_s, tmaj(conv_new_t)))
        br_s = (a_s.reshape(ns_rows, ATTN_WIDTH), tmaj(b_s_t).reshape(ns_rows, POOL_WIDTH), c_s,
                tmaj(d_s_t).reshape(ns_rows, CONV_WIDTH))

        def tail(grp, x, branches, p, mod3, tm_mm, tm_el, tm_moe):
            if precise:
                merged = _merge(branches, p, w_branch, l, tm=min(tm_mm, 512), tn=256, precise=True)
                x1 = _mm_res(grp, merged, w_out, l, x, mod3, _GT1, tm=min(tm_mm, 512), tn=512, precise=True)
            else:
                merged = _merge(branches, p, w_branch, l, tm=min(tm_mm, 1024), tn=512, precise=False)
                x1 = _mm_res(grp, merged, w_out, l, x, mod3, _GT1, tm=tm_mm, tn=512, precise=False)
            h2, comb = _normx(grp, x1, g_norm2[l], mod=(mod3, _SC2, _SH2), route=(wr_pad, b_router), tm=tm_el)
            return x1, _moe(h2, comb, w_gate, w_up, w_down, l, tm=tm_moe)

        xp, m_p = tail(grp_p, xp, (a_p, b_p, c_p, d_p), pp, mod_p, tm_p, te_p, min(1024, np_rows))
        xs, m_s = tail(grp_s, xs, br_s, ps, mod_s, tm_s, tm_s, tm_s)
        pend_p, pend_s = (m_p, mod_p), (m_s, mod_s)

    y_p = _normx(grp_p, xp, g_final, add=(pend_p[0], pend_p[1], _GT2), out_dtype=F32, tm=te_p)[0]
    y_s = _normx(grp_s, xs, g_final, add=(pend_s[0], pend_s[1], _GT2), out_dtype=F32, tm=tm_s)[0]

    def stack(states, i):
        return jnp.stack([s[i] for s in states])

    return ((y_p.reshape(batch, seq, d), y_s.reshape(dbatch, dseq, d))
            + tuple(stack(st_p, i) for i in range(6)) + tuple(stack(st_s, i) for i in range(6)))
```

```python
import functools

import jax
import jax.numpy as jnp
from jax import lax
from jax.experimental import pallas as pl
from jax.experimental.pallas import tpu as pltpu

F32 = jnp.float32
BF16 = jnp.bfloat16

D_MODEL = 2048
PAST_LEN = 8192
WINDOW = 128
HEAD_DIM = 64
N_HEADS = 16
N_KV_HEADS = 4
GQA_GROUP = N_HEADS // N_KV_HEADS
ATTN_WIDTH = N_HEADS * HEAD_DIM
KV_WIDTH = N_KV_HEADS * HEAD_DIM
ATTN_SCALE = HEAD_DIM ** -0.5
NEG_INF = -1e30
POOL_WINDOWS = (2, 4, 8, 16)
POOL_WIDTH = 768
POOL_GW = POOL_WIDTH // len(POOL_WINDOWS)
POOL_BUF = max(POOL_WINDOWS) - 1
RWKV_HEAD = 64
RWKV_WIDTH = 768
RWKV_HEADS = RWKV_WIDTH // RWKV_HEAD
DECAY_LORA = 64
AAA_LORA = 64
GATE_LORA = 128
RWKV_PROJ = 3 * RWKV_WIDTH + DECAY_LORA + AAA_LORA + GATE_LORA
RWKV_LN_EPS = 64e-5
CONV_WIDTH = 768
CONV_K = 3
N_BRANCH = 4
BRANCH_SECTIONS = (ATTN_WIDTH, POOL_WIDTH, RWKV_WIDTH, CONV_WIDTH)
MIX_WIDTH = sum(BRANCH_SECTIONS)
N_EXPERTS = 16
N_GROUPS = 4
EXP_PER_GROUP = N_EXPERTS // N_GROUPS
TOP_K = 2
D_EXPERT = 1024
RMS_EPS = 1e-6

OFF_Q = 0
OFF_K = OFF_Q + ATTN_WIDTH
OFF_V = OFF_K + KV_WIDTH
OFF_U = OFF_V + KV_WIDTH
OFF_PC = OFF_U + POOL_WIDTH
OFF_CB = OFF_PC + RWKV_PROJ
OFF_CC = OFF_CB + CONV_WIDTH
OFF_CX = OFF_CC + CONV_WIDTH
OFF_GL = OFF_CX + CONV_WIDTH
IN_WIDTH = OFF_GL + N_BRANCH * D_MODEL

LANES = 128
ROUTER_PAD = LANES
VMEM_LIMIT = 60 * 1024 * 1024


def _cp(*sem):
    return pltpu.CompilerParams(dimension_semantics=sem, vmem_limit_bytes=VMEM_LIMIT)


def _sigmoid(x):
    return 1.0 / (1.0 + jnp.exp(-x))


def _split(x):
    hi = x.astype(BF16)
    return hi, (x.astype(F32) - hi.astype(F32)).astype(BF16)


def _mxu(contract, a, b, precise):
    if not precise:
        return contract(a.astype(BF16), b.astype(BF16))
    ah, al = _split(a)
    bh, bl = _split(b)
    return contract(ah, bh) + (contract(ah, bl) + contract(al, bh))


def _dot2(a, b):
    return jnp.dot(a, b, preferred_element_type=F32)


def _bdot(a, b, precise=False):
    return _mxu(_dot2, a, b, precise)


def _act_dtype(precise):
    return F32 if precise else BF16


def _ada_kernel(c_ref, w_ref, b_ref, o_ref, *, precise):
    c = c_ref[...]
    o_ref[...] = _bdot(c * _sigmoid(c), w_ref[...], precise) + b_ref[...]


def _ada(c_all, w_ada, b_ada, layer, precise):
    depth, d, n = w_ada.shape
    nb = c_all.shape[0]
    tn = 1024
    return pl.pallas_call(
        functools.partial(_ada_kernel, precise=precise),
        grid=(n // tn,),
        in_specs=[pl.BlockSpec((nb, d), lambda j: (0, 0)),
                  pl.BlockSpec((None, d, tn), lambda j: (layer, 0, j)),
                  pl.BlockSpec((None, 1, tn), lambda j: (layer, 0, j))],
        out_specs=pl.BlockSpec((nb, tn), lambda j: (0, j)),
        out_shape=jax.ShapeDtypeStruct((nb, n), F32),
        compiler_params=_cp("parallel"),
        name="ada",
    )(c_all, w_ada, b_ada.reshape(depth, 1, n))


class _Group:
    def __init__(self, rows, rpm, mod_rows):
        self.rows, self.rpm, self.mod_rows = rows, rpm, mod_rows

    def mod_spec(self, tm, width, col_of, row_axis=0):
        if self.mod_rows == 1:
            per = self.rpm // tm
            return pl.BlockSpec((None, 1, width), lambda *idx: (idx[row_axis] // per, 0, col_of(idx)))
        assert tm == self.rpm == self.mod_rows
        return pl.BlockSpec((None, tm, width), lambda *idx: (idx[row_axis], 0, col_of(idx)))


def _route(y, wr, br):
    y_hi = y.astype(BF16)
    y_lo = (y - y_hi.astype(F32)).astype(BF16)
    w_hi = wr.astype(BF16)
    w_lo = (wr - w_hi.astype(F32)).astype(BF16)
    logits = (jnp.dot(y_hi, w_hi, preferred_element_type=F32)
              + (jnp.dot(y_hi, w_lo, preferred_element_type=F32) + jnp.dot(y_lo, w_hi, preferred_element_type=F32)))
    logits = logits[:, :N_EXPERTS] + br
    tm = logits.shape[0]
    e = jnp.exp(logits - jnp.max(logits, axis=-1, keepdims=True))
    probs = e / jnp.sum(e, axis=-1, keepdims=True)
    iota_g = lax.broadcasted_iota(jnp.int32, (tm, EXP_PER_GROUP), 1)
    best = None
    for g in range(N_GROUPS):
        pg = probs[:, g * EXP_PER_GROUP:(g + 1) * EXP_PER_GROUP]
        m1 = jnp.max(pg, axis=-1, keepdims=True)
        i1 = jnp.min(jnp.where(pg == m1, iota_g, EXP_PER_GROUP), axis=-1, keepdims=True)
        rest = jnp.where(iota_g == i1, -1.0, pg)
        m2 = jnp.max(rest, axis=-1, keepdims=True)
        i2 = jnp.min(jnp.where(rest == m2, iota_g, EXP_PER_GROUP), axis=-1, keepdims=True)
        cand = (m1 + m2, m1, m2, i1 + g * EXP_PER_GROUP, i2 + g * EXP_PER_GROUP)
        if best is None:
            best = cand
        else:
            take = cand[0] > best[0]
            best = tuple(jnp.where(take, c, b) for c, b in zip(cand, best))
    _, m1, m2, e1, e2 = best
    den = m1 + m2
    slot = lax.broadcasted_iota(jnp.int32, (tm, TOP_K), 1)
    return jnp.where(slot == 0, e1, e2), jnp.where(slot == 0, m1 / den, m2 / den)


def _normx_kernel(*refs, has_add, has_mod, has_route, emit_x):
    it = iter(refs)
    x_ref = next(it)
    if has_add:
        m_ref, gate_ref = next(it), next(it)
    g_ref = next(it)
    if has_mod:
        sc_ref, sh_ref = next(it), next(it)
    if has_route:
        wr_ref, br_ref = next(it), next(it)
    if emit_x:
        xo_ref = next(it)
    h_ref = next(it)
    if has_route:
        eidx_ref, wts_ref = next(it), next(it)
    x = x_ref[...]
    if has_add:
        x = x + gate_ref[...] * m_ref[...]
    if emit_x:
        xo_ref[...] = x
    y = x * lax.rsqrt(jnp.mean(x * x, axis=-1, keepdims=True) + RMS_EPS) * g_ref[...]
    if has_mod:
        y = y * (1.0 + sc_ref[...]) + sh_ref[...]
    h_ref[...] = y.astype(h_ref.dtype)
    if has_route:
        eidx_ref[...], wts_ref[...] = _route(y, wr_ref[...], br_ref[...])


def _normx(grp, x, g, *, add=None, mod=None, route=None, emit_x=False, out_dtype=BF16, tm=512):
    n, d = x.shape
    row = pl.BlockSpec((tm, d), lambda i: (i, 0))
    vec = pl.BlockSpec((1, d), lambda i: (0, 0))
    args, specs = [x], [row]
    if add is not None:
        m, m_row0, mod3, chunk = add
        args += [m, mod3]
        specs += [pl.BlockSpec((tm, d), lambda i, o=m_row0 // tm: (i + o, 0)),
                  grp.mod_spec(tm, d, lambda idx, c=chunk: c)]
    args.append(g.reshape(1, d))
    specs.append(vec)
    if mod is not None:
        mod3, c_sc, c_sh = mod
        args += [mod3, mod3]
        specs += [grp.mod_spec(tm, d, lambda idx, c=c_sc: c), grp.mod_spec(tm, d, lambda idx, c=c_sh: c)]
    if route is not None:
        wr, br = route
        args += [wr, br.reshape(1, N_EXPERTS)]
        specs += [pl.BlockSpec((d, ROUTER_PAD), lambda i: (0, 0)), pl.BlockSpec((1, N_EXPERTS), lambda i: (0, 0))]
    out_shape, out_specs = [], []
    if emit_x:
        out_shape.append(jax.ShapeDtypeStruct((n, d), F32))
        out_specs.append(row)
    out_shape.append(jax.ShapeDtypeStruct((n, d), out_dtype))
    out_specs.append(row)
    if route is not None:
        out_shape += [jax.ShapeDtypeStruct((n, TOP_K), jnp.int32), jax.ShapeDtypeStruct((n, TOP_K), F32)]
        out_specs += [pl.BlockSpec((tm, TOP_K), lambda i: (i, 0))] * 2
    return pl.pallas_call(
        functools.partial(_normx_kernel, has_add=add is not None, has_mod=mod is not None,
                          has_route=route is not None, emit_x=emit_x),
        grid=(n // tm,), in_specs=specs, out_specs=out_specs, out_shape=out_shape,
        compiler_params=_cp("parallel"), name="normx",
    )(*args)


def _mm_kernel(a_ref, w_ref, o_ref, *, precise):
    o_ref[...] = _bdot(a_ref[...], w_ref[...], precise).astype(o_ref.dtype)


def _mm(a, w3, layer, *, tm, tn, precise, out_dtype=F32):
    m, k = a.shape
    n = w3.shape[-1]
    return pl.pallas_call(
        functools.partial(_mm_kernel, precise=precise),
        grid=(m // tm, n // tn),
        in_specs=[pl.BlockSpec((tm, k), lambda i, j: (i, 0)),
                  pl.BlockSpec((None, k, tn), lambda i, j: (layer, 0, j))],
        out_specs=pl.BlockSpec((tm, tn), lambda i, j: (i, j)),
        out_shape=jax.ShapeDtypeStruct((m, n), out_dtype),
        compiler_params=_cp("parallel", "parallel"), name="mm_in",
    )(a, w3)


def _mm_res_kernel(a_ref, w_ref, x_ref, gate_ref, o_ref, *, precise):
    o_ref[...] = x_ref[...] + gate_ref[...] * _bdot(a_ref[...], w_ref[...], precise)


def _mm_res(grp, a, w3, layer, x, mod3, gate_chunk, *, tm, tn, precise):
    m, k = a.shape
    n = w3.shape[-1]
    per_chunk = n // tn
    return pl.pallas_call(
        functools.partial(_mm_res_kernel, precise=precise),
        grid=(m // tm, n // tn),
        in_specs=[pl.BlockSpec((tm, k), lambda i, j: (i, 0)),
                  pl.BlockSpec((None, k, tn), lambda i, j: (layer, 0, j)),
                  pl.BlockSpec((tm, tn), lambda i, j: (i, j)),
                  grp.mod_spec(tm, tn, lambda idx: gate_chunk * per_chunk + idx[1])],
        out_specs=pl.BlockSpec((tm, tn), lambda i, j: (i, j)),
        out_shape=jax.ShapeDtypeStruct((m, n), F32),
        compiler_params=_cp("parallel", "parallel"), name="mm_out",
    )(a, w3, x, mod3)


def _merge_kernel(a_ref, b_ref, c_ref, d_ref, g0_ref, g1_ref, g2_ref, g3_ref, w_ref, o_ref, *, precise):
    acc = None
    lo = 0
    for br_ref, g_ref, width in zip((a_ref, b_ref, c_ref, d_ref), (g0_ref, g1_ref, g2_ref, g3_ref), BRANCH_SECTIONS):
        t = _sigmoid(g_ref[...]) * _bdot(br_ref[...], w_ref[lo:lo + width, :], precise)
        acc = t if acc is None else acc + t
        lo += width
    o_ref[...] = acc.astype(o_ref.dtype)


def _merge(branches, p, w_branch, layer, *, tm, tn, precise):
    m = p.shape[0]
    gl_blk = OFF_GL // tn
    per = D_MODEL // tn
    br_specs = [pl.BlockSpec((tm, w), lambda i, j: (i, 0)) for w in BRANCH_SECTIONS]
    gl_specs = [pl.BlockSpec((tm, tn), lambda i, j, b=b: (i, gl_blk + b * per + j)) for b in range(N_BRANCH)]
    return pl.pallas_call(
        functools.partial(_merge_kernel, precise=precise),
        grid=(m // tm, D_MODEL // tn),
        in_specs=br_specs + gl_specs + [pl.BlockSpec((None, MIX_WIDTH, tn), lambda i, j: (layer, 0, j))],
        out_specs=pl.BlockSpec((tm, tn), lambda i, j: (i, j)),
        out_shape=jax.ShapeDtypeStruct((m, D_MODEL), _act_dtype(precise)),
        compiler_params=_cp("parallel", "parallel"), name="merge",
    )(*branches, p, p, p, p, w_branch)


def _sink_col(sink_ref, layer, kh, rows_per_head):
    return jnp.concatenate([jnp.full((rows_per_head, 1), sink_ref[layer, kh * GQA_GROUP + g], F32)
                            for g in range(GQA_GROUP)], axis=0)


def _dot_nt(a, b):
    return lax.dot_general(a, b, (((1,), (1,)), ((), ())), preferred_element_type=F32)


def _swa_prompt_kernel(sink_ref, q_ref, kc_ref, kp_ref, vc_ref, vp_ref, o_ref, *, layer, precise):
    n = pl.program_id(1)
    w = WINDOW
    q = q_ref[...]
    qi = jnp.bitwise_and(lax.broadcasted_iota(jnp.int32, (GQA_GROUP * w, 2 * w), 0), w - 1)
    sj = lax.broadcasted_iota(jnp.int32, (GQA_GROUP * w, 2 * w), 1)
    valid = (sj >= qi) & (sj <= qi + w) & ((sj >= w) | (n > 0))
    outs = []
    for kh in range(N_KV_HEADS):
        sl = slice(kh * HEAD_DIM, (kh + 1) * HEAD_DIM)
        k2 = jnp.concatenate([kp_ref[:, sl], kc_ref[:, sl]], axis=0)
        v2 = jnp.concatenate([vp_ref[:, sl], vc_ref[:, sl]], axis=0)
        q4 = jnp.concatenate([q[:, (kh * GQA_GROUP + g) * HEAD_DIM:(kh * GQA_GROUP + g + 1) * HEAD_DIM]
                              for g in range(GQA_GROUP)], axis=0)
        s = _mxu(_dot_nt, q4, k2, precise) * ATTN_SCALE
        s = jnp.where(valid, s, NEG_INF)
        sk = _sink_col(sink_ref, layer, kh, w)
        m = jnp.maximum(jnp.max(s, axis=-1, keepdims=True), sk)
        e = jnp.exp(s - m)
        p = e / (jnp.sum(e, axis=-1, keepdims=True) + jnp.exp(sk - m))
        o4 = _bdot(p, v2, precise)
        outs += [o4[g * w:(g + 1) * w] for g in range(GQA_GROUP)]
    o_ref[...] = jnp.concatenate(outs, axis=1).astype(o_ref.dtype)


def _swa_prompt(p, sinks, layer, batch, seq, precise):
    nb = seq // WINDOW
    kblk, vblk = OFF_K // KV_WIDTH, OFF_V // KV_WIDTH

    def cur(col):
        return lambda b, n: (b * nb + n, col)

    def prev(col):
        return lambda b, n: (b * nb + jnp.maximum(n - 1, 0), col)

    return pl.pallas_call(
        functools.partial(_swa_prompt_kernel, layer=layer, precise=precise),
        grid=(batch, nb),
        in_specs=[pl.BlockSpec(memory_space=pltpu.SMEM),
                  pl.BlockSpec((WINDOW, ATTN_WIDTH), cur(0)),
                  pl.BlockSpec((WINDOW, KV_WIDTH), cur(kblk)), pl.BlockSpec((WINDOW, KV_WIDTH), prev(kblk)),
                  pl.BlockSpec((WINDOW, KV_WIDTH), cur(vblk)), pl.BlockSpec((WINDOW, KV_WIDTH), prev(vblk))],
        out_specs=pl.BlockSpec((WINDOW, ATTN_WIDTH), lambda b, n: (b * nb + n, 0)),
        out_shape=jax.ShapeDtypeStruct((batch * seq, ATTN_WIDTH), _act_dtype(precise)),
        compiler_params=_cp("parallel", "parallel"), name="swa_prompt",
    )(sinks, p, p, p, p, p)


def _qk(a, b):
    return jnp.einsum("bqd,bkd->bqk", a, b, preferred_element_type=F32)


def _pv(a, b):
    return jnp.einsum("bqk,bkd->bqd", a, b, preferred_element_type=F32)


def _swa_sample_kernel(sink_ref, q_ref, kn_ref, vn_ref, kc_ref, vc_ref, o_ref, ko_ref, vo_ref, *, layer, steps, wbuf,
                       precise):
    q = q_ref[...]
    kn, vn = kn_ref[...], vn_ref[...]
    kc, vc = kc_ref[...], vc_ref[...]
    ko_ref[:, :wbuf - steps, :] = kc[:, steps:, :]
    ko_ref[:, wbuf - steps:, :] = kn
    vo_ref[:, :wbuf - steps, :] = vc[:, steps:, :]
    vo_ref[:, wbuf - steps:, :] = vn
    bb = q.shape[0]
    rows = GQA_GROUP * steps
    t_c = lax.rem(lax.broadcasted_iota(jnp.int32, (bb, rows, wbuf), 1), steps)
    j_c = lax.broadcasted_iota(jnp.int32, (bb, rows, wbuf), 2)
    dist_c = t_c + wbuf - j_c
    valid_c = (dist_c >= 0) & (dist_c <= WINDOW)
    t_n = lax.rem(lax.broadcasted_iota(jnp.int32, (bb, rows, steps), 1), steps)
    j_n = lax.broadcasted_iota(jnp.int32, (bb, rows, steps), 2)
    valid_n = (t_n - j_n >= 0) & (t_n - j_n <= WINDOW)
    outs = [None] * N_HEADS
    for kh in range(N_KV_HEADS):
        sl = slice(kh * HEAD_DIM, (kh + 1) * HEAD_DIM)
        qg = jnp.concatenate([q[:, :, (kh * GQA_GROUP + g) * HEAD_DIM:(kh * GQA_GROUP + g + 1) * HEAD_DIM]
                              for g in range(GQA_GROUP)], axis=1)
        s_c = _mxu(_qk, qg, kc[:, :, sl], precise) * ATTN_SCALE
        s_n = _mxu(_qk, qg, kn[:, :, sl], precise) * ATTN_SCALE
        s_c = jnp.where(valid_c, s_c, NEG_INF)
        s_n = jnp.where(valid_n, s_n, NEG_INF)
        sk = _sink_col(sink_ref, layer, kh, steps)[None]
        m = jnp.maximum(jnp.maximum(jnp.max(s_c, axis=-1, keepdims=True), jnp.max(s_n, axis=-1, keepdims=True)), sk)
        e_c, e_n = jnp.exp(s_c - m), jnp.exp(s_n - m)
        den = jnp.sum(e_c, axis=-1, keepdims=True) + jnp.sum(e_n, axis=-1, keepdims=True) + jnp.exp(sk - m)
        o = _mxu(_pv, e_c / den, vc[:, :, sl], precise) + _mxu(_pv, e_n / den, vn[:, :, sl], precise)
        for g in range(GQA_GROUP):
            outs[kh * GQA_GROUP + g] = o[:, g * steps:(g + 1) * steps, :]
    o_ref[...] = jnp.concatenate(outs, axis=2).astype(o_ref.dtype)


def _swa_sample(p3, cache_k, cache_v, sinks, layer, precise, *, bb=8):
    batch, steps, _ = p3.shape
    wbuf = cache_k.shape[2]
    kblk, vblk = OFF_K // KV_WIDTH, OFF_V // KV_WIDTH
    cache_spec = pl.BlockSpec((None, bb, wbuf, KV_WIDTH), lambda i: (layer, i, 0, 0))
    new_spec = pl.BlockSpec((bb, wbuf, KV_WIDTH), lambda i: (i, 0, 0))
    return pl.pallas_call(
        functools.partial(_swa_sample_kernel, layer=layer, steps=steps, wbuf=wbuf, precise=precise),
        grid=(batch // bb,),
        in_specs=[pl.BlockSpec(memory_space=pltpu.SMEM),
                  pl.BlockSpec((bb, steps, ATTN_WIDTH), lambda i: (i, 0, 0)),
                  pl.BlockSpec((bb, steps, KV_WIDTH), lambda i: (i, 0, kblk)),
                  pl.BlockSpec((bb, steps, KV_WIDTH), lambda i: (i, 0, vblk)),
                  cache_spec, cache_spec],
        out_specs=[pl.BlockSpec((bb, steps, ATTN_WIDTH), lambda i: (i, 0, 0)), new_spec, new_spec],
        out_shape=[jax.ShapeDtypeStruct((batch, steps, ATTN_WIDTH), _act_dtype(precise)),
                   jax.ShapeDtypeStruct((batch, wbuf, KV_WIDTH), F32),
                   jax.ShapeDtypeStruct((batch, wbuf, KV_WIDTH), F32)],
        compiler_params=_cp("parallel"), name="swa_sample",
    )(sinks, p3, p3, p3, cache_k, cache_v)


def _shift_rows(x, k):
    rows = lax.broadcasted_iota(jnp.int32, x.shape, 0)
    return jnp.where(rows >= k, pltpu.roll(x, k, axis=0), 0.0)


def _pool_prompt_kernel(u_ref, w_ref, ls_ref, o_ref, *, precise):
    u = u_ref[...]
    t = u.shape[0]
    pos1 = (lax.broadcasted_iota(jnp.int32, (t, 1), 0) + 1).astype(F32)
    sums = {1: u}
    win = 1
    while win < max(POOL_WINDOWS):
        sums[2 * win] = sums[win] + _shift_rows(sums[win], win)
        win *= 2
    outs = []
    for gi, win in enumerate(POOL_WINDOWS):
        sl = slice(gi * POOL_GW, (gi + 1) * POOL_GW)
        cnt = jnp.minimum(float(win), pos1)
        d = sums[win][:, sl] / cnt - u[:, sl]
        outs.append(_bdot(d, w_ref[gi], precise))
    o_ref[...] = (jnp.concatenate(outs, axis=1) * ls_ref[...]).astype(o_ref.dtype)


def _pool_prompt(p, w_pool, ls_pool, layer, batch, seq, precise):
    gw = POOL_GW
    return pl.pallas_call(
        functools.partial(_pool_prompt_kernel, precise=precise),
        grid=(batch,),
        in_specs=[pl.BlockSpec((seq, POOL_WIDTH), lambda b: (b, OFF_U // POOL_WIDTH)),
                  pl.BlockSpec((None, len(POOL_WINDOWS), gw, gw), lambda b: (layer, 0, 0, 0)),
                  pl.BlockSpec((None, 1, POOL_WIDTH), lambda b: (layer, 0, 0))],
        out_specs=pl.BlockSpec((seq, POOL_WIDTH), lambda b: (b, 0)),
        out_shape=jax.ShapeDtypeStruct((batch * seq, POOL_WIDTH), _act_dtype(precise)),
        compiler_params=_cp("parallel"), name="pool_prompt",
    )(p, w_pool, ls_pool.reshape(ls_pool.shape[0], 1, POOL_WIDTH))


def _pool_sample_kernel(u_ref, past_ref, w_ref, ls_ref, o_ref, new_ref, *, pos0, precise):
    steps, hist = u_ref.shape[0], past_ref.shape[0]
    full = [past_ref[i] for i in range(hist)] + [u_ref[i] for i in range(steps)]
    for i in range(hist):
        new_ref[i] = full[steps + i]
    ds = [[] for _ in POOL_WINDOWS]
    for t in range(steps):
        for gi, win in enumerate(POOL_WINDOWS):
            sl = slice(gi * POOL_GW, (gi + 1) * POOL_GW)
            wsum = full[hist + t][:, sl]
            for s in range(1, win):
                wsum = wsum + full[hist + t - s][:, sl]
            cnt = float(min(win, pos0 + t + 1))
            ds[gi].append(wsum / cnt - full[hist + t][:, sl])
    ys = [_bdot(jnp.concatenate(ds[gi], axis=0), w_ref[gi], precise) for gi in range(len(POOL_WINDOWS))]
    y = jnp.concatenate(ys, axis=1) * ls_ref[...]
    nb = u_ref.shape[1]
    for t in range(steps):
        o_ref[t] = y[t * nb:(t + 1) * nb].astype(o_ref.dtype)


def _pool_sample(u_t, past_t, w_pool, ls_pool, layer, pos0, precise):
    steps, nb, _ = u_t.shape
    return pl.pallas_call(
        functools.partial(_pool_sample_kernel, pos0=pos0, precise=precise),
        grid=(1,),
        in_specs=[pl.BlockSpec(u_t.shape, lambda i: (0, 0, 0)),
                  pl.BlockSpec(past_t.shape, lambda i: (0, 0, 0)),
                  pl.BlockSpec((None, len(POOL_WINDOWS), POOL_GW, POOL_GW), lambda i: (layer, 0, 0, 0)),
                  pl.BlockSpec((None, 1, POOL_WIDTH), lambda i: (layer, 0, 0))],
        out_specs=[pl.BlockSpec(u_t.shape, lambda i: (0, 0, 0)), pl.BlockSpec(past_t.shape, lambda i: (0, 0, 0))],
        out_shape=[jax.ShapeDtypeStruct(u_t.shape, _act_dtype(precise)), jax.ShapeDtypeStruct(past_t.shape, F32)],
        compiler_params=_cp("arbitrary"), name="pool_sample",
    )(u_t, past_t, w_pool, ls_pool.reshape(ls_pool.shape[0], 1, POOL_WIDTH))


def _conv_prompt_kernel(cb_ref, cc_ref, cx_ref, w_ref, o_ref, new_ref):
    z = cc_ref[...] * cx_ref[...]
    w = w_ref[...]
    y = w[CONV_K - 1:CONV_K] * z
    for j in range(1, CONV_K):
        y = y + w[CONV_K - 1 - j:CONV_K - j] * _shift_rows(z, j)
    o_ref[...] = (cb_ref[...] * y).astype(o_ref.dtype)
    new_ref[...] = z[z.shape[0] - (CONV_K - 1):]


def _conv_prompt(p, conv_w, layer, batch, seq, precise, *, tc=256):
    nc = CONV_WIDTH // tc

    def col(off):
        return lambda b, c: (b, off // tc + c)

    return pl.pallas_call(
        _conv_prompt_kernel,
        grid=(batch, nc),
        in_specs=[pl.BlockSpec((seq, tc), col(OFF_CB)), pl.BlockSpec((seq, tc), col(OFF_CC)),
                  pl.BlockSpec((seq, tc), col(OFF_CX)),
                  pl.BlockSpec((None, CONV_K, tc), lambda b, c: (layer, 0, c))],
        out_specs=[pl.BlockSpec((seq, tc), lambda b, c: (b, c)),
                   pl.BlockSpec((None, CONV_K - 1, tc), lambda b, c: (b, 0, c))],
        out_shape=[jax.ShapeDtypeStruct((batch * seq, CONV_WIDTH), _act_dtype(precise)),
                   jax.ShapeDtypeStruct((batch, CONV_K - 1, CONV_WIDTH), F32)],
        compiler_params=_cp("parallel", "parallel"), name="conv_prompt",
    )(p, p, p, conv_w)


def _conv_sample_kernel(cb_ref, cc_ref, cx_ref, past_ref, w_ref, o_ref, new_ref):
    steps, hist = cb_ref.shape[0], past_ref.shape[0]
    w = w_ref[...]
    full = [past_ref[i] for i in range(hist)] + [cc_ref[t] * cx_ref[t] for t in range(steps)]
    for t in range(steps):
        y = w[0:1] * full[t]
        for j in range(1, CONV_K):
            y = y + w[j:j + 1] * full[t + j]
        o_ref[t] = (cb_ref[t] * y).astype(o_ref.dtype)
    for i in range(hist):
        new_ref[i] = full[steps + i]


def _conv_sample(cb_t, cc_t, cx_t, past_t, conv_w, layer, precise):
    full3 = lambda shape: pl.BlockSpec(shape, lambda i: (0, 0, 0))
    return pl.pallas_call(
        _conv_sample_kernel,
        grid=(1,),
        in_specs=[full3(cb_t.shape), full3(cc_t.shape), full3(cx_t.shape), full3(past_t.shape),
                  pl.BlockSpec((None, CONV_K, CONV_WIDTH), lambda i: (layer, 0, 0))],
        out_specs=[full3(cb_t.shape), full3(past_t.shape)],
        out_shape=[jax.ShapeDtypeStruct(cb_t.shape, _act_dtype(precise)), jax.ShapeDtypeStruct(past_t.shape, F32)],
        compiler_params=_cp("arbitrary"), name="conv_sample",
    )(cb_t, cc_t, cx_t, past_t, conv_w)


def _head_sum(x):
    rows = x.shape[0]
    return jnp.concatenate(
        [jnp.broadcast_to(jnp.sum(x[:, h * RWKV_HEAD:(h + 1) * RWKV_HEAD], axis=-1, keepdims=True), (rows, RWKV_HEAD))
         for h in range(RWKV_HEADS)], axis=1)


def _softplus(x):
    return jnp.maximum(x, 0.0) + jnp.log(1.0 + jnp.exp(-jnp.abs(x)))


def _rwkv_pre_core(cur, sh, mu, w0, w2, a0, a2, g2, k_k, k_a, outs, precise):
    xr, xk, xv, xwa, xg = [c + (s - c) * m for c, s, m in zip(cur, sh, mu)]
    wd, ad = xwa[:, :DECAY_LORA], xwa[:, DECAY_LORA:]
    w_log = -_softplus(-(w0 + _bdot(jnp.tanh(wd), w2, precise))) - 0.5
    log_decay = -jnp.exp(w_log)
    a = _sigmoid(a0 + _bdot(ad, a2, precise))
    g = _bdot(_sigmoid(xg), g2, precise)
    kk = xk * k_k
    kk = kk / jnp.maximum(jnp.sqrt(_head_sum(kk * kk)), 1e-12)
    kf = xk * (1.0 + (a - 1.0) * k_a)
    r_ref, w_ref, k_ref, v_ref, a_ref, b_ref, g_ref = outs
    r_ref[...] = xr
    w_ref[...] = log_decay
    k_ref[...] = kf
    v_ref[...] = xv
    a_ref[...] = -kk
    b_ref[...] = kk * a
    g_ref[...] = g


_PRE_WIDTHS = (RWKV_WIDTH, RWKV_WIDTH, RWKV_WIDTH, DECAY_LORA + AAA_LORA, GATE_LORA)
_PRE_OFFS = (0, RWKV_WIDTH, 2 * RWKV_WIDTH, 3 * RWKV_WIDTH, 3 * RWKV_WIDTH + DECAY_LORA + AAA_LORA)
_HALO = 8


def _rwkv_pre_prompt_kernel(*refs, precise):
    cur_refs, halo_refs, mu_refs = refs[0:5], refs[5:10], refs[10:15]
    w0, w2, a0, a2, g2, k_k, k_a = [r[...] for r in refs[15:22]]
    outs = refs[22:]
    first = pl.program_id(1) == 0
    cur, sh = [], []
    for c_ref, h_ref in zip(cur_refs, halo_refs):
        c = c_ref[...]
        prev_row = jnp.where(first, 0.0, h_ref[_HALO - 1:_HALO, :])
        rows = lax.broadcasted_iota(jnp.int32, c.shape, 0)
        sh.append(jnp.where(rows == 0, prev_row, pltpu.roll(c, 1, axis=0)))
        cur.append(c)
    _rwkv_pre_core(cur, sh, [m[...] for m in mu_refs], w0, w2, a0, a2, g2, k_k, k_a, outs, precise)


def _rwkv_param_specs(layer):
    def spec(shape):
        return pl.BlockSpec((None,) + shape, lambda *idx: (layer,) + (0,) * len(shape))

    return [spec((1, RWKV_WIDTH)), spec((DECAY_LORA, RWKV_WIDTH)), spec((1, RWKV_WIDTH)),
            spec((AAA_LORA, RWKV_WIDTH)), spec((GATE_LORA, RWKV_WIDTH)), spec((1, RWKV_WIDTH)), spec((1, RWKV_WIDTH))]


def _rwkv_params(prm):
    depth = prm["w0"].shape[0]
    r3 = lambda a: a.reshape(depth, 1, RWKV_WIDTH)
    return [r3(prm["w0"]), prm["w2"], r3(prm["a0"]), prm["a2"], prm["g2"], r3(prm["kk"]), r3(prm["ka"])]


def _rwkv_pre_prompt(p, prm, layer, batch, seq, precise, *, tt=512):
    nt = seq // tt
    cur_specs, halo_specs, mu_specs = [], [], []
    for w, off in zip(_PRE_WIDTHS, _PRE_OFFS):
        cb = (OFF_PC + off) // w
        cur_specs.append(pl.BlockSpec((tt, w), lambda b, t, cb=cb: (b * nt + t, cb)))
        halo_specs.append(pl.BlockSpec(
            (_HALO, w), lambda b, t, cb=cb: (jnp.maximum((b * nt + t) * (tt // _HALO) - 1, 0), cb)))
        mu_specs.append(pl.BlockSpec((None, 1, w), lambda b, t, mb=off // w: (layer, 0, mb)))
    out_spec = pl.BlockSpec((tt, RWKV_WIDTH), lambda b, t: (b * nt + t, 0))
    mu3 = prm["mu"].reshape(prm["mu"].shape[0], 1, RWKV_PROJ)
    return pl.pallas_call(
        functools.partial(_rwkv_pre_prompt_kernel, precise=precise),
        grid=(batch, nt),
        in_specs=cur_specs + halo_specs + mu_specs + _rwkv_param_specs(layer),
        out_specs=[out_spec] * 7,
        out_shape=[jax.ShapeDtypeStruct((batch * seq, RWKV_WIDTH), F32)] * 7,
        compiler_params=_cp("parallel", "parallel"), name="rwkv_pre_prompt",
    )(*([p] * 10), *([mu3] * 5), *_rwkv_params(prm))


def _rwkv_pre_sample_kernel(*refs, precise):
    cur = [r[...] for r in refs[0:5]]
    sh = [r[...] for r in refs[5:10]]
    mu = [r[...] for r in refs[10:15]]
    w0, w2, a0, a2, g2, k_k, k_a = [r[...] for r in refs[15:22]]
    _rwkv_pre_core(cur, sh, mu, w0, w2, a0, a2, g2, k_k, k_a, refs[22:], precise)


def _rwkv_pre_sample(pc, pc_shifted, prm, layer, precise):
    rows = pc.shape[0]
    cur_specs, mu_specs = [], []
    for w, off in zip(_PRE_WIDTHS, _PRE_OFFS):
        cur_specs.append(pl.BlockSpec((rows, w), lambda i, cb=off // w: (0, cb)))
        mu_specs.append(pl.BlockSpec((None, 1, w), lambda i, mb=off // w: (layer, 0, mb)))
    out_spec = pl.BlockSpec((rows, RWKV_WIDTH), lambda i: (0, 0))
    mu3 = prm["mu"].reshape(prm["mu"].shape[0], 1, RWKV_PROJ)
    return pl.pallas_call(
        functools.partial(_rwkv_pre_sample_kernel, precise=precise),
        grid=(1,),
        in_specs=cur_specs + cur_specs + mu_specs + _rwkv_param_specs(layer),
        out_specs=[out_spec] * 7,
        out_shape=[jax.ShapeDtypeStruct((rows, RWKV_WIDTH), F32)] * 7,
        compiler_params=_cp("arbitrary"), name="rwkv_pre_sample",
    )(*([pc] * 5), *([pc_shifted] * 5), *([mu3] * 5), *_rwkv_params(prm))


def _split_heads(x):
    return jnp.stack([x[:, h * RWKV_HEAD:(h + 1) * RWKV_HEAD] for h in range(RWKV_HEADS)], axis=0)


def _join_heads(x):
    return jnp.concatenate([x[h] for h in range(RWKV_HEADS)], axis=1)


def _wkv_steps_kernel(r_ref, ld_ref, k_ref, v_ref, a_ref, b_ref, s0_ref, y_ref, sf_ref, *, steps, bb):
    n = RWKV_HEAD
    eye = lax.broadcasted_iota(jnp.int32, (n, n), 0) == lax.broadcasted_iota(jnp.int32, (n, n), 1)
    for i in range(bb):
        s = s0_ref[i]
        seqs = [_split_heads(ref[i]) for ref in (r_ref, ld_ref, k_ref, v_ref, a_ref, b_ref)]
        out_rows = []
        for t in range(steps):
            r, ld, k, v, a, b = [x[:, t:t + 1, :] for x in seqs]
            sa = jnp.sum(s * a, axis=-1, keepdims=True)
            vcol = jnp.sum(jnp.where(eye, v, 0.0), axis=-1, keepdims=True)
            s = s * jnp.exp(ld) + sa * b + vcol * k
            ycol = jnp.sum(s * r, axis=-1, keepdims=True)
            out_rows.append(jnp.sum(jnp.where(eye, ycol, 0.0), axis=1, keepdims=True))
        y_ref[i] = _join_heads(jnp.concatenate(out_rows, axis=1))
        sf_ref[i] = s


def _wkv_steps(seqs, s0, *, bb=4):
    batch, t, _ = seqs[0].shape
    seq_spec = pl.BlockSpec((bb, t, RWKV_WIDTH), lambda i: (i, 0, 0))
    st_spec = pl.BlockSpec((bb, RWKV_HEADS, RWKV_HEAD, RWKV_HEAD), lambda i: (i, 0, 0, 0))
    return pl.pallas_call(
        functools.partial(_wkv_steps_kernel, steps=t, bb=bb),
        grid=(batch // bb,),
        in_specs=[seq_spec] * 6 + [st_spec],
        out_specs=[seq_spec, st_spec],
        out_shape=[jax.ShapeDtypeStruct((batch, t, RWKV_WIDTH), F32), jax.ShapeDtypeStruct(s0.shape, F32)],
        compiler_params=_cp("parallel"), name="wkv_steps",
    )(*seqs, s0)


WKV_CHUNK = 64


def _e_nt(a, b):
    return jnp.einsum("hqd,hkd->hqk", a, b, preferred_element_type=F32)


def _e_nn(a, b):
    return jnp.einsum("hqk,hkd->hqd", a, b, preferred_element_type=F32)


def _wkv_chunk_kernel(r_ref, ld_ref, k_ref, v_ref, a_ref, b_ref, y_ref, sf_ref, s_scr):
    c = WKV_CHUNK
    dot_nt = functools.partial(_mxu, _e_nt, precise=True)
    dot_nn = functools.partial(_mxu, _e_nn, precise=True)

    def dot_tn(x, y):
        return dot_nn(jnp.swapaxes(x, 1, 2), y)

    @pl.when(pl.program_id(1) == 0)
    def _():
        s_scr[...] = jnp.zeros_like(s_scr)

    ld = ld_ref[...]
    cum = ld
    k = 1
    while k < c:
        cum = cum + _shift_rows(cum, k)
        k *= 2
    e_pos, e_prev, e_neg = jnp.exp(cum), jnp.exp(cum - ld), jnp.exp(-cum)
    at = _split_heads(a_ref[...] * e_prev)
    rt = _split_heads(r_ref[...] * e_pos)
    bt = _split_heads(b_ref[...] * e_neg)
    kt = _split_heads(k_ref[...] * e_neg)
    v = _split_heads(v_ref[...])
    lam = _split_heads(e_pos[c - 1:c, :])

    ti = lax.broadcasted_iota(jnp.int32, (c, c), 0)
    si = lax.broadcasted_iota(jnp.int32, (c, c), 1)
    strict = ti > si

    def blockmask(size):
        same = (ti // size) == (si // size)
        return strict & same & ((ti // (size // 2)) != (si // (size // 2)))

    ar = jnp.concatenate([at, rt], axis=1)
    g_b = dot_nt(ar, bt)
    g_k = dot_nt(ar, kt)
    n_ab = jnp.where(strict, g_b[:, :c, :], 0.0)
    n_ak = jnp.where(strict, g_k[:, :c, :], 0.0)
    m_rb = jnp.where(ti >= si, g_b[:, c:, :], 0.0)
    m_rk = jnp.where(ti >= si, g_k[:, c:, :], 0.0)

    base = 8
    n8 = jnp.where((ti // base) == (si // base), n_ab, 0.0)
    eye = (ti == si).astype(F32)
    n8_2 = dot_nn(n8, n8)
    n8_4 = dot_nn(n8_2, n8_2)
    t_inv = eye + n8
    t_inv = t_inv + dot_nn(t_inv, n8_2)
    t_inv = t_inv + dot_nn(t_inv, n8_4)
    size = 2 * base
    while size <= c:
        off = jnp.where(blockmask(size), n_ab, 0.0)
        t_inv = t_inv + dot_nn(dot_nn(t_inv, off), t_inv)
        size *= 2

    wv = dot_nn(n_ak, v)
    a_bar = dot_nn(t_inv, at)
    u_bar = dot_nn(t_inv, wv)
    r_bar = rt + dot_nn(m_rb, a_bar)
    y_bar = dot_nn(m_rb, u_bar) + dot_nn(m_rk, v)
    phi = dot_tn(a_bar, bt)
    psi = dot_tn(jnp.concatenate([u_bar, v], axis=1), jnp.concatenate([bt, kt], axis=1))

    s0 = s_scr[...]
    y_ref[...] = _join_heads(dot_nt(r_bar, s0) + y_bar)
    s_new = (s0 + dot_nn(s0, phi) + psi) * lam
    s_scr[...] = s_new

    @pl.when(pl.program_id(1) == pl.num_programs(1) - 1)
    def _():
        sf_ref[...] = s_new


def _wkv_chunked(seqs):
    batch, t, _ = seqs[0].shape
    c = WKV_CHUNK
    seq_spec = pl.BlockSpec((None, c, RWKV_WIDTH), lambda b, i: (b, i, 0))
    st_shape = (batch, RWKV_HEADS, RWKV_HEAD, RWKV_HEAD)
    st_spec = pl.BlockSpec((None,) + st_shape[1:], lambda b, i: (b, 0, 0, 0))
    return pl.pallas_call(
        _wkv_chunk_kernel,
        grid=(batch, t // c),
        in_specs=[seq_spec] * 6,
        out_specs=[seq_spec, st_spec],
        out_shape=[jax.ShapeDtypeStruct((batch, t, RWKV_WIDTH), F32), jax.ShapeDtypeStruct(st_shape, F32)],
        scratch_shapes=[pltpu.VMEM(st_shape[1:], F32)],
        compiler_params=_cp("parallel", "arbitrary"), name="wkv_chunk",
    )(*seqs)


def _rwkv_post_kernel(y_ref, r_ref, k_ref, v_ref, g_ref, rk_ref, lg_ref, lb_ref, o_ref):
    y = y_ref[...]
    inv = 1.0 / RWKV_HEAD
    mean = _head_sum(y) * inv
    yc = y - mean
    var = _head_sum(yc * yc) * inv
    yn = yc * lax.rsqrt(var + RWKV_LN_EPS) * lg_ref[...] + lb_ref[...]
    v = v_ref[...]
    bonus = _head_sum(r_ref[...] * k_ref[...] * rk_ref[...]) * v
    o_ref[...] = ((yn + bonus) * g_ref[...]).astype(o_ref.dtype)


def _rwkv_post(y, r, k, v, g, prm, layer, precise, *, tm):
    rows = y.shape[0]
    depth = prm["rk"].shape[0]
    row = pl.BlockSpec((tm, RWKV_WIDTH), lambda i: (i, 0))
    vec = pl.BlockSpec((None, 1, RWKV_WIDTH), lambda i: (layer, 0, 0))
    r3 = lambda a: a.reshape(depth, 1, RWKV_WIDTH)
    return pl.pallas_call(
        _rwkv_post_kernel,
        grid=(rows // tm,),
        in_specs=[row] * 5 + [vec] * 3,
        out_specs=row,
        out_shape=jax.ShapeDtypeStruct((rows, RWKV_WIDTH), _act_dtype(precise)),
        compiler_params=_cp("parallel"), name="rwkv_post",
    )(y, r, k, v, g, r3(prm["rk"]), r3(prm["ln_g"]), r3(prm["ln_b"]))


MOE_TM = 256


def _dispatch_plan(e_idx, tm):
    n = e_idx.shape[0]
    pairs = n * TOP_K
    e_flat = e_idx.reshape(pairs)
    onehot = (e_flat[:, None] == jnp.arange(N_EXPERTS, dtype=jnp.int32)[None, :]).astype(jnp.int32)
    csum = jnp.cumsum(onehot, axis=0)
    counts = csum[-1]
    padded = ((counts + tm - 1) // tm) * tm
    ends = jnp.cumsum(padded)
    starts = ends - padded
    pos = jnp.sum(onehot * (csum - 1 + starts[None, :]), axis=1)
    nt = (pairs + N_EXPERTS * (tm - 1)) // tm
    tile_start = jnp.arange(nt, dtype=jnp.int32) * tm
    tile_e = jnp.minimum(jnp.sum((tile_start[:, None] >= ends[None, :]).astype(jnp.int32), axis=1), N_EXPERTS - 1)
    tile_nv = jnp.clip(jnp.take(starts + counts, tile_e) - tile_start, 0, tm)
    src = jnp.zeros((nt * tm,), jnp.int32).at[pos].set(jnp.arange(pairs, dtype=jnp.int32) // TOP_K)
    return jnp.stack([tile_e, tile_nv]).astype(jnp.int32), src.reshape(nt, 1, tm), pos


def _moe_gather_kernel(meta_ref, src_ref, h_hbm, o_ref, sem):
    nv = meta_ref[1, pl.program_id(0)]

    def row_copy(r):
        return pltpu.make_async_copy(h_hbm.at[pl.ds(src_ref[0, r], 1)], o_ref.at[pl.ds(r, 1)], sem)

    def issue(r, c):
        row_copy(r).start()
        return c

    def drain(r, c):
        row_copy(r).wait()
        return c

    lax.fori_loop(0, nv, issue, 0)
    lax.fori_loop(0, nv, drain, 0)
    rows = lax.broadcasted_iota(jnp.int32, o_ref.shape, 0)
    o_ref[...] = jnp.where(rows < nv, o_ref[...], 0.0)


def _moe_gather(meta, src, h):
    nt, _, tm = src.shape
    d = h.shape[1]
    gs = pltpu.PrefetchScalarGridSpec(
        num_scalar_prefetch=1, grid=(nt,),
        in_specs=[pl.BlockSpec((None, 1, tm), lambda j, m: (j, 0, 0), memory_space=pltpu.SMEM),
                  pl.BlockSpec(memory_space=pl.ANY)],
        out_specs=pl.BlockSpec((tm, d), lambda j, m: (j, 0)),
        scratch_shapes=[pltpu.SemaphoreType.DMA(())])
    return pl.pallas_call(
        _moe_gather_kernel, grid_spec=gs, out_shape=jax.ShapeDtypeStruct((nt * tm, d), F32),
        compiler_params=_cp("arbitrary"), name="moe_gather",
    )(meta, src, h)


def _moe_experts_kernel(meta_ref, xs_ref, wg_ref, wu_ref, wd_ref, y_ref, wg_bf, wu_bf, wd_bf):
    j = pl.program_id(0)
    e = meta_ref[0, j]
    nv = meta_ref[1, j]
    new_expert = (j == 0) | (e != meta_ref[0, jnp.maximum(j - 1, 0)])

    @pl.when(new_expert)
    def _():
        wg_bf[...] = wg_ref[...].astype(BF16)
        wu_bf[...] = wu_ref[...].astype(BF16)
        wd_bf[...] = wd_ref[...].astype(BF16)

    @pl.when(nv > 0)
    def _():
        x = xs_ref[...].astype(BF16)
        gate = _dot2(x, wg_bf[...])
        act = gate * _sigmoid(gate) * _dot2(x, wu_bf[...])
        y_ref[...] = _dot2(act.astype(BF16), wd_bf[...])

    @pl.when(nv == 0)
    def _():
        y_ref[...] = jnp.zeros_like(y_ref)


def _moe_experts(meta, xs, w_gate, w_up, w_down, layer):
    rows, d = xs.shape
    nt = meta.shape[1]
    tm = rows // nt
    once = dict(pipeline_mode=pl.Buffered(1))
    gs = pltpu.PrefetchScalarGridSpec(
        num_scalar_prefetch=1, grid=(nt,),
        in_specs=[pl.BlockSpec((tm, d), lambda j, m: (j, 0)),
                  pl.BlockSpec((None, None, d, D_EXPERT), lambda j, m: (layer, m[0, j], 0, 0), **once),
                  pl.BlockSpec((None, None, d, D_EXPERT), lambda j, m: (layer, m[0, j], 0, 0), **once),
                  pl.BlockSpec((None, None, D_EXPERT, d), lambda j, m: (layer, m[0, j], 0, 0), **once)],
        out_specs=pl.BlockSpec((tm, d), lambda j, m: (j, 0)),
        scratch_shapes=[pltpu.VMEM((d, D_EXPERT), BF16), pltpu.VMEM((d, D_EXPERT), BF16),
                        pltpu.VMEM((D_EXPERT, d), BF16)])
    return pl.pallas_call(
        _moe_experts_kernel, grid_spec=gs, out_shape=jax.ShapeDtypeStruct((rows, d), F32),
        compiler_params=_cp("arbitrary"), name="moe_experts",
    )(meta, xs, w_gate, w_up, w_down)


def _moe_combine_kernel(pos_ref, w_ref, y_hbm, m_ref, buf, sem):
    tt = m_ref.shape[0]

    def row_copy(r):
        return pltpu.make_async_copy(y_hbm.at[pl.ds(pos_ref[0, r], 1)], buf.at[pl.ds(r, 1)], sem)

    def issue(r, c):
        row_copy(r).start()
        return c

    def drain(r, c):
        row_copy(r).wait()
        return c

    lax.fori_loop(0, TOP_K * tt, issue, 0)
    lax.fori_loop(0, TOP_K * tt, drain, 0)
    w = w_ref[...]
    m_ref[...] = w[:, 0:1] * buf[0:tt, :] + w[:, 1:2] * buf[tt:2 * tt, :]


def _moe_combine(pos, wts, y, *, tt):
    n = wts.shape[0]
    d = y.shape[1]
    pos_t = jnp.swapaxes(pos.reshape(n // tt, tt, TOP_K), 1, 2).reshape(n // tt, 1, TOP_K * tt)
    return pl.pallas_call(
        _moe_combine_kernel,
        grid=(n // tt,),
        in_specs=[pl.BlockSpec((None, 1, TOP_K * tt), lambda i: (i, 0, 0), memory_space=pltpu.SMEM),
                  pl.BlockSpec((tt, TOP_K), lambda i: (i, 0)),
                  pl.BlockSpec(memory_space=pl.ANY)],
        out_specs=pl.BlockSpec((tt, d), lambda i: (i, 0)),
        out_shape=jax.ShapeDtypeStruct((n, d), F32),
        scratch_shapes=[pltpu.VMEM((TOP_K * tt, d), F32), pltpu.SemaphoreType.DMA(())],
        compiler_params=_cp("arbitrary"), name="moe_combine",
    )(pos_t, wts, y)


def _moe(h_all, e_idx, wts, w_gate, w_up, w_down, layer):
    n = h_all.shape[0]
    meta, src, pos = _dispatch_plan(e_idx, MOE_TM)
    xs = _moe_gather(meta, src, h_all)
    y = _moe_experts(meta, xs, w_gate, w_up, w_down, layer)
    tt = 256
    while n % tt:
        tt //= 2
    return _moe_combine(pos, wts, y, tt=tt)


_SH1, _SC1, _GT1, _SH2, _SC2, _GT2 = range(6)


def kernel(x_prompt, x_sample, cache_swa_k, cache_swa_v, state_pool, state_rwkv_shift, state_rwkv_wkv, state_conv, c_prompt, c_sample, w_ada, b_ada, g_norm1, g_norm2, w_in, sinks, w_pool, ls_pool, rwkv_mu, rwkv_w0, rwkv_w2, rwkv_a0, rwkv_a2, rwkv_g2, rwkv_kk, rwkv_ka, rwkv_rk, rwkv_ln_g, rwkv_ln_b, conv_w, w_branch, w_out, w_router, b_router, w_gate, w_up, w_down, g_final):
    depth = w_in.shape[0]
    batch, seq, d = x_prompt.shape
    dbatch, dseq, _ = x_sample.shape
    wbuf = cache_swa_k.shape[2]
    np_rows, ns_rows = batch * seq, dbatch * dseq

    grp_p = _Group(np_rows, seq, 1)
    grp_s = _Group(ns_rows, ns_rows, ns_rows)
    tm_p, tm_s = min(2048, seq), ns_rows
    te_p = min(512, seq)

    c_all = jnp.concatenate([c_prompt, c_sample], axis=0)
    wr_pad = jnp.pad(w_router, ((0, 0), (0, ROUTER_PAD - N_EXPERTS)))
    prm = dict(mu=rwkv_mu, w0=rwkv_w0, w2=rwkv_w2, a0=rwkv_a0, a2=rwkv_a2, g2=rwkv_g2, kk=rwkv_kk, ka=rwkv_ka,
               rk=rwkv_rk.reshape(depth, RWKV_WIDTH), ln_g=rwkv_ln_g, ln_b=rwkv_ln_b)
    cache_k = cache_swa_k.reshape(depth, dbatch, wbuf, KV_WIDTH)
    cache_v = cache_swa_v.reshape(depth, dbatch, wbuf, KV_WIDTH)

    xp = x_prompt.reshape(np_rows, d)
    xs = x_sample.reshape(ns_rows, d)
    st_p, st_s = [], []
    pend_p = pend_s = None
    for l in range(depth):
        precise = l == 0
        hd = _act_dtype(precise)
        mod = _ada(c_all, w_ada, b_ada, l, precise)
        mod_p = mod[:batch].reshape(batch, 1, 6 * d)
        mod_s = jnp.repeat(mod[batch:], dseq, axis=0).reshape(1, ns_rows, 6 * d)

        def first_norm(grp, x, pend, mod3, tm):
            if pend is None:
                return x, _normx(grp, x, g_norm1[l], mod=(mod3, _SC1, _SH1), out_dtype=hd, tm=tm)[0]
            x, h = _normx(grp, x, g_norm1[l], add=pend + (_GT2,), mod=(mod3, _SC1, _SH1), emit_x=True,
                          out_dtype=hd, tm=tm)
            return x, h

        xp, hp = first_norm(grp_p, xp, pend_p, mod_p, te_p)
        xs, hs = first_norm(grp_s, xs, pend_s, mod_s, tm_s)

        pp = _mm(hp, w_in, l, tm=min(tm_p, 1024) if precise else tm_p, tn=512, precise=precise)
        ps = _mm(hs, w_in, l, tm=tm_s, tn=512, precise=precise)

        a_p = _swa_prompt(pp, sinks, l, batch, seq, precise)
        b_p = _pool_prompt(pp, w_pool, ls_pool, l, batch, seq, precise)
        d_p, conv_new_p = _conv_prompt(pp, conv_w, l, batch, seq, precise)
        pre_p = _rwkv_pre_prompt(pp, prm, l, batch, seq, precise, tt=te_p)
        r_p, w_p, k_p, v_p, ka_p, kb_p, g_p = pre_p
        as3 = lambda t: t.reshape(batch, seq, RWKV_WIDTH)
        y_p, wkv_new_p = _wkv_chunked([as3(t) for t in (r_p, w_p, k_p, v_p, ka_p, kb_p)])
        c_p = _rwkv_post(y_p.reshape(np_rows, RWKV_WIDTH), r_p, k_p, v_p, g_p, prm, l, precise, tm=te_p)
        pp3 = pp.reshape(batch, seq, IN_WIDTH)
        kw = min(WINDOW, seq)
        st_p.append((pp3[:, seq - kw:, OFF_K:OFF_K + KV_WIDTH].reshape(batch, kw, N_KV_HEADS, HEAD_DIM),
                     pp3[:, seq - kw:, OFF_V:OFF_V + KV_WIDTH].reshape(batch, kw, N_KV_HEADS, HEAD_DIM),
                     pp3[:, seq - POOL_BUF:, OFF_U:OFF_U + POOL_WIDTH],
                     pp3[:, seq - 1, OFF_PC:OFF_PC + RWKV_PROJ],
                     wkv_new_p, conv_new_p))

        ps3 = ps.reshape(dbatch, dseq, IN_WIDTH)
        a_s, k_new_s, v_new_s = _swa_sample(ps3, cache_k, cache_v, sinks, l, precise)
        tmaj = lambda t: jnp.swapaxes(t, 0, 1)
        b_s_t, pool_new_t = _pool_sample(tmaj(ps3[:, :, OFF_U:OFF_U + POOL_WIDTH]), tmaj(state_pool[l]),
                                         w_pool, ls_pool, l, PAST_LEN, precise)
        d_s_t, conv_new_t = _conv_sample(tmaj(ps3[:, :, OFF_CB:OFF_CB + CONV_WIDTH]),
                                         tmaj(ps3[:, :, OFF_CC:OFF_CC + CONV_WIDTH]),
                                         tmaj(ps3[:, :, OFF_CX:OFF_CX + CONV_WIDTH]), tmaj(state_conv[l]), conv_w, l,
                                         precise)
        pc_s3 = ps3[:, :, OFF_PC:OFF_PC + RWKV_PROJ]
        pc_shift = jnp.concatenate([state_rwkv_shift[l][:, None, :], pc_s3[:, :-1, :]], axis=1)
        pre_s = _rwkv_pre_sample(pc_s3.reshape(ns_rows, RWKV_PROJ), pc_shift.reshape(ns_rows, RWKV_PROJ), prm, l,
                                 precise)
        r_s, w_s, k_s, v_s, ka_s, kb_s, g_s = pre_s
        as3s = lambda t: t.reshape(dbatch, dseq, RWKV_WIDTH)
        y_s, wkv_new_s = _wkv_steps([as3s(t) for t in (r_s, w_s, k_s, v_s, ka_s, kb_s)], state_rwkv_wkv[l])
        c_s = _rwkv_post(y_s.reshape(ns_rows, RWKV_WIDTH), r_s, k_s, v_s, g_s, prm, l, precise, tm=ns_rows)
        st_s.append((k_new_s.reshape(dbatch, wbuf, N_KV_HEADS, HEAD_DIM),
                     v_new_s.reshape(dbatch, wbuf, N_KV_HEADS, HEAD_DIM),
                     tmaj(pool_new_t), pc_s3[:, dseq - 1, :], wkv_new_s, tmaj(conv_new_t)))
        br_s = (a_s.reshape(ns_rows, ATTN_WIDTH), tmaj(b_s_t).reshape(ns_rows, POOL_WIDTH), c_s,
                tmaj(d_s_t).reshape(ns_rows, CONV_WIDTH))

        def tail(grp, x, branches, p, mod3, tm_mm, tm_el):
            if precise:
                merged = _merge(branches, p, w_branch, l, tm=min(tm_mm, 512), tn=256, precise=True)
                x1 = _mm_res(grp, merged, w_out, l, x, mod3, _GT1, tm=min(tm_mm, 512), tn=512, precise=True)
            else:
                merged = _merge(branches, p, w_branch, l, tm=min(tm_mm, 1024), tn=512, precise=False)
                x1 = _mm_res(grp, merged, w_out, l, x, mod3, _GT1, tm=tm_mm, tn=512, precise=False)
            return (x1,) + tuple(_normx(grp, x1, g_norm2[l], mod=(mod3, _SC2, _SH2), route=(wr_pad, b_router),
                                        out_dtype=F32, tm=tm_el))

        xp, h2_p, e_p, wt_p = tail(grp_p, xp, (a_p, b_p, c_p, d_p), pp, mod_p, tm_p, te_p)
        xs, h2_s, e_s, wt_s = tail(grp_s, xs, br_s, ps, mod_s, tm_s, tm_s)
        m_all = _moe(jnp.concatenate([h2_p, h2_s]), jnp.concatenate([e_p, e_s]), jnp.concatenate([wt_p, wt_s]),
                     w_gate, w_up, w_down, l)
        pend_p, pend_s = (m_all, 0, mod_p), (m_all, np_rows, mod_s)

    y_p = _normx(grp_p, xp, g_final, add=pend_p + (_GT2,), out_dtype=F32, tm=te_p)[0]
    y_s = _normx(grp_s, xs, g_final, add=pend_s + (_GT2,), out_dtype=F32, tm=tm_s)[0]

    def stack(states, i):
        return jnp.stack([s[i] for s in states])

    return ((y_p.reshape(batch, seq, d), y_s.reshape(dbatch, dseq, d))
            + tuple(stack(st_p, i) for i in range(6)) + tuple(stack(st_s, i) for i in range(6)))
```

```python
import functools

import jax
import jax.numpy as jnp
from jax import lax
from jax.experimental import pallas as pl
from jax.experimental.pallas import tpu as pltpu

F32 = jnp.float32
BF16 = jnp.bfloat16

D_MODEL = 2048
PAST_LEN = 8192
WINDOW = 128
HEAD_DIM = 64
N_HEADS = 16
N_KV_HEADS = 4
GQA_GROUP = N_HEADS // N_KV_HEADS
ATTN_WIDTH = N_HEADS * HEAD_DIM
KV_WIDTH = N_KV_HEADS * HEAD_DIM
ATTN_SCALE = HEAD_DIM ** -0.5
NEG_INF = -1e30
POOL_WINDOWS = (2, 4, 8, 16)
POOL_WIDTH = 768
POOL_GW = POOL_WIDTH // len(POOL_WINDOWS)
POOL_BUF = max(POOL_WINDOWS) - 1
RWKV_HEAD = 64
RWKV_WIDTH = 768
RWKV_HEADS = RWKV_WIDTH // RWKV_HEAD
DECAY_LORA = 64
AAA_LORA = 64
GATE_LORA = 128
RWKV_PROJ = 3 * RWKV_WIDTH + DECAY_LORA + AAA_LORA + GATE_LORA
RWKV_LN_EPS = 64e-5
CONV_WIDTH = 768
CONV_K = 3
N_BRANCH = 4
BRANCH_SECTIONS = (ATTN_WIDTH, POOL_WIDTH, RWKV_WIDTH, CONV_WIDTH)
MIX_WIDTH = sum(BRANCH_SECTIONS)
N_EXPERTS = 16
N_GROUPS = 4
EXP_PER_GROUP = N_EXPERTS // N_GROUPS
TOP_K = 2
D_EXPERT = 1024
RMS_EPS = 1e-6

OFF_Q = 0
OFF_K = OFF_Q + ATTN_WIDTH
OFF_V = OFF_K + KV_WIDTH
OFF_U = OFF_V + KV_WIDTH
OFF_PC = OFF_U + POOL_WIDTH
OFF_CB = OFF_PC + RWKV_PROJ
OFF_CC = OFF_CB + CONV_WIDTH
OFF_CX = OFF_CC + CONV_WIDTH
OFF_GL = OFF_CX + CONV_WIDTH
IN_WIDTH = OFF_GL + N_BRANCH * D_MODEL

PRECISE_TAIL = 2 * WINDOW
LANES = 128
ROUTER_PAD = LANES
VMEM_LIMIT = 60 * 1024 * 1024


def _cp(*sem):
    return pltpu.CompilerParams(dimension_semantics=sem, vmem_limit_bytes=VMEM_LIMIT)


def _sigmoid(x):
    return 1.0 / (1.0 + jnp.exp(-x))


def _split(x):
    hi = x.astype(BF16)
    return hi, (x.astype(F32) - hi.astype(F32)).astype(BF16)


def _mxu(contract, a, b, precise):
    if not precise:
        return contract(a.astype(BF16), b.astype(BF16))
    ah, al = _split(a)
    bh, bl = _split(b)
    return contract(ah, bh) + (contract(ah, bl) + contract(al, bh))


def _dot2(a, b):
    return jnp.dot(a, b, preferred_element_type=F32)


def _bdot(a, b, precise=False):
    return _mxu(_dot2, a, b, precise)


def _act_dtype(precise):
    return F32 if precise else BF16


def _ada_kernel(c_ref, w_ref, b_ref, o_ref, *, precise):
    c = c_ref[...]
    o_ref[...] = _bdot(c * _sigmoid(c), w_ref[...], precise) + b_ref[...]


def _ada(c_all, w_ada, b_ada, layer, precise):
    depth, d, n = w_ada.shape
    nb = c_all.shape[0]
    tn = 1024
    return pl.pallas_call(
        functools.partial(_ada_kernel, precise=precise),
        grid=(n // tn,),
        in_specs=[pl.BlockSpec((nb, d), lambda j: (0, 0)),
                  pl.BlockSpec((None, d, tn), lambda j: (layer, 0, j)),
                  pl.BlockSpec((None, 1, tn), lambda j: (layer, 0, j))],
        out_specs=pl.BlockSpec((nb, tn), lambda j: (0, j)),
        out_shape=jax.ShapeDtypeStruct((nb, n), F32),
        compiler_params=_cp("parallel"),
        name="ada",
    )(c_all, w_ada, b_ada.reshape(depth, 1, n))


class _Group:
    def __init__(self, rows, rpm, mod_rows):
        self.rows, self.rpm, self.mod_rows = rows, rpm, mod_rows

    def mod_spec(self, tm, width, col_of, row_axis=0):
        if self.mod_rows == 1:
            per = self.rpm // tm
            return pl.BlockSpec((None, 1, width), lambda *idx: (idx[row_axis] // per, 0, col_of(idx)))
        assert tm == self.rpm == self.mod_rows
        return pl.BlockSpec((None, tm, width), lambda *idx: (idx[row_axis], 0, col_of(idx)))


def _route(y, wr, br):
    y_hi = y.astype(BF16)
    y_lo = (y - y_hi.astype(F32)).astype(BF16)
    w_hi = wr.astype(BF16)
    w_lo = (wr - w_hi.astype(F32)).astype(BF16)
    logits = (jnp.dot(y_hi, w_hi, preferred_element_type=F32)
              + (jnp.dot(y_hi, w_lo, preferred_element_type=F32) + jnp.dot(y_lo, w_hi, preferred_element_type=F32)))
    logits = logits[:, :N_EXPERTS] + br
    tm = logits.shape[0]
    e = jnp.exp(logits - jnp.max(logits, axis=-1, keepdims=True))
    probs = e / jnp.sum(e, axis=-1, keepdims=True)
    iota_g = lax.broadcasted_iota(jnp.int32, (tm, EXP_PER_GROUP), 1)
    best = None
    for g in range(N_GROUPS):
        pg = probs[:, g * EXP_PER_GROUP:(g + 1) * EXP_PER_GROUP]
        m1 = jnp.max(pg, axis=-1, keepdims=True)
        i1 = jnp.min(jnp.where(pg == m1, iota_g, EXP_PER_GROUP), axis=-1, keepdims=True)
        rest = jnp.where(iota_g == i1, -1.0, pg)
        m2 = jnp.max(rest, axis=-1, keepdims=True)
        i2 = jnp.min(jnp.where(rest == m2, iota_g, EXP_PER_GROUP), axis=-1, keepdims=True)
        cand = (m1 + m2, m1, m2, i1 + g * EXP_PER_GROUP, i2 + g * EXP_PER_GROUP)
        if best is None:
            best = cand
        else:
            take = cand[0] > best[0]
            best = tuple(jnp.where(take, c, b) for c, b in zip(cand, best))
    _, m1, m2, e1, e2 = best
    den = m1 + m2
    slot = lax.broadcasted_iota(jnp.int32, (tm, TOP_K), 1)
    return jnp.where(slot == 0, e1, e2), jnp.where(slot == 0, m1 / den, m2 / den)


def _normx_kernel(*refs, has_add, has_mod, has_route, emit_x):
    it = iter(refs)
    x_ref = next(it)
    if has_add:
        m_ref, gate_ref = next(it), next(it)
    g_ref = next(it)
    if has_mod:
        sc_ref, sh_ref = next(it), next(it)
    if has_route:
        wr_ref, br_ref = next(it), next(it)
    if emit_x:
        xo_ref = next(it)
    h_ref = next(it)
    if has_route:
        eidx_ref, wts_ref = next(it), next(it)
    x = x_ref[...]
    if has_add:
        x = x + gate_ref[...] * m_ref[...]
    if emit_x:
        xo_ref[...] = x
    y = x * lax.rsqrt(jnp.mean(x * x, axis=-1, keepdims=True) + RMS_EPS) * g_ref[...]
    if has_mod:
        y = y * (1.0 + sc_ref[...]) + sh_ref[...]
    h_ref[...] = y.astype(h_ref.dtype)
    if has_route:
        eidx_ref[...], wts_ref[...] = _route(y, wr_ref[...], br_ref[...])


def _normx(grp, x, g, *, add=None, mod=None, route=None, emit_x=False, out_dtype=BF16, tm=512):
    n, d = x.shape
    row = pl.BlockSpec((tm, d), lambda i: (i, 0))
    vec = pl.BlockSpec((1, d), lambda i: (0, 0))
    args, specs = [x], [row]
    if add is not None:
        m, m_row0, mod3, chunk = add
        args += [m, mod3]
        specs += [pl.BlockSpec((tm, d), lambda i, o=m_row0 // tm: (i + o, 0)),
                  grp.mod_spec(tm, d, lambda idx, c=chunk: c)]
    args.append(g.reshape(1, d))
    specs.append(vec)
    if mod is not None:
        mod3, c_sc, c_sh = mod
        args += [mod3, mod3]
        specs += [grp.mod_spec(tm, d, lambda idx, c=c_sc: c), grp.mod_spec(tm, d, lambda idx, c=c_sh: c)]
    if route is not None:
        wr, br = route
        args += [wr, br.reshape(1, N_EXPERTS)]
        specs += [pl.BlockSpec((d, ROUTER_PAD), lambda i: (0, 0)), pl.BlockSpec((1, N_EXPERTS), lambda i: (0, 0))]
    out_shape, out_specs = [], []
    if emit_x:
        out_shape.append(jax.ShapeDtypeStruct((n, d), F32))
        out_specs.append(row)
    out_shape.append(jax.ShapeDtypeStruct((n, d), out_dtype))
    out_specs.append(row)
    if route is not None:
        out_shape += [jax.ShapeDtypeStruct((n, TOP_K), jnp.int32), jax.ShapeDtypeStruct((n, TOP_K), F32)]
        out_specs += [pl.BlockSpec((tm, TOP_K), lambda i: (i, 0))] * 2
    return pl.pallas_call(
        functools.partial(_normx_kernel, has_add=add is not None, has_mod=mod is not None,
                          has_route=route is not None, emit_x=emit_x),
        grid=(n // tm,), in_specs=specs, out_specs=out_specs, out_shape=out_shape,
        compiler_params=_cp("parallel"), name="normx",
    )(*args)


def _tail_tile(tail):
    return lax.rem(pl.program_id(0) + 1, tail[1]) == 0


def _tail_fix(a, w, tail):
    a_hi, a_lo = _split(a[a.shape[0] - tail[0]:, :])
    w_hi, w_lo = _split(w)
    return _dot2(a_hi, w_lo) + _dot2(a_lo, w_hi)


def _tail_rows(x, rows):
    return x if x.shape[0] == 1 else x[x.shape[0] - rows:, :]


def _mm_kernel(a_ref, w_ref, o_ref, *, tail):
    o_ref[...] = _bdot(a_ref[...], w_ref[...])
    if tail is not None:
        @pl.when(_tail_tile(tail))
        def _():
            lo = o_ref.shape[0] - tail[0]
            o_ref[lo:, :] += _tail_fix(a_ref[...], w_ref[...], tail)


def _mm(a, w3, layer, *, tm, tn, tail):
    m, k = a.shape
    n = w3.shape[-1]
    return pl.pallas_call(
        functools.partial(_mm_kernel, tail=tail),
        grid=(m // tm, n // tn),
        in_specs=[pl.BlockSpec((tm, k), lambda i, j: (i, 0)),
                  pl.BlockSpec((None, k, tn), lambda i, j: (layer, 0, j))],
        out_specs=pl.BlockSpec((tm, tn), lambda i, j: (i, j)),
        out_shape=jax.ShapeDtypeStruct((m, n), F32),
        compiler_params=_cp("parallel", "parallel"), name="mm_in",
    )(a, w3)


def _mm_res_kernel(a_ref, w_ref, x_ref, gate_ref, o_ref, *, tail):
    o_ref[...] = x_ref[...] + gate_ref[...] * _bdot(a_ref[...], w_ref[...])
    if tail is not None:
        @pl.when(_tail_tile(tail))
        def _():
            lo = o_ref.shape[0] - tail[0]
            o_ref[lo:, :] += _tail_rows(gate_ref[...], tail[0]) * _tail_fix(a_ref[...], w_ref[...], tail)


def _mm_res(grp, a, w3, layer, x, mod3, gate_chunk, *, tm, tn, tail):
    m, k = a.shape
    n = w3.shape[-1]
    per_chunk = n // tn
    return pl.pallas_call(
        functools.partial(_mm_res_kernel, tail=tail),
        grid=(m // tm, n // tn),
        in_specs=[pl.BlockSpec((tm, k), lambda i, j: (i, 0)),
                  pl.BlockSpec((None, k, tn), lambda i, j: (layer, 0, j)),
                  pl.BlockSpec((tm, tn), lambda i, j: (i, j)),
                  grp.mod_spec(tm, tn, lambda idx: gate_chunk * per_chunk + idx[1])],
        out_specs=pl.BlockSpec((tm, tn), lambda i, j: (i, j)),
        out_shape=jax.ShapeDtypeStruct((m, n), F32),
        compiler_params=_cp("parallel", "parallel"), name="mm_out",
    )(a, w3, x, mod3)


def _merge_kernel(a_ref, b_ref, c_ref, d_ref, g0_ref, g1_ref, g2_ref, g3_ref, w_ref, o_ref, *, tail):
    parts = []
    lo = 0
    for br_ref, g_ref, width in zip((a_ref, b_ref, c_ref, d_ref), (g0_ref, g1_ref, g2_ref, g3_ref), BRANCH_SECTIONS):
        parts.append((br_ref, g_ref, lo, width))
        lo += width
    acc = None
    for br_ref, g_ref, lo, width in parts:
        t = _sigmoid(g_ref[...]) * _bdot(br_ref[...], w_ref[lo:lo + width, :])
        acc = t if acc is None else acc + t
    o_ref[...] = acc.astype(o_ref.dtype)
    if tail is not None:
        @pl.when(_tail_tile(tail))
        def _():
            row0 = o_ref.shape[0] - tail[0]
            fix = None
            for br_ref, g_ref, lo, width in parts:
                t = _sigmoid(g_ref[row0:, :]) * _tail_fix(br_ref[...], w_ref[lo:lo + width, :], tail)
                fix = t if fix is None else fix + t
            o_ref[row0:, :] += fix


def _merge(branches, p, w_branch, layer, *, tm, tn, tail):
    m = p.shape[0]
    gl_blk = OFF_GL // tn
    per = D_MODEL // tn
    br_specs = [pl.BlockSpec((tm, w), lambda i, j: (i, 0)) for w in BRANCH_SECTIONS]
    gl_specs = [pl.BlockSpec((tm, tn), lambda i, j, b=b: (i, gl_blk + b * per + j)) for b in range(N_BRANCH)]
    return pl.pallas_call(
        functools.partial(_merge_kernel, tail=tail),
        grid=(m // tm, D_MODEL // tn),
        in_specs=br_specs + gl_specs + [pl.BlockSpec((None, MIX_WIDTH, tn), lambda i, j: (layer, 0, j))],
        out_specs=pl.BlockSpec((tm, tn), lambda i, j: (i, j)),
        out_shape=jax.ShapeDtypeStruct((m, D_MODEL), BF16 if tail is None else F32),
        compiler_params=_cp("parallel", "parallel"), name="merge",
    )(*branches, p, p, p, p, w_branch)


def _sink_col(sink_ref, layer, kh, rows_per_head):
    return jnp.concatenate([jnp.full((rows_per_head, 1), sink_ref[layer, kh * GQA_GROUP + g], F32)
                            for g in range(GQA_GROUP)], axis=0)


def _dot_nt(a, b):
    return lax.dot_general(a, b, (((1,), (1,)), ((), ())), preferred_element_type=F32)


def _swa_prompt_kernel(sink_ref, q_ref, kc_ref, kp_ref, vc_ref, vp_ref, o_ref, *, layer, precise):
    n = pl.program_id(1)
    w = WINDOW
    q = q_ref[...]
    qi = jnp.bitwise_and(lax.broadcasted_iota(jnp.int32, (GQA_GROUP * w, 2 * w), 0), w - 1)
    sj = lax.broadcasted_iota(jnp.int32, (GQA_GROUP * w, 2 * w), 1)
    valid = (sj >= qi) & (sj <= qi + w) & ((sj >= w) | (n > 0))
    outs = []
    for kh in range(N_KV_HEADS):
        sl = slice(kh * HEAD_DIM, (kh + 1) * HEAD_DIM)
        k2 = jnp.concatenate([kp_ref[:, sl], kc_ref[:, sl]], axis=0)
        v2 = jnp.concatenate([vp_ref[:, sl], vc_ref[:, sl]], axis=0)
        q4 = jnp.concatenate([q[:, (kh * GQA_GROUP + g) * HEAD_DIM:(kh * GQA_GROUP + g + 1) * HEAD_DIM]
                              for g in range(GQA_GROUP)], axis=0)
        s = _mxu(_dot_nt, q4, k2, precise) * ATTN_SCALE
        s = jnp.where(valid, s, NEG_INF)
        sk = _sink_col(sink_ref, layer, kh, w)
        m = jnp.maximum(jnp.max(s, axis=-1, keepdims=True), sk)
        e = jnp.exp(s - m)
        p = e / (jnp.sum(e, axis=-1, keepdims=True) + jnp.exp(sk - m))
        o4 = _bdot(p, v2, precise)
        outs += [o4[g * w:(g + 1) * w] for g in range(GQA_GROUP)]
    o_ref[...] = jnp.concatenate(outs, axis=1).astype(o_ref.dtype)


def _swa_prompt(p, sinks, layer, batch, seq, precise):
    nb = seq // WINDOW
    kblk, vblk = OFF_K // KV_WIDTH, OFF_V // KV_WIDTH

    def cur(col):
        return lambda b, n: (b * nb + n, col)

    def prev(col):
        return lambda b, n: (b * nb + jnp.maximum(n - 1, 0), col)

    return pl.pallas_call(
        functools.partial(_swa_prompt_kernel, layer=layer, precise=precise),
        grid=(batch, nb),
        in_specs=[pl.BlockSpec(memory_space=pltpu.SMEM),
                  pl.BlockSpec((WINDOW, ATTN_WIDTH), cur(0)),
                  pl.BlockSpec((WINDOW, KV_WIDTH), cur(kblk)), pl.BlockSpec((WINDOW, KV_WIDTH), prev(kblk)),
                  pl.BlockSpec((WINDOW, KV_WIDTH), cur(vblk)), pl.BlockSpec((WINDOW, KV_WIDTH), prev(vblk))],
        out_specs=pl.BlockSpec((WINDOW, ATTN_WIDTH), lambda b, n: (b * nb + n, 0)),
        out_shape=jax.ShapeDtypeStruct((batch * seq, ATTN_WIDTH), _act_dtype(precise)),
        compiler_params=_cp("parallel", "parallel"), name="swa_prompt",
    )(sinks, p, p, p, p, p)


def _qk(a, b):
    return jnp.einsum("bqd,bkd->bqk", a, b, preferred_element_type=F32)


def _pv(a, b):
    return jnp.einsum("bqk,bkd->bqd", a, b, preferred_element_type=F32)


def _swa_sample_kernel(sink_ref, q_ref, kn_ref, vn_ref, kc_ref, vc_ref, o_ref, ko_ref, vo_ref, *, layer, steps, wbuf,
                       precise):
    q = q_ref[...]
    kn, vn = kn_ref[...], vn_ref[...]
    kc, vc = kc_ref[...], vc_ref[...]
    ko_ref[:, :wbuf - steps, :] = kc[:, steps:, :]
    ko_ref[:, wbuf - steps:, :] = kn
    vo_ref[:, :wbuf - steps, :] = vc[:, steps:, :]
    vo_ref[:, wbuf - steps:, :] = vn
    bb = q.shape[0]
    rows = GQA_GROUP * steps
    t_c = lax.rem(lax.broadcasted_iota(jnp.int32, (bb, rows, wbuf), 1), steps)
    j_c = lax.broadcasted_iota(jnp.int32, (bb, rows, wbuf), 2)
    dist_c = t_c + wbuf - j_c
    valid_c = (dist_c >= 0) & (dist_c <= WINDOW)
    t_n = lax.rem(lax.broadcasted_iota(jnp.int32, (bb, rows, steps), 1), steps)
    j_n = lax.broadcasted_iota(jnp.int32, (bb, rows, steps), 2)
    valid_n = (t_n - j_n >= 0) & (t_n - j_n <= WINDOW)
    outs = [None] * N_HEADS
    for kh in range(N_KV_HEADS):
        sl = slice(kh * HEAD_DIM, (kh + 1) * HEAD_DIM)
        qg = jnp.concatenate([q[:, :, (kh * GQA_GROUP + g) * HEAD_DIM:(kh * GQA_GROUP + g + 1) * HEAD_DIM]
                              for g in range(GQA_GROUP)], axis=1)
        s_c = _mxu(_qk, qg, kc[:, :, sl], precise) * ATTN_SCALE
        s_n = _mxu(_qk, qg, kn[:, :, sl], precise) * ATTN_SCALE
        s_c = jnp.where(valid_c, s_c, NEG_INF)
        s_n = jnp.where(valid_n, s_n, NEG_INF)
        sk = _sink_col(sink_ref, layer, kh, steps)[None]
        m = jnp.maximum(jnp.maximum(jnp.max(s_c, axis=-1, keepdims=True), jnp.max(s_n, axis=-1, keepdims=True)), sk)
        e_c, e_n = jnp.exp(s_c - m), jnp.exp(s_n - m)
        den = jnp.sum(e_c, axis=-1, keepdims=True) + jnp.sum(e_n, axis=-1, keepdims=True) + jnp.exp(sk - m)
        o = _mxu(_pv, e_c / den, vc[:, :, sl], precise) + _mxu(_pv, e_n / den, vn[:, :, sl], precise)
        for g in range(GQA_GROUP):
            outs[kh * GQA_GROUP + g] = o[:, g * steps:(g + 1) * steps, :]
    o_ref[...] = jnp.concatenate(outs, axis=2).astype(o_ref.dtype)


def _swa_sample(p3, cache_k, cache_v, sinks, layer, precise, *, bb=8):
    batch, steps, _ = p3.shape
    wbuf = cache_k.shape[2]
    kblk, vblk = OFF_K // KV_WIDTH, OFF_V // KV_WIDTH
    cache_spec = pl.BlockSpec((None, bb, wbuf, KV_WIDTH), lambda i: (layer, i, 0, 0))
    new_spec = pl.BlockSpec((bb, wbuf, KV_WIDTH), lambda i: (i, 0, 0))
    return pl.pallas_call(
        functools.partial(_swa_sample_kernel, layer=layer, steps=steps, wbuf=wbuf, precise=precise),
        grid=(batch // bb,),
        in_specs=[pl.BlockSpec(memory_space=pltpu.SMEM),
                  pl.BlockSpec((bb, steps, ATTN_WIDTH), lambda i: (i, 0, 0)),
                  pl.BlockSpec((bb, steps, KV_WIDTH), lambda i: (i, 0, kblk)),
                  pl.BlockSpec((bb, steps, KV_WIDTH), lambda i: (i, 0, vblk)),
                  cache_spec, cache_spec],
        out_specs=[pl.BlockSpec((bb, steps, ATTN_WIDTH), lambda i: (i, 0, 0)), new_spec, new_spec],
        out_shape=[jax.ShapeDtypeStruct((batch, steps, ATTN_WIDTH), _act_dtype(precise)),
                   jax.ShapeDtypeStruct((batch, wbuf, KV_WIDTH), F32),
                   jax.ShapeDtypeStruct((batch, wbuf, KV_WIDTH), F32)],
        compiler_params=_cp("parallel"), name="swa_sample",
    )(sinks, p3, p3, p3, cache_k, cache_v)


def _shift_rows(x, k):
    rows = lax.broadcasted_iota(jnp.int32, x.shape, 0)
    return jnp.where(rows >= k, pltpu.roll(x, k, axis=0), 0.0)


def _pool_prompt_kernel(u_ref, w_ref, ls_ref, o_ref, *, precise):
    u = u_ref[...]
    t = u.shape[0]
    pos1 = (lax.broadcasted_iota(jnp.int32, (t, 1), 0) + 1).astype(F32)
    sums = {1: u}
    win = 1
    while win < max(POOL_WINDOWS):
        sums[2 * win] = sums[win] + _shift_rows(sums[win], win)
        win *= 2
    outs = []
    for gi, win in enumerate(POOL_WINDOWS):
        sl = slice(gi * POOL_GW, (gi + 1) * POOL_GW)
        cnt = jnp.minimum(float(win), pos1)
        d = sums[win][:, sl] / cnt - u[:, sl]
        outs.append(_bdot(d, w_ref[gi], precise))
    o_ref[...] = (jnp.concatenate(outs, axis=1) * ls_ref[...]).astype(o_ref.dtype)


def _pool_prompt(p, w_pool, ls_pool, layer, batch, seq, precise):
    gw = POOL_GW
    return pl.pallas_call(
        functools.partial(_pool_prompt_kernel, precise=precise),
        grid=(batch,),
        in_specs=[pl.BlockSpec((seq, POOL_WIDTH), lambda b: (b, OFF_U // POOL_WIDTH)),
                  pl.BlockSpec((None, len(POOL_WINDOWS), gw, gw), lambda b: (layer, 0, 0, 0)),
                  pl.BlockSpec((None, 1, POOL_WIDTH), lambda b: (layer, 0, 0))],
        out_specs=pl.BlockSpec((seq, POOL_WIDTH), lambda b: (b, 0)),
        out_shape=jax.ShapeDtypeStruct((batch * seq, POOL_WIDTH), _act_dtype(precise)),
        compiler_params=_cp("parallel"), name="pool_prompt",
    )(p, w_pool, ls_pool.reshape(ls_pool.shape[0], 1, POOL_WIDTH))


def _pool_sample_kernel(u_ref, past_ref, w_ref, ls_ref, o_ref, new_ref, *, pos0, precise):
    steps, hist = u_ref.shape[0], past_ref.shape[0]
    full = [past_ref[i] for i in range(hist)] + [u_ref[i] for i in range(steps)]
    for i in range(hist):
        new_ref[i] = full[steps + i]
    ds = [[] for _ in POOL_WINDOWS]
    for t in range(steps):
        for gi, win in enumerate(POOL_WINDOWS):
            sl = slice(gi * POOL_GW, (gi + 1) * POOL_GW)
            wsum = full[hist + t][:, sl]
            for s in range(1, win):
                wsum = wsum + full[hist + t - s][:, sl]
            cnt = float(min(win, pos0 + t + 1))
            ds[gi].append(wsum / cnt - full[hist + t][:, sl])
    ys = [_bdot(jnp.concatenate(ds[gi], axis=0), w_ref[gi], precise) for gi in range(len(POOL_WINDOWS))]
    y = jnp.concatenate(ys, axis=1) * ls_ref[...]
    nb = u_ref.shape[1]
    for t in range(steps):
        o_ref[t] = y[t * nb:(t + 1) * nb].astype(o_ref.dtype)


def _pool_sample(u_t, past_t, w_pool, ls_pool, layer, pos0, precise):
    steps, nb, _ = u_t.shape
    return pl.pallas_call(
        functools.partial(_pool_sample_kernel, pos0=pos0, precise=precise),
        grid=(1,),
        in_specs=[pl.BlockSpec(u_t.shape, lambda i: (0, 0, 0)),
                  pl.BlockSpec(past_t.shape, lambda i: (0, 0, 0)),
                  pl.BlockSpec((None, len(POOL_WINDOWS), POOL_GW, POOL_GW), lambda i: (layer, 0, 0, 0)),
                  pl.BlockSpec((None, 1, POOL_WIDTH), lambda i: (layer, 0, 0))],
        out_specs=[pl.BlockSpec(u_t.shape, lambda i: (0, 0, 0)), pl.BlockSpec(past_t.shape, lambda i: (0, 0, 0))],
        out_shape=[jax.ShapeDtypeStruct(u_t.shape, _act_dtype(precise)), jax.ShapeDtypeStruct(past_t.shape, F32)],
        compiler_params=_cp("arbitrary"), name="pool_sample",
    )(u_t, past_t, w_pool, ls_pool.reshape(ls_pool.shape[0], 1, POOL_WIDTH))


def _conv_prompt_kernel(cb_ref, cc_ref, cx_ref, w_ref, o_ref, new_ref):
    z = cc_ref[...] * cx_ref[...]
    w = w_ref[...]
    y = w[CONV_K - 1:CONV_K] * z
    for j in range(1, CONV_K):
        y = y + w[CONV_K - 1 - j:CONV_K - j] * _shift_rows(z, j)
    o_ref[...] = (cb_ref[...] * y).astype(o_ref.dtype)
    new_ref[...] = z[z.shape[0] - (CONV_K - 1):]


def _conv_prompt(p, conv_w, layer, batch, seq, precise, *, tc=256):
    nc = CONV_WIDTH // tc

    def col(off):
        return lambda b, c: (b, off // tc + c)

    return pl.pallas_call(
        _conv_prompt_kernel,
        grid=(batch, nc),
        in_specs=[pl.BlockSpec((seq, tc), col(OFF_CB)), pl.BlockSpec((seq, tc), col(OFF_CC)),
                  pl.BlockSpec((seq, tc), col(OFF_CX)),
                  pl.BlockSpec((None, CONV_K, tc), lambda b, c: (layer, 0, c))],
        out_specs=[pl.BlockSpec((seq, tc), lambda b, c: (b, c)),
                   pl.BlockSpec((None, CONV_K - 1, tc), lambda b, c: (b, 0, c))],
        out_shape=[jax.ShapeDtypeStruct((batch * seq, CONV_WIDTH), _act_dtype(precise)),
                   jax.ShapeDtypeStruct((batch, CONV_K - 1, CONV_WIDTH), F32)],
        compiler_params=_cp("parallel", "parallel"), name="conv_prompt",
    )(p, p, p, conv_w)


def _conv_sample_kernel(cb_ref, cc_ref, cx_ref, past_ref, w_ref, o_ref, new_ref):
    steps, hist = cb_ref.shape[0], past_ref.shape[0]
    w = w_ref[...]
    full = [past_ref[i] for i in range(hist)] + [cc_ref[t] * cx_ref[t] for t in range(steps)]
    for t in range(steps):
        y = w[0:1] * full[t]
        for j in range(1, CONV_K):
            y = y + w[j:j + 1] * full[t + j]
        o_ref[t] = (cb_ref[t] * y).astype(o_ref.dtype)
    for i in range(hist):
        new_ref[i] = full[steps + i]


def _conv_sample(cb_t, cc_t, cx_t, past_t, conv_w, layer, precise):
    full3 = lambda shape: pl.BlockSpec(shape, lambda i: (0, 0, 0))
    return pl.pallas_call(
        _conv_sample_kernel,
        grid=(1,),
        in_specs=[full3(cb_t.shape), full3(cc_t.shape), full3(cx_t.shape), full3(past_t.shape),
                  pl.BlockSpec((None, CONV_K, CONV_WIDTH), lambda i: (layer, 0, 0))],
        out_specs=[full3(cb_t.shape), full3(past_t.shape)],
        out_shape=[jax.ShapeDtypeStruct(cb_t.shape, _act_dtype(precise)), jax.ShapeDtypeStruct(past_t.shape, F32)],
        compiler_params=_cp("arbitrary"), name="conv_sample",
    )(cb_t, cc_t, cx_t, past_t, conv_w)


def _head_sum(x):
    rows = x.shape[0]
    return jnp.concatenate(
        [jnp.broadcast_to(jnp.sum(x[:, h * RWKV_HEAD:(h + 1) * RWKV_HEAD], axis=-1, keepdims=True), (rows, RWKV_HEAD))
         for h in range(RWKV_HEADS)], axis=1)


def _softplus(x):
    return jnp.maximum(x, 0.0) + jnp.log(1.0 + jnp.exp(-jnp.abs(x)))


def _rwkv_pre_core(cur, sh, mu, w0, w2, a0, a2, g2, k_k, k_a, outs, precise):
    xr, xk, xv, xwa, xg = [c + (s - c) * m for c, s, m in zip(cur, sh, mu)]
    wd, ad = xwa[:, :DECAY_LORA], xwa[:, DECAY_LORA:]
    w_log = -_softplus(-(w0 + _bdot(jnp.tanh(wd), w2, precise))) - 0.5
    log_decay = -jnp.exp(w_log)
    a = _sigmoid(a0 + _bdot(ad, a2, precise))
    g = _bdot(_sigmoid(xg), g2, precise)
    kk = xk * k_k
    kk = kk / jnp.maximum(jnp.sqrt(_head_sum(kk * kk)), 1e-12)
    kf = xk * (1.0 + (a - 1.0) * k_a)
    r_ref, w_ref, k_ref, v_ref, a_ref, b_ref, g_ref = outs
    r_ref[...] = xr
    w_ref[...] = log_decay
    k_ref[...] = kf
    v_ref[...] = xv
    a_ref[...] = -kk
    b_ref[...] = kk * a
    g_ref[...] = g


_PRE_WIDTHS = (RWKV_WIDTH, RWKV_WIDTH, RWKV_WIDTH, DECAY_LORA + AAA_LORA, GATE_LORA)
_PRE_OFFS = (0, RWKV_WIDTH, 2 * RWKV_WIDTH, 3 * RWKV_WIDTH, 3 * RWKV_WIDTH + DECAY_LORA + AAA_LORA)
_HALO = 8


def _rwkv_pre_prompt_kernel(*refs, precise):
    cur_refs, halo_refs, mu_refs = refs[0:5], refs[5:10], refs[10:15]
    w0, w2, a0, a2, g2, k_k, k_a = [r[...] for r in refs[15:22]]
    outs = refs[22:]
    first = pl.program_id(1) == 0
    cur, sh = [], []
    for c_ref, h_ref in zip(cur_refs, halo_refs):
        c = c_ref[...]
        prev_row = jnp.where(first, 0.0, h_ref[_HALO - 1:_HALO, :])
        rows = lax.broadcasted_iota(jnp.int32, c.shape, 0)
        sh.append(jnp.where(rows == 0, prev_row, pltpu.roll(c, 1, axis=0)))
        cur.append(c)
    _rwkv_pre_core(cur, sh, [m[...] for m in mu_refs], w0, w2, a0, a2, g2, k_k, k_a, outs, precise)


def _rwkv_param_specs(layer):
    def spec(shape):
        return pl.BlockSpec((None,) + shape, lambda *idx: (layer,) + (0,) * len(shape))

    return [spec((1, RWKV_WIDTH)), spec((DECAY_LORA, RWKV_WIDTH)), spec((1, RWKV_WIDTH)),
            spec((AAA_LORA, RWKV_WIDTH)), spec((GATE_LORA, RWKV_WIDTH)), spec((1, RWKV_WIDTH)), spec((1, RWKV_WIDTH))]


def _rwkv_params(prm):
    depth = prm["w0"].shape[0]
    r3 = lambda a: a.reshape(depth, 1, RWKV_WIDTH)
    return [r3(prm["w0"]), prm["w2"], r3(prm["a0"]), prm["a2"], prm["g2"], r3(prm["kk"]), r3(prm["ka"])]


def _rwkv_pre_prompt(p, prm, layer, batch, seq, precise, *, tt=512):
    nt = seq // tt
    cur_specs, halo_specs, mu_specs = [], [], []
    for w, off in zip(_PRE_WIDTHS, _PRE_OFFS):
        cb = (OFF_PC + off) // w
        cur_specs.append(pl.BlockSpec((tt, w), lambda b, t, cb=cb: (b * nt + t, cb)))
        halo_specs.append(pl.BlockSpec(
            (_HALO, w), lambda b, t, cb=cb: (jnp.maximum((b * nt + t) * (tt // _HALO) - 1, 0), cb)))
        mu_specs.append(pl.BlockSpec((None, 1, w), lambda b, t, mb=off // w: (layer, 0, mb)))
    out_spec = pl.BlockSpec((tt, RWKV_WIDTH), lambda b, t: (b * nt + t, 0))
    mu3 = prm["mu"].reshape(prm["mu"].shape[0], 1, RWKV_PROJ)
    return pl.pallas_call(
        functools.partial(_rwkv_pre_prompt_kernel, precise=precise),
        grid=(batch, nt),
        in_specs=cur_specs + halo_specs + mu_specs + _rwkv_param_specs(layer),
        out_specs=[out_spec] * 7,
        out_shape=[jax.ShapeDtypeStruct((batch * seq, RWKV_WIDTH), F32)] * 7,
        compiler_params=_cp("parallel", "parallel"), name="rwkv_pre_prompt",
    )(*([p] * 10), *([mu3] * 5), *_rwkv_params(prm))


def _rwkv_pre_sample_kernel(*refs, precise):
    cur = [r[...] for r in refs[0:5]]
    sh = [r[...] for r in refs[5:10]]
    mu = [r[...] for r in refs[10:15]]
    w0, w2, a0, a2, g2, k_k, k_a = [r[...] for r in refs[15:22]]
    _rwkv_pre_core(cur, sh, mu, w0, w2, a0, a2, g2, k_k, k_a, refs[22:], precise)


def _rwkv_pre_sample(pc, pc_shifted, prm, layer, precise):
    rows = pc.shape[0]
    cur_specs, mu_specs = [], []
    for w, off in zip(_PRE_WIDTHS, _PRE_OFFS):
        cur_specs.append(pl.BlockSpec((rows, w), lambda i, cb=off // w: (0, cb)))
        mu_specs.append(pl.BlockSpec((None, 1, w), lambda i, mb=off // w: (layer, 0, mb)))
    out_spec = pl.BlockSpec((rows, RWKV_WIDTH), lambda i: (0, 0))
    mu3 = prm["mu"].reshape(prm["mu"].shape[0], 1, RWKV_PROJ)
    return pl.pallas_call(
        functools.partial(_rwkv_pre_sample_kernel, precise=precise),
        grid=(1,),
        in_specs=cur_specs + cur_specs + mu_specs + _rwkv_param_specs(layer),
        out_specs=[out_spec] * 7,
        out_shape=[jax.ShapeDtypeStruct((rows, RWKV_WIDTH), F32)] * 7,
        compiler_params=_cp("arbitrary"), name="rwkv_pre_sample",
    )(*([pc] * 5), *([pc_shifted] * 5), *([mu3] * 5), *_rwkv_params(prm))


def _split_heads(x):
    return jnp.stack([x[:, h * RWKV_HEAD:(h + 1) * RWKV_HEAD] for h in range(RWKV_HEADS)], axis=0)


def _join_heads(x):
    return jnp.concatenate([x[h] for h in range(RWKV_HEADS)], axis=1)


def _wkv_steps_kernel(r_ref, ld_ref, k_ref, v_ref, a_ref, b_ref, s0_ref, y_ref, sf_ref, *, steps, bb):
    n = RWKV_HEAD
    eye = lax.broadcasted_iota(jnp.int32, (n, n), 0) == lax.broadcasted_iota(jnp.int32, (n, n), 1)
    for i in range(bb):
        s = s0_ref[i]
        seqs = [_split_heads(ref[i]) for ref in (r_ref, ld_ref, k_ref, v_ref, a_ref, b_ref)]
        out_rows = []
        for t in range(steps):
            r, ld, k, v, a, b = [x[:, t:t + 1, :] for x in seqs]
            sa = jnp.sum(s * a, axis=-1, keepdims=True)
            vcol = jnp.sum(jnp.where(eye, v, 0.0), axis=-1, keepdims=True)
            s = s * jnp.exp(ld) + sa * b + vcol * k
            ycol = jnp.sum(s * r, axis=-1, keepdims=True)
            out_rows.append(jnp.sum(jnp.where(eye, ycol, 0.0), axis=1, keepdims=True))
        y_ref[i] = _join_heads(jnp.concatenate(out_rows, axis=1))
        sf_ref[i] = s


def _wkv_steps(seqs, s0, *, bb=4):
    batch, t, _ = seqs[0].shape
    seq_spec = pl.BlockSpec((bb, t, RWKV_WIDTH), lambda i: (i, 0, 0))
    st_spec = pl.BlockSpec((bb, RWKV_HEADS, RWKV_HEAD, RWKV_HEAD), lambda i: (i, 0, 0, 0))
    return pl.pallas_call(
        functools.partial(_wkv_steps_kernel, steps=t, bb=bb),
        grid=(batch // bb,),
        in_specs=[seq_spec] * 6 + [st_spec],
        out_specs=[seq_spec, st_spec],
        out_shape=[jax.ShapeDtypeStruct((batch, t, RWKV_WIDTH), F32), jax.ShapeDtypeStruct(s0.shape, F32)],
        compiler_params=_cp("parallel"), name="wkv_steps",
    )(*seqs, s0)


WKV_CHUNK = 64


def _e_nt(a, b):
    return jnp.einsum("hqd,hkd->hqk", a, b, preferred_element_type=F32)


def _e_nn(a, b):
    return jnp.einsum("hqk,hkd->hqd", a, b, preferred_element_type=F32)


def _wkv_chunk_kernel(r_ref, ld_ref, k_ref, v_ref, a_ref, b_ref, y_ref, sf_ref, s_scr):
    c = WKV_CHUNK
    dot_nt = functools.partial(_mxu, _e_nt, precise=True)
    dot_nn = functools.partial(_mxu, _e_nn, precise=True)

    def dot_tn(x, y):
        return dot_nn(jnp.swapaxes(x, 1, 2), y)

    @pl.when(pl.program_id(1) == 0)
    def _():
        s_scr[...] = jnp.zeros_like(s_scr)

    ld = ld_ref[...]
    cum = ld
    k = 1
    while k < c:
        cum = cum + _shift_rows(cum, k)
        k *= 2
    e_pos, e_prev, e_neg = jnp.exp(cum), jnp.exp(cum - ld), jnp.exp(-cum)
    at = _split_heads(a_ref[...] * e_prev)
    rt = _split_heads(r_ref[...] * e_pos)
    bt = _split_heads(b_ref[...] * e_neg)
    kt = _split_heads(k_ref[...] * e_neg)
    v = _split_heads(v_ref[...])
    lam = _split_heads(e_pos[c - 1:c, :])

    ti = lax.broadcasted_iota(jnp.int32, (c, c), 0)
    si = lax.broadcasted_iota(jnp.int32, (c, c), 1)
    strict = ti > si

    def blockmask(size):
        same = (ti // size) == (si // size)
        return strict & same & ((ti // (size // 2)) != (si // (size // 2)))

    ar = jnp.concatenate([at, rt], axis=1)
    g_b = dot_nt(ar, bt)
    g_k = dot_nt(ar, kt)
    n_ab = jnp.where(strict, g_b[:, :c, :], 0.0)
    n_ak = jnp.where(strict, g_k[:, :c, :], 0.0)
    m_rb = jnp.where(ti >= si, g_b[:, c:, :], 0.0)
    m_rk = jnp.where(ti >= si, g_k[:, c:, :], 0.0)

    base = 8
    n8 = jnp.where((ti // base) == (si // base), n_ab, 0.0)
    eye = (ti == si).astype(F32)
    n8_2 = dot_nn(n8, n8)
    n8_4 = dot_nn(n8_2, n8_2)
    t_inv = eye + n8
    t_inv = t_inv + dot_nn(t_inv, n8_2)
    t_inv = t_inv + dot_nn(t_inv, n8_4)
    size = 2 * base
    while size <= c:
        off = jnp.where(blockmask(size), n_ab, 0.0)
        t_inv = t_inv + dot_nn(dot_nn(t_inv, off), t_inv)
        size *= 2

    wv = dot_nn(n_ak, v)
    a_bar = dot_nn(t_inv, at)
    u_bar = dot_nn(t_inv, wv)
    r_bar = rt + dot_nn(m_rb, a_bar)
    y_bar = dot_nn(m_rb, u_bar) + dot_nn(m_rk, v)
    phi = dot_tn(a_bar, bt)
    psi = dot_tn(jnp.concatenate([u_bar, v], axis=1), jnp.concatenate([bt, kt], axis=1))

    s0 = s_scr[...]
    y_ref[...] = _join_heads(dot_nt(r_bar, s0) + y_bar)
    s_new = (s0 + dot_nn(s0, phi) + psi) * lam
    s_scr[...] = s_new

    @pl.when(pl.program_id(1) == pl.num_programs(1) - 1)
    def _():
        sf_ref[...] = s_new


def _wkv_chunked(seqs):
    batch, t, _ = seqs[0].shape
    c = WKV_CHUNK
    seq_spec = pl.BlockSpec((None, c, RWKV_WIDTH), lambda b, i: (b, i, 0))
    st_shape = (batch, RWKV_HEADS, RWKV_HEAD, RWKV_HEAD)
    st_spec = pl.BlockSpec((None,) + st_shape[1:], lambda b, i: (b, 0, 0, 0))
    return pl.pallas_call(
        _wkv_chunk_kernel,
        grid=(batch, t // c),
        in_specs=[seq_spec] * 6,
        out_specs=[seq_spec, st_spec],
        out_shape=[jax.ShapeDtypeStruct((batch, t, RWKV_WIDTH), F32), jax.ShapeDtypeStruct(st_shape, F32)],
        scratch_shapes=[pltpu.VMEM(st_shape[1:], F32)],
        compiler_params=_cp("parallel", "arbitrary"), name="wkv_chunk",
    )(*seqs)


def _rwkv_post_kernel(y_ref, r_ref, k_ref, v_ref, g_ref, rk_ref, lg_ref, lb_ref, o_ref):
    y = y_ref[...]
    inv = 1.0 / RWKV_HEAD
    mean = _head_sum(y) * inv
    yc = y - mean
    var = _head_sum(yc * yc) * inv
    yn = yc * lax.rsqrt(var + RWKV_LN_EPS) * lg_ref[...] + lb_ref[...]
    v = v_ref[...]
    bonus = _head_sum(r_ref[...] * k_ref[...] * rk_ref[...]) * v
    o_ref[...] = ((yn + bonus) * g_ref[...]).astype(o_ref.dtype)


def _rwkv_post(y, r, k, v, g, prm, layer, precise, *, tm):
    rows = y.shape[0]
    depth = prm["rk"].shape[0]
    row = pl.BlockSpec((tm, RWKV_WIDTH), lambda i: (i, 0))
    vec = pl.BlockSpec((None, 1, RWKV_WIDTH), lambda i: (layer, 0, 0))
    r3 = lambda a: a.reshape(depth, 1, RWKV_WIDTH)
    return pl.pallas_call(
        _rwkv_post_kernel,
        grid=(rows // tm,),
        in_specs=[row] * 5 + [vec] * 3,
        out_specs=row,
        out_shape=jax.ShapeDtypeStruct((rows, RWKV_WIDTH), _act_dtype(precise)),
        compiler_params=_cp("parallel"), name="rwkv_post",
    )(y, r, k, v, g, r3(prm["rk"]), r3(prm["ln_g"]), r3(prm["ln_b"]))


MOE_TM = 256


def _dispatch_plan(e_idx, tm):
    n = e_idx.shape[0]
    pairs = n * TOP_K
    e_flat = e_idx.reshape(pairs)
    onehot = (e_flat[:, None] == jnp.arange(N_EXPERTS, dtype=jnp.int32)[None, :]).astype(jnp.int32)
    csum = jnp.cumsum(onehot, axis=0)
    counts = csum[-1]
    padded = ((counts + tm - 1) // tm) * tm
    ends = jnp.cumsum(padded)
    starts = ends - padded
    pos = jnp.sum(onehot * (csum - 1 + starts[None, :]), axis=1)
    nt = (pairs + N_EXPERTS * (tm - 1)) // tm
    tile_start = jnp.arange(nt, dtype=jnp.int32) * tm
    tile_e = jnp.minimum(jnp.sum((tile_start[:, None] >= ends[None, :]).astype(jnp.int32), axis=1), N_EXPERTS - 1)
    tile_nv = jnp.clip(jnp.take(starts + counts, tile_e) - tile_start, 0, tm)
    src = jnp.zeros((nt * tm,), jnp.int32).at[pos].set(jnp.arange(pairs, dtype=jnp.int32) // TOP_K)
    return jnp.stack([tile_e, tile_nv]).astype(jnp.int32), src.reshape(nt, 1, tm), pos


def _moe_gather_kernel(meta_ref, src_ref, h_hbm, o_ref, sem):
    nv = meta_ref[1, pl.program_id(0)]
    tm = o_ref.shape[0]

    @pl.when(nv > 0)
    def _():
        def issue(r, c):
            pltpu.make_async_copy(h_hbm.at[pl.ds(src_ref[0, r], 1)], o_ref.at[pl.ds(r, 1)], sem).start()
            return c

        lax.fori_loop(0, tm, issue, 0, unroll=8)
        pltpu.make_async_copy(h_hbm.at[pl.ds(0, tm)], o_ref, sem).wait()

    rows = lax.broadcasted_iota(jnp.int32, o_ref.shape, 0)
    o_ref[...] = jnp.where(rows < nv, o_ref[...], 0.0)


def _moe_gather(meta, src, h):
    nt, _, tm = src.shape
    d = h.shape[1]
    gs = pltpu.PrefetchScalarGridSpec(
        num_scalar_prefetch=1, grid=(nt,),
        in_specs=[pl.BlockSpec((None, 1, tm), lambda j, m: (j, 0, 0), memory_space=pltpu.SMEM),
                  pl.BlockSpec(memory_space=pl.ANY)],
        out_specs=pl.BlockSpec((tm, d), lambda j, m: (j, 0)),
        scratch_shapes=[pltpu.SemaphoreType.DMA(())])
    return pl.pallas_call(
        _moe_gather_kernel, grid_spec=gs, out_shape=jax.ShapeDtypeStruct((nt * tm, d), F32),
        compiler_params=_cp("arbitrary"), name="moe_gather",
    )(meta, src, h)


def _moe_experts_kernel(meta_ref, xs_ref, wg_ref, wu_ref, wd_ref, y_ref, wg_bf, wu_bf, wd_bf):
    j = pl.program_id(0)
    e = meta_ref[0, j]
    nv = meta_ref[1, j]
    new_expert = (j == 0) | (e != meta_ref[0, jnp.maximum(j - 1, 0)])

    @pl.when(new_expert)
    def _():
        wg_bf[...] = wg_ref[...].astype(BF16)
        wu_bf[...] = wu_ref[...].astype(BF16)
        wd_bf[...] = wd_ref[...].astype(BF16)

    @pl.when(nv > 0)
    def _():
        x = xs_ref[...].astype(BF16)
        gate = _dot2(x, wg_bf[...])
        act = gate * _sigmoid(gate) * _dot2(x, wu_bf[...])
        y_ref[...] = _dot2(act.astype(BF16), wd_bf[...])

    @pl.when(nv == 0)
    def _():
        y_ref[...] = jnp.zeros_like(y_ref)


def _moe_experts(meta, xs, w_gate, w_up, w_down, layer):
    rows, d = xs.shape
    nt = meta.shape[1]
    tm = rows // nt
    once = dict(pipeline_mode=pl.Buffered(1))
    gs = pltpu.PrefetchScalarGridSpec(
        num_scalar_prefetch=1, grid=(nt,),
        in_specs=[pl.BlockSpec((tm, d), lambda j, m: (j, 0)),
                  pl.BlockSpec((None, None, d, D_EXPERT), lambda j, m: (layer, m[0, j], 0, 0), **once),
                  pl.BlockSpec((None, None, d, D_EXPERT), lambda j, m: (layer, m[0, j], 0, 0), **once),
                  pl.BlockSpec((None, None, D_EXPERT, d), lambda j, m: (layer, m[0, j], 0, 0), **once)],
        out_specs=pl.BlockSpec((tm, d), lambda j, m: (j, 0)),
        scratch_shapes=[pltpu.VMEM((d, D_EXPERT), BF16), pltpu.VMEM((d, D_EXPERT), BF16),
                        pltpu.VMEM((D_EXPERT, d), BF16)])
    return pl.pallas_call(
        _moe_experts_kernel, grid_spec=gs, out_shape=jax.ShapeDtypeStruct((rows, d), F32),
        compiler_params=_cp("arbitrary"), name="moe_experts",
    )(meta, xs, w_gate, w_up, w_down)


def _moe_combine_kernel(pos_ref, w_ref, y_hbm, m_ref, buf, sem):
    tt = m_ref.shape[0]

    def issue(r, c):
        pltpu.make_async_copy(y_hbm.at[pl.ds(pos_ref[0, r], 1)], buf.at[pl.ds(r, 1)], sem).start()
        return c

    lax.fori_loop(0, TOP_K * tt, issue, 0, unroll=8)
    pltpu.make_async_copy(y_hbm.at[pl.ds(0, TOP_K * tt)], buf, sem).wait()
    w = w_ref[...]
    m_ref[...] = w[:, 0:1] * buf[0:tt, :] + w[:, 1:2] * buf[tt:2 * tt, :]


def _moe_combine(pos, wts, y, *, tt):
    n = wts.shape[0]
    d = y.shape[1]
    pos_t = jnp.swapaxes(pos.reshape(n // tt, tt, TOP_K), 1, 2).reshape(n // tt, 1, TOP_K * tt)
    return pl.pallas_call(
        _moe_combine_kernel,
        grid=(n // tt,),
        in_specs=[pl.BlockSpec((None, 1, TOP_K * tt), lambda i: (i, 0, 0), memory_space=pltpu.SMEM),
                  pl.BlockSpec((tt, TOP_K), lambda i: (i, 0)),
                  pl.BlockSpec(memory_space=pl.ANY)],
        out_specs=pl.BlockSpec((tt, d), lambda i: (i, 0)),
        out_shape=jax.ShapeDtypeStruct((n, d), F32),
        scratch_shapes=[pltpu.VMEM((TOP_K * tt, d), F32), pltpu.SemaphoreType.DMA(())],
        compiler_params=_cp("arbitrary"), name="moe_combine",
    )(pos_t, wts, y)


def _moe(h_all, e_idx, wts, w_gate, w_up, w_down, layer):
    n = h_all.shape[0]
    meta, src, pos = _dispatch_plan(e_idx, MOE_TM)
    xs = _moe_gather(meta, src, h_all)
    y = _moe_experts(meta, xs, w_gate, w_up, w_down, layer)
    tt = 256
    while n % tt:
        tt //= 2
    return _moe_combine(pos, wts, y, tt=tt)


_SH1, _SC1, _GT1, _SH2, _SC2, _GT2 = range(6)


def kernel(x_prompt, x_sample, cache_swa_k, cache_swa_v, state_pool, state_rwkv_shift, state_rwkv_wkv, state_conv, c_prompt, c_sample, w_ada, b_ada, g_norm1, g_norm2, w_in, sinks, w_pool, ls_pool, rwkv_mu, rwkv_w0, rwkv_w2, rwkv_a0, rwkv_a2, rwkv_g2, rwkv_kk, rwkv_ka, rwkv_rk, rwkv_ln_g, rwkv_ln_b, conv_w, w_branch, w_out, w_router, b_router, w_gate, w_up, w_down, g_final):
    depth = w_in.shape[0]
    batch, seq, d = x_prompt.shape
    dbatch, dseq, _ = x_sample.shape
    wbuf = cache_swa_k.shape[2]
    np_rows, ns_rows = batch * seq, dbatch * dseq

    grp_p = _Group(np_rows, seq, 1)
    grp_s = _Group(ns_rows, ns_rows, ns_rows)
    tm_p, tm_s = min(2048, seq), ns_rows
    te_p = min(512, seq)

    c_all = jnp.concatenate([c_prompt, c_sample], axis=0)
    wr_pad = jnp.pad(w_router, ((0, 0), (0, ROUTER_PAD - N_EXPERTS)))
    prm = dict(mu=rwkv_mu, w0=rwkv_w0, w2=rwkv_w2, a0=rwkv_a0, a2=rwkv_a2, g2=rwkv_g2, kk=rwkv_kk, ka=rwkv_ka,
               rk=rwkv_rk.reshape(depth, RWKV_WIDTH), ln_g=rwkv_ln_g, ln_b=rwkv_ln_b)
    cache_k = cache_swa_k.reshape(depth, dbatch, wbuf, KV_WIDTH)
    cache_v = cache_swa_v.reshape(depth, dbatch, wbuf, KV_WIDTH)

    xp = x_prompt.reshape(np_rows, d)
    xs = x_sample.reshape(ns_rows, d)
    st_p, st_s = [], []
    pend_p = pend_s = None
    for l in range(depth):
        precise = l == 0
        hd = _act_dtype(precise)
        mod = _ada(c_all, w_ada, b_ada, l, precise)
        mod_p = mod[:batch].reshape(batch, 1, 6 * d)
        mod_s = jnp.repeat(mod[batch:], dseq, axis=0).reshape(1, ns_rows, 6 * d)

        def first_norm(grp, x, pend, mod3, tm):
            if pend is None:
                return x, _normx(grp, x, g_norm1[l], mod=(mod3, _SC1, _SH1), out_dtype=hd, tm=tm)[0]
            x, h = _normx(grp, x, g_norm1[l], add=pend + (_GT2,), mod=(mod3, _SC1, _SH1), emit_x=True,
                          out_dtype=hd, tm=tm)
            return x, h

        xp, hp = first_norm(grp_p, xp, pend_p, mod_p, te_p)
        xs, hs = first_norm(grp_s, xs, pend_s, mod_s, tm_s)

        def tails(tm, grp):
            if not precise:
                return None
            return (tm, 1) if grp is grp_s else (min(PRECISE_TAIL, tm), seq // tm)

        tm_in = min(tm_p, 1024) if precise else tm_p
        pp = _mm(hp, w_in, l, tm=tm_in, tn=512, tail=tails(tm_in, grp_p))
        ps = _mm(hs, w_in, l, tm=tm_s, tn=512, tail=tails(tm_s, grp_s))

        a_p = _swa_prompt(pp, sinks, l, batch, seq, precise)
        b_p = _pool_prompt(pp, w_pool, ls_pool, l, batch, seq, precise)
        d_p, conv_new_p = _conv_prompt(pp, conv_w, l, batch, seq, precise)
        pre_p = _rwkv_pre_prompt(pp, prm, l, batch, seq, precise, tt=te_p)
        r_p, w_p, k_p, v_p, ka_p, kb_p, g_p = pre_p
        as3 = lambda t: t.reshape(batch, seq, RWKV_WIDTH)
        y_p, wkv_new_p = _wkv_chunked([as3(t) for t in (r_p, w_p, k_p, v_p, ka_p, kb_p)])
        c_p = _rwkv_post(y_p.reshape(np_rows, RWKV_WIDTH), r_p, k_p, v_p, g_p, prm, l, precise, tm=te_p)
        pp3 = pp.reshape(batch, seq, IN_WIDTH)
        kw = min(WINDOW, seq)
        st_p.append((pp3[:, seq - kw:, OFF_K:OFF_K + KV_WIDTH].reshape(batch, kw, N_KV_HEADS, HEAD_DIM),
                     pp3[:, seq - kw:, OFF_V:OFF_V + KV_WIDTH].reshape(batch, kw, N_KV_HEADS, HEAD_DIM),
                     pp3[:, seq - POOL_BUF:, OFF_U:OFF_U + POOL_WIDTH],
                     pp3[:, seq - 1, OFF_PC:OFF_PC + RWKV_PROJ],
                     wkv_new_p, conv_new_p))

        ps3 = ps.reshape(dbatch, dseq, IN_WIDTH)
        a_s, k_new_s, v_new_s = _swa_sample(ps3, cache_k, cache_v, sinks, l, precise)
        tmaj = lambda t: jnp.swapaxes(t, 0, 1)
        b_s_t, pool_new_t = _pool_sample(tmaj(ps3[:, :, OFF_U:OFF_U + POOL_WIDTH]), tmaj(state_pool[l]),
                                         w_pool, ls_pool, l, PAST_LEN, precise)
        d_s_t, conv_new_t = _conv_sample(tmaj(ps3[:, :, OFF_CB:OFF_CB + CONV_WIDTH]),
                                         tmaj(ps3[:, :, OFF_CC:OFF_CC + CONV_WIDTH]),
                                         tmaj(ps3[:, :, OFF_CX:OFF_CX + CONV_WIDTH]), tmaj(state_conv[l]), conv_w, l,
                                         precise)
        pc_s3 = ps3[:, :, OFF_PC:OFF_PC + RWKV_PROJ]
        pc_shift = jnp.concatenate([state_rwkv_shift[l][:, None, :], pc_s3[:, :-1, :]], axis=1)
        pre_s = _rwkv_pre_sample(pc_s3.reshape(ns_rows, RWKV_PROJ), pc_shift.reshape(ns_rows, RWKV_PROJ), prm, l,
                                 precise)
        r_s, w_s, k_s, v_s, ka_s, kb_s, g_s = pre_s
        as3s = lambda t: t.reshape(dbatch, dseq, RWKV_WIDTH)
        y_s, wkv_new_s = _wkv_steps([as3s(t) for t in (r_s, w_s, k_s, v_s, ka_s, kb_s)], state_rwkv_wkv[l])
        c_s = _rwkv_post(y_s.reshape(ns_rows, RWKV_WIDTH), r_s, k_s, v_s, g_s, prm, l, precise, tm=ns_rows)
        st_s.append((k_new_s.reshape(dbatch, wbuf, N_KV_HEADS, HEAD_DIM),
                     v_new_s.reshape(dbatch, wbuf, N_KV_HEADS, HEAD_DIM),
                     tmaj(pool_new_t), pc_s3[:, dseq - 1, :], wkv_new_s, tmaj(conv_new_t)))
        br_s = (a_s.reshape(ns_rows, ATTN_WIDTH), tmaj(b_s_t).reshape(ns_rows, POOL_WIDTH), c_s,
                tmaj(d_s_t).reshape(ns_rows, CONV_WIDTH))

        def tail(grp, x, branches, p, mod3, tm_mm, tm_el):
            if precise:
                tm_t = min(tm_mm, 512)
                merged = _merge(branches, p, w_branch, l, tm=tm_t, tn=512, tail=tails(tm_t, grp))
                x1 = _mm_res(grp, merged, w_out, l, x, mod3, _GT1, tm=tm_t, tn=512, tail=tails(tm_t, grp))
            else:
                merged = _merge(branches, p, w_branch, l, tm=min(tm_mm, 1024), tn=512, tail=None)
                x1 = _mm_res(grp, merged, w_out, l, x, mod3, _GT1, tm=tm_mm, tn=512, tail=None)
            return (x1,) + tuple(_normx(grp, x1, g_norm2[l], mod=(mod3, _SC2, _SH2), route=(wr_pad, b_router),
                                        out_dtype=F32, tm=tm_el))

        xp, h2_p, e_p, wt_p = tail(grp_p, xp, (a_p, b_p, c_p, d_p), pp, mod_p, tm_p, te_p)
        xs, h2_s, e_s, wt_s = tail(grp_s, xs, br_s, ps, mod_s, tm_s, tm_s)
        m_all = _moe(jnp.concatenate([h2_p, h2_s]), jnp.concatenate([e_p, e_s]), jnp.concatenate([wt_p, wt_s]),
                     w_gate, w_up, w_down, l)
        pend_p, pend_s = (m_all, 0, mod_p), (m_all, np_rows, mod_s)

    y_p = _normx(grp_p, xp, g_final, add=pend_p + (_GT2,), out_dtype=F32, tm=te_p)[0]
    y_s = _normx(grp_s, xs, g_final, add=pend_s + (_GT2,), out_dtype=F32, tm=tm_s)[0]

    def stack(states, i):
        return jnp.stack([s[i] for s in states])

    return ((y_p.reshape(batch, seq, d), y_s.reshape(dbatch, dseq, d))
            + tuple(stack(st_p, i) for i in range(6)) + tuple(stack(st_s, i) for i in range(6)))
```

```python
import functools

import jax
import jax.numpy as jnp
from jax import lax
from jax.experimental import pallas as pl
from jax.experimental.pallas import tpu as pltpu

F32 = jnp.float32
BF16 = jnp.bfloat16

D_MODEL = 2048
PAST_LEN = 8192
WINDOW = 128
HEAD_DIM = 64
N_HEADS = 16
N_KV_HEADS = 4
GQA_GROUP = N_HEADS // N_KV_HEADS
ATTN_WIDTH = N_HEADS * HEAD_DIM
KV_WIDTH = N_KV_HEADS * HEAD_DIM
ATTN_SCALE = HEAD_DIM ** -0.5
NEG_INF = -1e30
POOL_WINDOWS = (2, 4, 8, 16)
POOL_WIDTH = 768
POOL_GW = POOL_WIDTH // len(POOL_WINDOWS)
POOL_BUF = max(POOL_WINDOWS) - 1
RWKV_HEAD = 64
RWKV_WIDTH = 768
RWKV_HEADS = RWKV_WIDTH // RWKV_HEAD
DECAY_LORA = 64
AAA_LORA = 64
GATE_LORA = 128
RWKV_PROJ = 3 * RWKV_WIDTH + DECAY_LORA + AAA_LORA + GATE_LORA
RWKV_LN_EPS = 64e-5
CONV_WIDTH = 768
CONV_K = 3
N_BRANCH = 4
BRANCH_SECTIONS = (ATTN_WIDTH, POOL_WIDTH, RWKV_WIDTH, CONV_WIDTH)
MIX_WIDTH = sum(BRANCH_SECTIONS)
N_EXPERTS = 16
N_GROUPS = 4
EXP_PER_GROUP = N_EXPERTS // N_GROUPS
TOP_K = 2
D_EXPERT = 1024
RMS_EPS = 1e-6

OFF_Q = 0
OFF_K = OFF_Q + ATTN_WIDTH
OFF_V = OFF_K + KV_WIDTH
OFF_U = OFF_V + KV_WIDTH
OFF_PC = OFF_U + POOL_WIDTH
OFF_CB = OFF_PC + RWKV_PROJ
OFF_CC = OFF_CB + CONV_WIDTH
OFF_CX = OFF_CC + CONV_WIDTH
OFF_GL = OFF_CX + CONV_WIDTH
IN_WIDTH = OFF_GL + N_BRANCH * D_MODEL

PRECISE_TAIL = 2 * WINDOW
LANES = 128
ROUTER_PAD = LANES
VMEM_LIMIT = 60 * 1024 * 1024


def _cp(*sem):
    return pltpu.CompilerParams(dimension_semantics=sem, vmem_limit_bytes=VMEM_LIMIT)


def _sigmoid(x):
    return 1.0 / (1.0 + jnp.exp(-x))


def _split(x):
    hi = x.astype(BF16)
    return hi, (x.astype(F32) - hi.astype(F32)).astype(BF16)


def _mxu(contract, a, b, precise):
    if not precise:
        return contract(a.astype(BF16), b.astype(BF16))
    ah, al = _split(a)
    bh, bl = _split(b)
    return contract(ah, bh) + (contract(ah, bl) + contract(al, bh))


def _dot2(a, b):
    return jnp.dot(a, b, preferred_element_type=F32)


def _bdot(a, b, precise=False):
    return _mxu(_dot2, a, b, precise)


def _act_dtype(precise):
    return F32 if precise else BF16


def _ada_kernel(c_ref, w_ref, b_ref, o_ref, *, precise):
    c = c_ref[...]
    o_ref[...] = _bdot(c * _sigmoid(c), w_ref[...], precise) + b_ref[...]


def _ada(c_all, w_ada, b_ada, layer, precise):
    depth, d, n = w_ada.shape
    nb = c_all.shape[0]
    tn = 1024
    return pl.pallas_call(
        functools.partial(_ada_kernel, precise=precise),
        grid=(n // tn,),
        in_specs=[pl.BlockSpec((nb, d), lambda j: (0, 0)),
                  pl.BlockSpec((None, d, tn), lambda j: (layer, 0, j)),
                  pl.BlockSpec((None, 1, tn), lambda j: (layer, 0, j))],
        out_specs=pl.BlockSpec((nb, tn), lambda j: (0, j)),
        out_shape=jax.ShapeDtypeStruct((nb, n), F32),
        compiler_params=_cp("parallel"),
        name="ada",
    )(c_all, w_ada, b_ada.reshape(depth, 1, n))


class _Group:
    def __init__(self, rows, rpm, mod_rows):
        self.rows, self.rpm, self.mod_rows = rows, rpm, mod_rows

    def mod_spec(self, tm, width, col_of, row_axis=0):
        if self.mod_rows == 1:
            per = self.rpm // tm
            return pl.BlockSpec((None, 1, width), lambda *idx: (idx[row_axis] // per, 0, col_of(idx)))
        assert tm == self.rpm == self.mod_rows
        return pl.BlockSpec((None, tm, width), lambda *idx: (idx[row_axis], 0, col_of(idx)))


def _route(y, wr, br):
    y_hi = y.astype(BF16)
    y_lo = (y - y_hi.astype(F32)).astype(BF16)
    w_hi = wr.astype(BF16)
    w_lo = (wr - w_hi.astype(F32)).astype(BF16)
    logits = (jnp.dot(y_hi, w_hi, preferred_element_type=F32)
              + (jnp.dot(y_hi, w_lo, preferred_element_type=F32) + jnp.dot(y_lo, w_hi, preferred_element_type=F32)))
    logits = logits[:, :N_EXPERTS] + br
    tm = logits.shape[0]
    e = jnp.exp(logits - jnp.max(logits, axis=-1, keepdims=True))
    probs = e / jnp.sum(e, axis=-1, keepdims=True)
    iota_g = lax.broadcasted_iota(jnp.int32, (tm, EXP_PER_GROUP), 1)
    best = None
    for g in range(N_GROUPS):
        pg = probs[:, g * EXP_PER_GROUP:(g + 1) * EXP_PER_GROUP]
        m1 = jnp.max(pg, axis=-1, keepdims=True)
        i1 = jnp.min(jnp.where(pg == m1, iota_g, EXP_PER_GROUP), axis=-1, keepdims=True)
        rest = jnp.where(iota_g == i1, -1.0, pg)
        m2 = jnp.max(rest, axis=-1, keepdims=True)
        i2 = jnp.min(jnp.where(rest == m2, iota_g, EXP_PER_GROUP), axis=-1, keepdims=True)
        cand = (m1 + m2, m1, m2, i1 + g * EXP_PER_GROUP, i2 + g * EXP_PER_GROUP)
        if best is None:
            best = cand
        else:
            take = cand[0] > best[0]
            best = tuple(jnp.where(take, c, b) for c, b in zip(cand, best))
    _, m1, m2, e1, e2 = best
    den = m1 + m2
    slot = lax.broadcasted_iota(jnp.int32, (tm, TOP_K), 1)
    return jnp.where(slot == 0, e1, e2), jnp.where(slot == 0, m1 / den, m2 / den)


def _normx_kernel(*refs, has_add, has_mod, has_route, emit_x):
    it = iter(refs)
    x_ref = next(it)
    if has_add:
        m_ref, gate_ref = next(it), next(it)
    g_ref = next(it)
    if has_mod:
        sc_ref, sh_ref = next(it), next(it)
    if has_route:
        wr_ref, br_ref = next(it), next(it)
    if emit_x:
        xo_ref = next(it)
    h_ref = next(it)
    if has_route:
        eidx_ref, wts_ref = next(it), next(it)
    x = x_ref[...]
    if has_add:
        x = x + gate_ref[...] * m_ref[...]
    if emit_x:
        xo_ref[...] = x
    y = x * lax.rsqrt(jnp.mean(x * x, axis=-1, keepdims=True) + RMS_EPS) * g_ref[...]
    if has_mod:
        y = y * (1.0 + sc_ref[...]) + sh_ref[...]
    h_ref[...] = y.astype(h_ref.dtype)
    if has_route:
        eidx_ref[...], wts_ref[...] = _route(y, wr_ref[...], br_ref[...])


def _normx(grp, x, g, *, add=None, mod=None, route=None, emit_x=False, out_dtype=BF16, tm=512):
    n, d = x.shape
    row = pl.BlockSpec((tm, d), lambda i: (i, 0))
    vec = pl.BlockSpec((1, d), lambda i: (0, 0))
    args, specs = [x], [row]
    if add is not None:
        m, m_row0, mod3, chunk = add
        args += [m, mod3]
        specs += [pl.BlockSpec((tm, d), lambda i, o=m_row0 // tm: (i + o, 0)),
                  grp.mod_spec(tm, d, lambda idx, c=chunk: c)]
    args.append(g.reshape(1, d))
    specs.append(vec)
    if mod is not None:
        mod3, c_sc, c_sh = mod
        args += [mod3, mod3]
        specs += [grp.mod_spec(tm, d, lambda idx, c=c_sc: c), grp.mod_spec(tm, d, lambda idx, c=c_sh: c)]
    if route is not None:
        wr, br = route
        args += [wr, br.reshape(1, N_EXPERTS)]
        specs += [pl.BlockSpec((d, ROUTER_PAD), lambda i: (0, 0)), pl.BlockSpec((1, N_EXPERTS), lambda i: (0, 0))]
    out_shape, out_specs = [], []
    if emit_x:
        out_shape.append(jax.ShapeDtypeStruct((n, d), F32))
        out_specs.append(row)
    out_shape.append(jax.ShapeDtypeStruct((n, d), out_dtype))
    out_specs.append(row)
    if route is not None:
        out_shape += [jax.ShapeDtypeStruct((n, TOP_K), jnp.int32), jax.ShapeDtypeStruct((n, TOP_K), F32)]
        out_specs += [pl.BlockSpec((tm, TOP_K), lambda i: (i, 0))] * 2
    return pl.pallas_call(
        functools.partial(_normx_kernel, has_add=add is not None, has_mod=mod is not None,
                          has_route=route is not None, emit_x=emit_x),
        grid=(n // tm,), in_specs=specs, out_specs=out_specs, out_shape=out_shape,
        compiler_params=_cp("parallel"), name="normx",
    )(*args)


def _tail_tile(tail):
    return lax.rem(pl.program_id(0) + 1, tail[1]) == 0


def _tail_fix(a, w, tail):
    a_hi, a_lo = _split(a[a.shape[0] - tail[0]:, :])
    w_hi, w_lo = _split(w)
    return _dot2(a_hi, w_lo) + _dot2(a_lo, w_hi)


def _tail_rows(x, rows):
    return x if x.shape[0] == 1 else x[x.shape[0] - rows:, :]


def _mm_kernel(a_ref, w_ref, o_ref, *, tail):
    o_ref[...] = _bdot(a_ref[...], w_ref[...])
    if tail is not None:
        @pl.when(_tail_tile(tail))
        def _():
            lo = o_ref.shape[0] - tail[0]
            o_ref[lo:, :] += _tail_fix(a_ref[...], w_ref[...], tail)


def _mm(a, w3, layer, *, tm, tn, tail):
    m, k = a.shape
    n = w3.shape[-1]
    return pl.pallas_call(
        functools.partial(_mm_kernel, tail=tail),
        grid=(m // tm, n // tn),
        in_specs=[pl.BlockSpec((tm, k), lambda i, j: (i, 0)),
                  pl.BlockSpec((None, k, tn), lambda i, j: (layer, 0, j))],
        out_specs=pl.BlockSpec((tm, tn), lambda i, j: (i, j)),
        out_shape=jax.ShapeDtypeStruct((m, n), F32),
        compiler_params=_cp("parallel", "parallel"), name="mm_in",
    )(a, w3)


def _mm_res_kernel(a_ref, w_ref, x_ref, gate_ref, o_ref, *, tail):
    o_ref[...] = x_ref[...] + gate_ref[...] * _bdot(a_ref[...], w_ref[...])
    if tail is not None:
        @pl.when(_tail_tile(tail))
        def _():
            lo = o_ref.shape[0] - tail[0]
            o_ref[lo:, :] += _tail_rows(gate_ref[...], tail[0]) * _tail_fix(a_ref[...], w_ref[...], tail)


def _mm_res(grp, a, w3, layer, x, mod3, gate_chunk, *, tm, tn, tail):
    m, k = a.shape
    n = w3.shape[-1]
    per_chunk = n // tn
    return pl.pallas_call(
        functools.partial(_mm_res_kernel, tail=tail),
        grid=(m // tm, n // tn),
        in_specs=[pl.BlockSpec((tm, k), lambda i, j: (i, 0)),
                  pl.BlockSpec((None, k, tn), lambda i, j: (layer, 0, j)),
                  pl.BlockSpec((tm, tn), lambda i, j: (i, j)),
                  grp.mod_spec(tm, tn, lambda idx: gate_chunk * per_chunk + idx[1])],
        out_specs=pl.BlockSpec((tm, tn), lambda i, j: (i, j)),
        out_shape=jax.ShapeDtypeStruct((m, n), F32),
        compiler_params=_cp("parallel", "parallel"), name="mm_out",
    )(a, w3, x, mod3)


def _merge_kernel(a_ref, b_ref, c_ref, d_ref, g0_ref, g1_ref, g2_ref, g3_ref, w_ref, o_ref, *, tail):
    parts = []
    lo = 0
    for br_ref, g_ref, width in zip((a_ref, b_ref, c_ref, d_ref), (g0_ref, g1_ref, g2_ref, g3_ref), BRANCH_SECTIONS):
        parts.append((br_ref, g_ref, lo, width))
        lo += width
    acc = None
    for br_ref, g_ref, lo, width in parts:
        t = _sigmoid(g_ref[...]) * _bdot(br_ref[...], w_ref[lo:lo + width, :])
        acc = t if acc is None else acc + t
    o_ref[...] = acc.astype(o_ref.dtype)
    if tail is not None:
        @pl.when(_tail_tile(tail))
        def _():
            row0 = o_ref.shape[0] - tail[0]
            fix = None
            for br_ref, g_ref, lo, width in parts:
                t = _sigmoid(g_ref[row0:, :]) * _tail_fix(br_ref[...], w_ref[lo:lo + width, :], tail)
                fix = t if fix is None else fix + t
            o_ref[row0:, :] += fix


def _merge(branches, p, w_branch, layer, *, tm, tn, tail):
    m = p.shape[0]
    gl_blk = OFF_GL // tn
    per = D_MODEL // tn
    br_specs = [pl.BlockSpec((tm, w), lambda i, j: (i, 0)) for w in BRANCH_SECTIONS]
    gl_specs = [pl.BlockSpec((tm, tn), lambda i, j, b=b: (i, gl_blk + b * per + j)) for b in range(N_BRANCH)]
    return pl.pallas_call(
        functools.partial(_merge_kernel, tail=tail),
        grid=(m // tm, D_MODEL // tn),
        in_specs=br_specs + gl_specs + [pl.BlockSpec((None, MIX_WIDTH, tn), lambda i, j: (layer, 0, j))],
        out_specs=pl.BlockSpec((tm, tn), lambda i, j: (i, j)),
        out_shape=jax.ShapeDtypeStruct((m, D_MODEL), BF16 if tail is None else F32),
        compiler_params=_cp("parallel", "parallel"), name="merge",
    )(*branches, p, p, p, p, w_branch)


def _sink_col(sink_ref, layer, kh, rows_per_head):
    return jnp.concatenate([jnp.full((rows_per_head, 1), sink_ref[layer, kh * GQA_GROUP + g], F32)
                            for g in range(GQA_GROUP)], axis=0)


def _dot_nt(a, b):
    return lax.dot_general(a, b, (((1,), (1,)), ((), ())), preferred_element_type=F32)


def _swa_prompt_kernel(sink_ref, q_ref, kc_ref, kp_ref, vc_ref, vp_ref, o_ref, *, layer, precise):
    n = pl.program_id(1)
    w = WINDOW
    q = q_ref[...]
    qi = jnp.bitwise_and(lax.broadcasted_iota(jnp.int32, (GQA_GROUP * w, 2 * w), 0), w - 1)
    sj = lax.broadcasted_iota(jnp.int32, (GQA_GROUP * w, 2 * w), 1)
    valid = (sj >= qi) & (sj <= qi + w) & ((sj >= w) | (n > 0))

    def run(three_pass):
        outs = []
        for kh in range(N_KV_HEADS):
            sl = slice(kh * HEAD_DIM, (kh + 1) * HEAD_DIM)
            k2 = jnp.concatenate([kp_ref[:, sl], kc_ref[:, sl]], axis=0)
            v2 = jnp.concatenate([vp_ref[:, sl], vc_ref[:, sl]], axis=0)
            q4 = jnp.concatenate([q[:, (kh * GQA_GROUP + g) * HEAD_DIM:(kh * GQA_GROUP + g + 1) * HEAD_DIM]
                                  for g in range(GQA_GROUP)], axis=0)
            s = _mxu(_dot_nt, q4, k2, three_pass) * ATTN_SCALE
            s = jnp.where(valid, s, NEG_INF)
            sk = _sink_col(sink_ref, layer, kh, w)
            m = jnp.maximum(jnp.max(s, axis=-1, keepdims=True), sk)
            e = jnp.exp(s - m)
            p = e / (jnp.sum(e, axis=-1, keepdims=True) + jnp.exp(sk - m))
            o4 = _bdot(p, v2, three_pass)
            outs += [o4[g * w:(g + 1) * w] for g in range(GQA_GROUP)]
        o_ref[...] = jnp.concatenate(outs, axis=1).astype(o_ref.dtype)

    if not precise:
        run(False)
    else:
        last = n >= pl.num_programs(1) - PRECISE_TAIL // w

        @pl.when(last)
        def _():
            run(True)

        @pl.when(jnp.logical_not(last))
        def _():
            run(False)


def _swa_prompt(p, sinks, layer, batch, seq, precise):
    nb = seq // WINDOW
    kblk, vblk = OFF_K // KV_WIDTH, OFF_V // KV_WIDTH

    def cur(col):
        return lambda b, n: (b * nb + n, col)

    def prev(col):
        return lambda b, n: (b * nb + jnp.maximum(n - 1, 0), col)

    return pl.pallas_call(
        functools.partial(_swa_prompt_kernel, layer=layer, precise=precise),
        grid=(batch, nb),
        in_specs=[pl.BlockSpec(memory_space=pltpu.SMEM),
                  pl.BlockSpec((WINDOW, ATTN_WIDTH), cur(0)),
                  pl.BlockSpec((WINDOW, KV_WIDTH), cur(kblk)), pl.BlockSpec((WINDOW, KV_WIDTH), prev(kblk)),
                  pl.BlockSpec((WINDOW, KV_WIDTH), cur(vblk)), pl.BlockSpec((WINDOW, KV_WIDTH), prev(vblk))],
        out_specs=pl.BlockSpec((WINDOW, ATTN_WIDTH), lambda b, n: (b * nb + n, 0)),
        out_shape=jax.ShapeDtypeStruct((batch * seq, ATTN_WIDTH), _act_dtype(precise)),
        compiler_params=_cp("parallel", "parallel"), name="swa_prompt",
    )(sinks, p, p, p, p, p)


def _qk(a, b):
    return jnp.einsum("bqd,bkd->bqk", a, b, preferred_element_type=F32)


def _pv(a, b):
    return jnp.einsum("bqk,bkd->bqd", a, b, preferred_element_type=F32)


def _swa_sample_kernel(sink_ref, q_ref, kn_ref, vn_ref, kc_ref, vc_ref, o_ref, ko_ref, vo_ref, *, layer, steps, wbuf,
                       precise):
    q = q_ref[...]
    kn, vn = kn_ref[...], vn_ref[...]
    kc, vc = kc_ref[...], vc_ref[...]
    ko_ref[:, :wbuf - steps, :] = kc[:, steps:, :]
    ko_ref[:, wbuf - steps:, :] = kn
    vo_ref[:, :wbuf - steps, :] = vc[:, steps:, :]
    vo_ref[:, wbuf - steps:, :] = vn
    bb = q.shape[0]
    rows = GQA_GROUP * steps
    t_c = lax.rem(lax.broadcasted_iota(jnp.int32, (bb, rows, wbuf), 1), steps)
    j_c = lax.broadcasted_iota(jnp.int32, (bb, rows, wbuf), 2)
    dist_c = t_c + wbuf - j_c
    valid_c = (dist_c >= 0) & (dist_c <= WINDOW)
    t_n = lax.rem(lax.broadcasted_iota(jnp.int32, (bb, rows, steps), 1), steps)
    j_n = lax.broadcasted_iota(jnp.int32, (bb, rows, steps), 2)
    valid_n = (t_n - j_n >= 0) & (t_n - j_n <= WINDOW)
    outs = [None] * N_HEADS
    for kh in range(N_KV_HEADS):
        sl = slice(kh * HEAD_DIM, (kh + 1) * HEAD_DIM)
        qg = jnp.concatenate([q[:, :, (kh * GQA_GROUP + g) * HEAD_DIM:(kh * GQA_GROUP + g + 1) * HEAD_DIM]
                              for g in range(GQA_GROUP)], axis=1)
        s_c = _mxu(_qk, qg, kc[:, :, sl], precise) * ATTN_SCALE
        s_n = _mxu(_qk, qg, kn[:, :, sl], precise) * ATTN_SCALE
        s_c = jnp.where(valid_c, s_c, NEG_INF)
        s_n = jnp.where(valid_n, s_n, NEG_INF)
        sk = _sink_col(sink_ref, layer, kh, steps)[None]
        m = jnp.maximum(jnp.maximum(jnp.max(s_c, axis=-1, keepdims=True), jnp.max(s_n, axis=-1, keepdims=True)), sk)
        e_c, e_n = jnp.exp(s_c - m), jnp.exp(s_n - m)
        den = jnp.sum(e_c, axis=-1, keepdims=True) + jnp.sum(e_n, axis=-1, keepdims=True) + jnp.exp(sk - m)
        o = _mxu(_pv, e_c / den, vc[:, :, sl], precise) + _mxu(_pv, e_n / den, vn[:, :, sl], precise)
        for g in range(GQA_GROUP):
            outs[kh * GQA_GROUP + g] = o[:, g * steps:(g + 1) * steps, :]
    o_ref[...] = jnp.concatenate(outs, axis=2).astype(o_ref.dtype)


def _swa_sample(p3, cache_k, cache_v, sinks, layer, precise, *, bb=8):
    batch, steps, _ = p3.shape
    wbuf = cache_k.shape[2]
    kblk, vblk = OFF_K // KV_WIDTH, OFF_V // KV_WIDTH
    cache_spec = pl.BlockSpec((None, bb, wbuf, KV_WIDTH), lambda i: (layer, i, 0, 0))
    new_spec = pl.BlockSpec((bb, wbuf, KV_WIDTH), lambda i: (i, 0, 0))
    return pl.pallas_call(
        functools.partial(_swa_sample_kernel, layer=layer, steps=steps, wbuf=wbuf, precise=precise),
        grid=(batch // bb,),
        in_specs=[pl.BlockSpec(memory_space=pltpu.SMEM),
                  pl.BlockSpec((bb, steps, ATTN_WIDTH), lambda i: (i, 0, 0)),
                  pl.BlockSpec((bb, steps, KV_WIDTH), lambda i: (i, 0, kblk)),
                  pl.BlockSpec((bb, steps, KV_WIDTH), lambda i: (i, 0, vblk)),
                  cache_spec, cache_spec],
        out_specs=[pl.BlockSpec((bb, steps, ATTN_WIDTH), lambda i: (i, 0, 0)), new_spec, new_spec],
        out_shape=[jax.ShapeDtypeStruct((batch, steps, ATTN_WIDTH), _act_dtype(precise)),
                   jax.ShapeDtypeStruct((batch, wbuf, KV_WIDTH), F32),
                   jax.ShapeDtypeStruct((batch, wbuf, KV_WIDTH), F32)],
        compiler_params=_cp("parallel"), name="swa_sample",
    )(sinks, p3, p3, p3, cache_k, cache_v)


def _shift_rows(x, k):
    rows = lax.broadcasted_iota(jnp.int32, x.shape, 0)
    return jnp.where(rows >= k, pltpu.roll(x, k, axis=0), 0.0)


def _pool_prompt_kernel(u_ref, w_ref, ls_ref, o_ref, *, precise):
    u = u_ref[...]
    t = u.shape[0]
    pos1 = (lax.broadcasted_iota(jnp.int32, (t, 1), 0) + 1).astype(F32)
    sums = {1: u}
    win = 1
    while win < max(POOL_WINDOWS):
        sums[2 * win] = sums[win] + _shift_rows(sums[win], win)
        win *= 2
    outs = []
    for gi, win in enumerate(POOL_WINDOWS):
        sl = slice(gi * POOL_GW, (gi + 1) * POOL_GW)
        cnt = jnp.minimum(float(win), pos1)
        d = sums[win][:, sl] / cnt - u[:, sl]
        outs.append(_bdot(d, w_ref[gi], precise))
    o_ref[...] = (jnp.concatenate(outs, axis=1) * ls_ref[...]).astype(o_ref.dtype)


def _pool_prompt(p, w_pool, ls_pool, layer, batch, seq, precise):
    gw = POOL_GW
    return pl.pallas_call(
        functools.partial(_pool_prompt_kernel, precise=precise),
        grid=(batch,),
        in_specs=[pl.BlockSpec((seq, POOL_WIDTH), lambda b: (b, OFF_U // POOL_WIDTH)),
                  pl.BlockSpec((None, len(POOL_WINDOWS), gw, gw), lambda b: (layer, 0, 0, 0)),
                  pl.BlockSpec((None, 1, POOL_WIDTH), lambda b: (layer, 0, 0))],
        out_specs=pl.BlockSpec((seq, POOL_WIDTH), lambda b: (b, 0)),
        out_shape=jax.ShapeDtypeStruct((batch * seq, POOL_WIDTH), _act_dtype(precise)),
        compiler_params=_cp("parallel"), name="pool_prompt",
    )(p, w_pool, ls_pool.reshape(ls_pool.shape[0], 1, POOL_WIDTH))


def _pool_sample_kernel(u_ref, past_ref, w_ref, ls_ref, o_ref, new_ref, *, pos0, precise):
    steps, hist = u_ref.shape[0], past_ref.shape[0]
    full = [past_ref[i] for i in range(hist)] + [u_ref[i] for i in range(steps)]
    for i in range(hist):
        new_ref[i] = full[steps + i]
    ds = [[] for _ in POOL_WINDOWS]
    for t in range(steps):
        for gi, win in enumerate(POOL_WINDOWS):
            sl = slice(gi * POOL_GW, (gi + 1) * POOL_GW)
            wsum = full[hist + t][:, sl]
            for s in range(1, win):
                wsum = wsum + full[hist + t - s][:, sl]
            cnt = float(min(win, pos0 + t + 1))
            ds[gi].append(wsum / cnt - full[hist + t][:, sl])
    ys = [_bdot(jnp.concatenate(ds[gi], axis=0), w_ref[gi], precise) for gi in range(len(POOL_WINDOWS))]
    y = jnp.concatenate(ys, axis=1) * ls_ref[...]
    nb = u_ref.shape[1]
    for t in range(steps):
        o_ref[t] = y[t * nb:(t + 1) * nb].astype(o_ref.dtype)


def _pool_sample(u_t, past_t, w_pool, ls_pool, layer, pos0, precise):
    steps, nb, _ = u_t.shape
    return pl.pallas_call(
        functools.partial(_pool_sample_kernel, pos0=pos0, precise=precise),
        grid=(1,),
        in_specs=[pl.BlockSpec(u_t.shape, lambda i: (0, 0, 0)),
                  pl.BlockSpec(past_t.shape, lambda i: (0, 0, 0)),
                  pl.BlockSpec((None, len(POOL_WINDOWS), POOL_GW, POOL_GW), lambda i: (layer, 0, 0, 0)),
                  pl.BlockSpec((None, 1, POOL_WIDTH), lambda i: (layer, 0, 0))],
        out_specs=[pl.BlockSpec(u_t.shape, lambda i: (0, 0, 0)), pl.BlockSpec(past_t.shape, lambda i: (0, 0, 0))],
        out_shape=[jax.ShapeDtypeStruct(u_t.shape, _act_dtype(precise)), jax.ShapeDtypeStruct(past_t.shape, F32)],
        compiler_params=_cp("arbitrary"), name="pool_sample",
    )(u_t, past_t, w_pool, ls_pool.reshape(ls_pool.shape[0], 1, POOL_WIDTH))


def _conv_prompt_kernel(cb_ref, cc_ref, cx_ref, w_ref, o_ref, new_ref):
    z = cc_ref[...] * cx_ref[...]
    w = w_ref[...]
    y = w[CONV_K - 1:CONV_K] * z
    for j in range(1, CONV_K):
        y = y + w[CONV_K - 1 - j:CONV_K - j] * _shift_rows(z, j)
    o_ref[...] = (cb_ref[...] * y).astype(o_ref.dtype)
    new_ref[...] = z[z.shape[0] - (CONV_K - 1):]


def _conv_prompt(p, conv_w, layer, batch, seq, precise, *, tc=256):
    nc = CONV_WIDTH // tc

    def col(off):
        return lambda b, c: (b, off // tc + c)

    return pl.pallas_call(
        _conv_prompt_kernel,
        grid=(batch, nc),
        in_specs=[pl.BlockSpec((seq, tc), col(OFF_CB)), pl.BlockSpec((seq, tc), col(OFF_CC)),
                  pl.BlockSpec((seq, tc), col(OFF_CX)),
                  pl.BlockSpec((None, CONV_K, tc), lambda b, c: (layer, 0, c))],
        out_specs=[pl.BlockSpec((seq, tc), lambda b, c: (b, c)),
                   pl.BlockSpec((None, CONV_K - 1, tc), lambda b, c: (b, 0, c))],
        out_shape=[jax.ShapeDtypeStruct((batch * seq, CONV_WIDTH), _act_dtype(precise)),
                   jax.ShapeDtypeStruct((batch, CONV_K - 1, CONV_WIDTH), F32)],
        compiler_params=_cp("parallel", "parallel"), name="conv_prompt",
    )(p, p, p, conv_w)


def _conv_sample_kernel(cb_ref, cc_ref, cx_ref, past_ref, w_ref, o_ref, new_ref):
    steps, hist = cb_ref.shape[0], past_ref.shape[0]
    w = w_ref[...]
    full = [past_ref[i] for i in range(hist)] + [cc_ref[t] * cx_ref[t] for t in range(steps)]
    for t in range(steps):
        y = w[0:1] * full[t]
        for j in range(1, CONV_K):
            y = y + w[j:j + 1] * full[t + j]
        o_ref[t] = (cb_ref[t] * y).astype(o_ref.dtype)
    for i in range(hist):
        new_ref[i] = full[steps + i]


def _conv_sample(cb_t, cc_t, cx_t, past_t, conv_w, layer, precise):
    full3 = lambda shape: pl.BlockSpec(shape, lambda i: (0, 0, 0))
    return pl.pallas_call(
        _conv_sample_kernel,
        grid=(1,),
        in_specs=[full3(cb_t.shape), full3(cc_t.shape), full3(cx_t.shape), full3(past_t.shape),
                  pl.BlockSpec((None, CONV_K, CONV_WIDTH), lambda i: (layer, 0, 0))],
        out_specs=[full3(cb_t.shape), full3(past_t.shape)],
        out_shape=[jax.ShapeDtypeStruct(cb_t.shape, _act_dtype(precise)), jax.ShapeDtypeStruct(past_t.shape, F32)],
        compiler_params=_cp("arbitrary"), name="conv_sample",
    )(cb_t, cc_t, cx_t, past_t, conv_w)


def _head_sum(x):
    rows = x.shape[0]
    return jnp.concatenate(
        [jnp.broadcast_to(jnp.sum(x[:, h * RWKV_HEAD:(h + 1) * RWKV_HEAD], axis=-1, keepdims=True), (rows, RWKV_HEAD))
         for h in range(RWKV_HEADS)], axis=1)


def _softplus(x):
    return jnp.maximum(x, 0.0) + jnp.log(1.0 + jnp.exp(-jnp.abs(x)))


def _rwkv_pre_core(cur, sh, mu, w0, w2, a0, a2, g2, k_k, k_a, outs, precise):
    xr, xk, xv, xwa, xg = [c + (s - c) * m for c, s, m in zip(cur, sh, mu)]
    wd, ad = xwa[:, :DECAY_LORA], xwa[:, DECAY_LORA:]
    w_log = -_softplus(-(w0 + _bdot(jnp.tanh(wd), w2, precise))) - 0.5
    log_decay = -jnp.exp(w_log)
    a = _sigmoid(a0 + _bdot(ad, a2, precise))
    g = _bdot(_sigmoid(xg), g2, precise)
    kk = xk * k_k
    kk = kk / jnp.maximum(jnp.sqrt(_head_sum(kk * kk)), 1e-12)
    kf = xk * (1.0 + (a - 1.0) * k_a)
    r_ref, w_ref, k_ref, v_ref, a_ref, b_ref, g_ref = outs
    r_ref[...] = xr
    w_ref[...] = log_decay
    k_ref[...] = kf
    v_ref[...] = xv
    a_ref[...] = -kk
    b_ref[...] = kk * a
    g_ref[...] = g


_PRE_WIDTHS = (RWKV_WIDTH, RWKV_WIDTH, RWKV_WIDTH, DECAY_LORA + AAA_LORA, GATE_LORA)
_PRE_OFFS = (0, RWKV_WIDTH, 2 * RWKV_WIDTH, 3 * RWKV_WIDTH, 3 * RWKV_WIDTH + DECAY_LORA + AAA_LORA)
_HALO = 8


def _rwkv_pre_prompt_kernel(*refs, precise):
    cur_refs, halo_refs, mu_refs = refs[0:5], refs[5:10], refs[10:15]
    w0, w2, a0, a2, g2, k_k, k_a = [r[...] for r in refs[15:22]]
    outs = refs[22:]
    first = pl.program_id(1) == 0
    cur, sh = [], []
    for c_ref, h_ref in zip(cur_refs, halo_refs):
        c = c_ref[...]
        prev_row = jnp.where(first, 0.0, h_ref[_HALO - 1:_HALO, :])
        rows = lax.broadcasted_iota(jnp.int32, c.shape, 0)
        sh.append(jnp.where(rows == 0, prev_row, pltpu.roll(c, 1, axis=0)))
        cur.append(c)
    _rwkv_pre_core(cur, sh, [m[...] for m in mu_refs], w0, w2, a0, a2, g2, k_k, k_a, outs, precise)


def _rwkv_param_specs(layer):
    def spec(shape):
        return pl.BlockSpec((None,) + shape, lambda *idx: (layer,) + (0,) * len(shape))

    return [spec((1, RWKV_WIDTH)), spec((DECAY_LORA, RWKV_WIDTH)), spec((1, RWKV_WIDTH)),
            spec((AAA_LORA, RWKV_WIDTH)), spec((GATE_LORA, RWKV_WIDTH)), spec((1, RWKV_WIDTH)), spec((1, RWKV_WIDTH))]


def _rwkv_params(prm):
    depth = prm["w0"].shape[0]
    r3 = lambda a: a.reshape(depth, 1, RWKV_WIDTH)
    return [r3(prm["w0"]), prm["w2"], r3(prm["a0"]), prm["a2"], prm["g2"], r3(prm["kk"]), r3(prm["ka"])]


def _rwkv_pre_prompt(p, prm, layer, batch, seq, precise, *, tt=512):
    nt = seq // tt
    cur_specs, halo_specs, mu_specs = [], [], []
    for w, off in zip(_PRE_WIDTHS, _PRE_OFFS):
        cb = (OFF_PC + off) // w
        cur_specs.append(pl.BlockSpec((tt, w), lambda b, t, cb=cb: (b * nt + t, cb)))
        halo_specs.append(pl.BlockSpec(
            (_HALO, w), lambda b, t, cb=cb: (jnp.maximum((b * nt + t) * (tt // _HALO) - 1, 0), cb)))
        mu_specs.append(pl.BlockSpec((None, 1, w), lambda b, t, mb=off // w: (layer, 0, mb)))
    out_spec = pl.BlockSpec((tt, RWKV_WIDTH), lambda b, t: (b * nt + t, 0))
    mu3 = prm["mu"].reshape(prm["mu"].shape[0], 1, RWKV_PROJ)
    return pl.pallas_call(
        functools.partial(_rwkv_pre_prompt_kernel, precise=precise),
        grid=(batch, nt),
        in_specs=cur_specs + halo_specs + mu_specs + _rwkv_param_specs(layer),
        out_specs=[out_spec] * 7,
        out_shape=[jax.ShapeDtypeStruct((batch * seq, RWKV_WIDTH), F32)] * 7,
        compiler_params=_cp("parallel", "parallel"), name="rwkv_pre_prompt",
    )(*([p] * 10), *([mu3] * 5), *_rwkv_params(prm))


def _rwkv_pre_sample_kernel(*refs, precise):
    cur = [r[...] for r in refs[0:5]]
    sh = [r[...] for r in refs[5:10]]
    mu = [r[...] for r in refs[10:15]]
    w0, w2, a0, a2, g2, k_k, k_a = [r[...] for r in refs[15:22]]
    _rwkv_pre_core(cur, sh, mu, w0, w2, a0, a2, g2, k_k, k_a, refs[22:], precise)


def _rwkv_pre_sample(pc, pc_shifted, prm, layer, precise):
    rows = pc.shape[0]
    cur_specs, mu_specs = [], []
    for w, off in zip(_PRE_WIDTHS, _PRE_OFFS):
        cur_specs.append(pl.BlockSpec((rows, w), lambda i, cb=off // w: (0, cb)))
        mu_specs.append(pl.BlockSpec((None, 1, w), lambda i, mb=off // w: (layer, 0, mb)))
    out_spec = pl.BlockSpec((rows, RWKV_WIDTH), lambda i: (0, 0))
    mu3 = prm["mu"].reshape(prm["mu"].shape[0], 1, RWKV_PROJ)
    return pl.pallas_call(
        functools.partial(_rwkv_pre_sample_kernel, precise=precise),
        grid=(1,),
        in_specs=cur_specs + cur_specs + mu_specs + _rwkv_param_specs(layer),
        out_specs=[out_spec] * 7,
        out_shape=[jax.ShapeDtypeStruct((rows, RWKV_WIDTH), F32)] * 7,
        compiler_params=_cp("arbitrary"), name="rwkv_pre_sample",
    )(*([pc] * 5), *([pc_shifted] * 5), *([mu3] * 5), *_rwkv_params(prm))


def _split_heads(x):
    return jnp.stack([x[:, h * RWKV_HEAD:(h + 1) * RWKV_HEAD] for h in range(RWKV_HEADS)], axis=0)


def _join_heads(x):
    return jnp.concatenate([x[h] for h in range(RWKV_HEADS)], axis=1)


def _wkv_steps_kernel(r_ref, ld_ref, k_ref, v_ref, a_ref, b_ref, s0_ref, y_ref, sf_ref, *, steps, bb):
    n = RWKV_HEAD
    eye = lax.broadcasted_iota(jnp.int32, (n, n), 0) == lax.broadcasted_iota(jnp.int32, (n, n), 1)
    for i in range(bb):
        s = s0_ref[i]
        seqs = [_split_heads(ref[i]) for ref in (r_ref, ld_ref, k_ref, v_ref, a_ref, b_ref)]
        out_rows = []
        for t in range(steps):
            r, ld, k, v, a, b = [x[:, t:t + 1, :] for x in seqs]
            sa = jnp.sum(s * a, axis=-1, keepdims=True)
            vcol = jnp.sum(jnp.where(eye, v, 0.0), axis=-1, keepdims=True)
            s = s * jnp.exp(ld) + sa * b + vcol * k
            ycol = jnp.sum(s * r, axis=-1, keepdims=True)
            out_rows.append(jnp.sum(jnp.where(eye, ycol, 0.0), axis=1, keepdims=True))
        y_ref[i] = _join_heads(jnp.concatenate(out_rows, axis=1))
        sf_ref[i] = s


def _wkv_steps(seqs, s0, *, bb=4):
    batch, t, _ = seqs[0].shape
    seq_spec = pl.BlockSpec((bb, t, RWKV_WIDTH), lambda i: (i, 0, 0))
    st_spec = pl.BlockSpec((bb, RWKV_HEADS, RWKV_HEAD, RWKV_HEAD), lambda i: (i, 0, 0, 0))
    return pl.pallas_call(
        functools.partial(_wkv_steps_kernel, steps=t, bb=bb),
        grid=(batch // bb,),
        in_specs=[seq_spec] * 6 + [st_spec],
        out_specs=[seq_spec, st_spec],
        out_shape=[jax.ShapeDtypeStruct((batch, t, RWKV_WIDTH), F32), jax.ShapeDtypeStruct(s0.shape, F32)],
        compiler_params=_cp("parallel"), name="wkv_steps",
    )(*seqs, s0)


WKV_CHUNK = 64


def _e_nt(a, b):
    return jnp.einsum("hqd,hkd->hqk", a, b, preferred_element_type=F32)


def _e_nn(a, b):
    return jnp.einsum("hqk,hkd->hqd", a, b, preferred_element_type=F32)


def _wkv_chunk_kernel(r_ref, ld_ref, k_ref, v_ref, a_ref, b_ref, y_ref, sf_ref, s_scr):
    c = WKV_CHUNK

    @pl.when(pl.program_id(1) == 0)
    def _():
        s_scr[...] = jnp.zeros_like(s_scr)

    ld = ld_ref[...]
    cum = ld
    k = 1
    while k < c:
        cum = cum + _shift_rows(cum, k)
        k *= 2
    e_pos, e_prev, e_neg = jnp.exp(cum), jnp.exp(cum - ld), jnp.exp(-cum)
    at = _split_heads(a_ref[...] * e_prev)
    rt = _split_heads(r_ref[...] * e_pos)
    bt = _split_heads(b_ref[...] * e_neg)
    kt = _split_heads(k_ref[...] * e_neg)
    v = _split_heads(v_ref[...])
    lam = _split_heads(e_pos[c - 1:c, :])

    ti = lax.broadcasted_iota(jnp.int32, (c, c), 0)
    si = lax.broadcasted_iota(jnp.int32, (c, c), 1)
    strict = ti > si

    def blockmask(size):
        same = (ti // size) == (si // size)
        return strict & same & ((ti // (size // 2)) != (si // (size // 2)))

    def run(precise):
        dot_nt = functools.partial(_mxu, _e_nt, precise=precise)
        dot_nn = functools.partial(_mxu, _e_nn, precise=precise)

        def dot_tn(x, y):
            return dot_nn(jnp.swapaxes(x, 1, 2), y)

        ar = jnp.concatenate([at, rt], axis=1)
        g_b = dot_nt(ar, bt)
        g_k = dot_nt(ar, kt)
        n_ab = jnp.where(strict, g_b[:, :c, :], 0.0)
        n_ak = jnp.where(strict, g_k[:, :c, :], 0.0)
        m_rb = jnp.where(ti >= si, g_b[:, c:, :], 0.0)
        m_rk = jnp.where(ti >= si, g_k[:, c:, :], 0.0)

        base = 8
        n8 = jnp.where((ti // base) == (si // base), n_ab, 0.0)
        eye = (ti == si).astype(F32)
        n8_2 = dot_nn(n8, n8)
        n8_4 = dot_nn(n8_2, n8_2)
        t_inv = eye + n8
        t_inv = t_inv + dot_nn(t_inv, n8_2)
        t_inv = t_inv + dot_nn(t_inv, n8_4)
        size = 2 * base
        while size <= c:
            off = jnp.where(blockmask(size), n_ab, 0.0)
            t_inv = t_inv + dot_nn(dot_nn(t_inv, off), t_inv)
            size *= 2

        wv = dot_nn(n_ak, v)
        a_bar = dot_nn(t_inv, at)
        u_bar = dot_nn(t_inv, wv)
        r_bar = rt + dot_nn(m_rb, a_bar)
        y_bar = dot_nn(m_rb, u_bar) + dot_nn(m_rk, v)
        phi = dot_tn(a_bar, bt)
        psi = dot_tn(jnp.concatenate([u_bar, v], axis=1), jnp.concatenate([bt, kt], axis=1))

        s0 = s_scr[...]
        y_ref[...] = _join_heads(dot_nt(r_bar, s0) + y_bar)
        s_scr[...] = (s0 + dot_nn(s0, phi) + psi) * lam

    last = pl.program_id(1) >= pl.num_programs(1) - PRECISE_TAIL // c

    @pl.when(last)
    def _():
        run(True)

    @pl.when(jnp.logical_not(last))
    def _():
        run(False)

    @pl.when(pl.program_id(1) == pl.num_programs(1) - 1)
    def _():
        sf_ref[...] = s_scr[...]


def _wkv_chunked(seqs):
    batch, t, _ = seqs[0].shape
    c = WKV_CHUNK
    seq_spec = pl.BlockSpec((None, c, RWKV_WIDTH), lambda b, i: (b, i, 0))
    st_shape = (batch, RWKV_HEADS, RWKV_HEAD, RWKV_HEAD)
    st_spec = pl.BlockSpec((None,) + st_shape[1:], lambda b, i: (b, 0, 0, 0))
    return pl.pallas_call(
        _wkv_chunk_kernel,
        grid=(batch, t // c),
        in_specs=[seq_spec] * 6,
        out_specs=[seq_spec, st_spec],
        out_shape=[jax.ShapeDtypeStruct((batch, t, RWKV_WIDTH), F32), jax.ShapeDtypeStruct(st_shape, F32)],
        scratch_shapes=[pltpu.VMEM(st_shape[1:], F32)],
        compiler_params=_cp("parallel", "arbitrary"), name="wkv_chunk",
    )(*seqs)


def _rwkv_post_kernel(y_ref, r_ref, k_ref, v_ref, g_ref, rk_ref, lg_ref, lb_ref, o_ref):
    y = y_ref[...]
    inv = 1.0 / RWKV_HEAD
    mean = _head_sum(y) * inv
    yc = y - mean
    var = _head_sum(yc * yc) * inv
    yn = yc * lax.rsqrt(var + RWKV_LN_EPS) * lg_ref[...] + lb_ref[...]
    v = v_ref[...]
    bonus = _head_sum(r_ref[...] * k_ref[...] * rk_ref[...]) * v
    o_ref[...] = ((yn + bonus) * g_ref[...]).astype(o_ref.dtype)


def _rwkv_post(y, r, k, v, g, prm, layer, precise, *, tm):
    rows = y.shape[0]
    depth = prm["rk"].shape[0]
    row = pl.BlockSpec((tm, RWKV_WIDTH), lambda i: (i, 0))
    vec = pl.BlockSpec((None, 1, RWKV_WIDTH), lambda i: (layer, 0, 0))
    r3 = lambda a: a.reshape(depth, 1, RWKV_WIDTH)
    return pl.pallas_call(
        _rwkv_post_kernel,
        grid=(rows // tm,),
        in_specs=[row] * 5 + [vec] * 3,
        out_specs=row,
        out_shape=jax.ShapeDtypeStruct((rows, RWKV_WIDTH), _act_dtype(precise)),
        compiler_params=_cp("parallel"), name="rwkv_post",
    )(y, r, k, v, g, r3(prm["rk"]), r3(prm["ln_g"]), r3(prm["ln_b"]))


MOE_TM = 256


def _dispatch_plan(e_idx, tm):
    n = e_idx.shape[0]
    pairs = n * TOP_K
    e_flat = e_idx.reshape(pairs)
    onehot = (e_flat[:, None] == jnp.arange(N_EXPERTS, dtype=jnp.int32)[None, :]).astype(jnp.int32)
    csum = jnp.cumsum(onehot, axis=0)
    counts = csum[-1]
    padded = ((counts + tm - 1) // tm) * tm
    ends = jnp.cumsum(padded)
    starts = ends - padded
    pos = jnp.sum(onehot * (csum - 1 + starts[None, :]), axis=1)
    nt = (pairs + N_EXPERTS * (tm - 1)) // tm
    tile_start = jnp.arange(nt, dtype=jnp.int32) * tm
    tile_e = jnp.minimum(jnp.sum((tile_start[:, None] >= ends[None, :]).astype(jnp.int32), axis=1), N_EXPERTS - 1)
    tile_nv = jnp.clip(jnp.take(starts + counts, tile_e) - tile_start, 0, tm)
    src = jnp.zeros((nt * tm,), jnp.int32).at[pos].set(jnp.arange(pairs, dtype=jnp.int32) // TOP_K)
    return jnp.stack([tile_e, tile_nv]).astype(jnp.int32), src.reshape(nt, 1, tm), pos


def _moe_gather_kernel(meta_ref, src_ref, h_hbm, o_ref, sem):
    nv = meta_ref[1, pl.program_id(0)]
    tm = o_ref.shape[0]

    @pl.when(nv > 0)
    def _():
        def issue(r, c):
            pltpu.make_async_copy(h_hbm.at[pl.ds(src_ref[0, r], 1)], o_ref.at[pl.ds(r, 1)], sem).start()
            return c

        lax.fori_loop(0, tm, issue, 0, unroll=8)
        pltpu.make_async_copy(h_hbm.at[pl.ds(0, tm)], o_ref, sem).wait()

    rows = lax.broadcasted_iota(jnp.int32, o_ref.shape, 0)
    o_ref[...] = jnp.where(rows < nv, o_ref[...], 0.0)


def _moe_gather(meta, src, h):
    nt, _, tm = src.shape
    d = h.shape[1]
    gs = pltpu.PrefetchScalarGridSpec(
        num_scalar_prefetch=1, grid=(nt,),
        in_specs=[pl.BlockSpec((None, 1, tm), lambda j, m: (j, 0, 0), memory_space=pltpu.SMEM),
                  pl.BlockSpec(memory_space=pl.ANY)],
        out_specs=pl.BlockSpec((tm, d), lambda j, m: (j, 0)),
        scratch_shapes=[pltpu.SemaphoreType.DMA(())])
    return pl.pallas_call(
        _moe_gather_kernel, grid_spec=gs, out_shape=jax.ShapeDtypeStruct((nt * tm, d), F32),
        compiler_params=_cp("arbitrary"), name="moe_gather",
    )(meta, src, h)


def _moe_experts_kernel(meta_ref, xs_ref, wg_ref, wu_ref, wd_ref, y_ref, wg_bf, wu_bf, wd_bf):
    j = pl.program_id(0)
    e = meta_ref[0, j]
    nv = meta_ref[1, j]
    new_expert = (j == 0) | (e != meta_ref[0, jnp.maximum(j - 1, 0)])

    @pl.when(new_expert)
    def _():
        wg_bf[...] = wg_ref[...].astype(BF16)
        wu_bf[...] = wu_ref[...].astype(BF16)
        wd_bf[...] = wd_ref[...].astype(BF16)

    @pl.when(nv > 0)
    def _():
        x = xs_ref[...].astype(BF16)
        gate = _dot2(x, wg_bf[...])
        act = gate * _sigmoid(gate) * _dot2(x, wu_bf[...])
        y_ref[...] = _dot2(act.astype(BF16), wd_bf[...])

    @pl.when(nv == 0)
    def _():
        y_ref[...] = jnp.zeros_like(y_ref)


def _moe_experts(meta, xs, w_gate, w_up, w_down, layer):
    rows, d = xs.shape
    nt = meta.shape[1]
    tm = rows // nt
    once = dict(pipeline_mode=pl.Buffered(1))
    gs = pltpu.PrefetchScalarGridSpec(
        num_scalar_prefetch=1, grid=(nt,),
        in_specs=[pl.BlockSpec((tm, d), lambda j, m: (j, 0)),
                  pl.BlockSpec((None, None, d, D_EXPERT), lambda j, m: (layer, m[0, j], 0, 0), **once),
                  pl.BlockSpec((None, None, d, D_EXPERT), lambda j, m: (layer, m[0, j], 0, 0), **once),
                  pl.BlockSpec((None, None, D_EXPERT, d), lambda j, m: (layer, m[0, j], 0, 0), **once)],
        out_specs=pl.BlockSpec((tm, d), lambda j, m: (j, 0)),
        scratch_shapes=[pltpu.VMEM((d, D_EXPERT), BF16), pltpu.VMEM((d, D_EXPERT), BF16),
                        pltpu.VMEM((D_EXPERT, d), BF16)])
    return pl.pallas_call(
        _moe_experts_kernel, grid_spec=gs, out_shape=jax.ShapeDtypeStruct((rows, d), F32),
        compiler_params=_cp("arbitrary"), name="moe_experts",
    )(meta, xs, w_gate, w_up, w_down)


def _moe_combine_kernel(pos_ref, w_ref, y_hbm, m_ref, buf, sem):
    tt = m_ref.shape[0]

    def issue(r, c):
        pltpu.make_async_copy(y_hbm.at[pl.ds(pos_ref[0, r], 1)], buf.at[pl.ds(r, 1)], sem).start()
        return c

    lax.fori_loop(0, TOP_K * tt, issue, 0, unroll=8)
    pltpu.make_async_copy(y_hbm.at[pl.ds(0, TOP_K * tt)], buf, sem).wait()
    w = w_ref[...]
    m_ref[...] = w[:, 0:1] * buf[0:tt, :] + w[:, 1:2] * buf[tt:2 * tt, :]


def _moe_combine(pos, wts, y, *, tt):
    n = wts.shape[0]
    d = y.shape[1]
    pos_t = jnp.swapaxes(pos.reshape(n // tt, tt, TOP_K), 1, 2).reshape(n // tt, 1, TOP_K * tt)
    return pl.pallas_call(
        _moe_combine_kernel,
        grid=(n // tt,),
        in_specs=[pl.BlockSpec((None, 1, TOP_K * tt), lambda i: (i, 0, 0), memory_space=pltpu.SMEM),
                  pl.BlockSpec((tt, TOP_K), lambda i: (i, 0)),
                  pl.BlockSpec(memory_space=pl.ANY)],
        out_specs=pl.BlockSpec((tt, d), lambda i: (i, 0)),
        out_shape=jax.ShapeDtypeStruct((n, d), F32),
        scratch_shapes=[pltpu.VMEM((TOP_K * tt, d), F32), pltpu.SemaphoreType.DMA(())],
        compiler_params=_cp("arbitrary"), name="moe_combine",
    )(pos_t, wts, y)


def _moe(h_all, e_idx, wts, w_gate, w_up, w_down, layer):
    n = h_all.shape[0]
    meta, src, pos = _dispatch_plan(e_idx, MOE_TM)
    xs = _moe_gather(meta, src, h_all)
    y = _moe_experts(meta, xs, w_gate, w_up, w_down, layer)
    tt = 256
    while n % tt:
        tt //= 2
    return _moe_combine(pos, wts, y, tt=tt)


_SH1, _SC1, _GT1, _SH2, _SC2, _GT2 = range(6)


def kernel(x_prompt, x_sample, cache_swa_k, cache_swa_v, state_pool, state_rwkv_shift, state_rwkv_wkv, state_conv, c_prompt, c_sample, w_ada, b_ada, g_norm1, g_norm2, w_in, sinks, w_pool, ls_pool, rwkv_mu, rwkv_w0, rwkv_w2, rwkv_a0, rwkv_a2, rwkv_g2, rwkv_kk, rwkv_ka, rwkv_rk, rwkv_ln_g, rwkv_ln_b, conv_w, w_branch, w_out, w_router, b_router, w_gate, w_up, w_down, g_final):
    depth = w_in.shape[0]
    batch, seq, d = x_prompt.shape
    dbatch, dseq, _ = x_sample.shape
    wbuf = cache_swa_k.shape[2]
    np_rows, ns_rows = batch * seq, dbatch * dseq

    grp_p = _Group(np_rows, seq, 1)
    grp_s = _Group(ns_rows, ns_rows, ns_rows)
    tm_p, tm_s = min(2048, seq), ns_rows
    te_p = min(512, seq)

    c_all = jnp.concatenate([c_prompt, c_sample], axis=0)
    wr_pad = jnp.pad(w_router, ((0, 0), (0, ROUTER_PAD - N_EXPERTS)))
    prm = dict(mu=rwkv_mu, w0=rwkv_w0, w2=rwkv_w2, a0=rwkv_a0, a2=rwkv_a2, g2=rwkv_g2, kk=rwkv_kk, ka=rwkv_ka,
               rk=rwkv_rk.reshape(depth, RWKV_WIDTH), ln_g=rwkv_ln_g, ln_b=rwkv_ln_b)
    cache_k = cache_swa_k.reshape(depth, dbatch, wbuf, KV_WIDTH)
    cache_v = cache_swa_v.reshape(depth, dbatch, wbuf, KV_WIDTH)

    xp = x_prompt.reshape(np_rows, d)
    xs = x_sample.reshape(ns_rows, d)
    st_p, st_s = [], []
    pend_p = pend_s = None
    for l in range(depth):
        precise = l == 0
        hd = _act_dtype(precise)
        mod = _ada(c_all, w_ada, b_ada, l, precise)
        mod_p = mod[:batch].reshape(batch, 1, 6 * d)
        mod_s = jnp.repeat(mod[batch:], dseq, axis=0).reshape(1, ns_rows, 6 * d)

        def first_norm(grp, x, pend, mod3, tm):
            if pend is None:
                return x, _normx(grp, x, g_norm1[l], mod=(mod3, _SC1, _SH1), out_dtype=hd, tm=tm)[0]
            x, h = _normx(grp, x, g_norm1[l], add=pend + (_GT2,), mod=(mod3, _SC1, _SH1), emit_x=True,
                          out_dtype=hd, tm=tm)
            return x, h

        xp, hp = first_norm(grp_p, xp, pend_p, mod_p, te_p)
        xs, hs = first_norm(grp_s, xs, pend_s, mod_s, tm_s)

        def tails(tm, grp):
            if not precise:
                return None
            return (tm, 1) if grp is grp_s else (min(PRECISE_TAIL, tm), seq // tm)

        tm_in = min(tm_p, 1024) if precise else tm_p
        pp = _mm(hp, w_in, l, tm=tm_in, tn=512, tail=tails(tm_in, grp_p))
        ps = _mm(hs, w_in, l, tm=tm_s, tn=512, tail=tails(tm_s, grp_s))

        a_p = _swa_prompt(pp, sinks, l, batch, seq, precise)
        b_p = _pool_prompt(pp, w_pool, ls_pool, l, batch, seq, precise)
        d_p, conv_new_p = _conv_prompt(pp, conv_w, l, batch, seq, precise)
        pre_p = _rwkv_pre_prompt(pp, prm, l, batch, seq, precise, tt=te_p)
        r_p, w_p, k_p, v_p, ka_p, kb_p, g_p = pre_p
        as3 = lambda t: t.reshape(batch, seq, RWKV_WIDTH)
        y_p, wkv_new_p = _wkv_chunked([as3(t) for t in (r_p, w_p, k_p, v_p, ka_p, kb_p)])
        c_p = _rwkv_post(y_p.reshape(np_rows, RWKV_WIDTH), r_p, k_p, v_p, g_p, prm, l, precise, tm=te_p)
        pp3 = pp.reshape(batch, seq, IN_WIDTH)
        kw = min(WINDOW, seq)
        st_p.append((pp3[:, seq - kw:, OFF_K:OFF_K + KV_WIDTH].reshape(batch, kw, N_KV_HEADS, HEAD_DIM),
                     pp3[:, seq - kw:, OFF_V:OFF_V + KV_WIDTH].reshape(batch, kw, N_KV_HEADS, HEAD_DIM),
                     pp3[:, seq - POOL_BUF:, OFF_U:OFF_U + POOL_WIDTH],
                     pp3[:, seq - 1, OFF_PC:OFF_PC + RWKV_PROJ],
                     wkv_new_p, conv_new_p))

        ps3 = ps.reshape(dbatch, dseq, IN_WIDTH)
        a_s, k_new_s, v_new_s = _swa_sample(ps3, cache_k, cache_v, sinks, l, precise)
        tmaj = lambda t: jnp.swapaxes(t, 0, 1)
        b_s_t, pool_new_t = _pool_sample(tmaj(ps3[:, :, OFF_U:OFF_U + POOL_WIDTH]), tmaj(state_pool[l]),
                                         w_pool, ls_pool, l, PAST_LEN, precise)
        d_s_t, conv_new_t = _conv_sample(tmaj(ps3[:, :, OFF_CB:OFF_CB + CONV_WIDTH]),
                                         tmaj(ps3[:, :, OFF_CC:OFF_CC + CONV_WIDTH]),
                                         tmaj(ps3[:, :, OFF_CX:OFF_CX + CONV_WIDTH]), tmaj(state_conv[l]), conv_w, l,
                                         precise)
        pc_s3 = ps3[:, :, OFF_PC:OFF_PC + RWKV_PROJ]
        pc_shift = jnp.concatenate([state_rwkv_shift[l][:, None, :], pc_s3[:, :-1, :]], axis=1)
        pre_s = _rwkv_pre_sample(pc_s3.reshape(ns_rows, RWKV_PROJ), pc_shift.reshape(ns_rows, RWKV_PROJ), prm, l,
                                 precise)
        r_s, w_s, k_s, v_s, ka_s, kb_s, g_s = pre_s
        as3s = lambda t: t.reshape(dbatch, dseq, RWKV_WIDTH)
        y_s, wkv_new_s = _wkv_steps([as3s(t) for t in (r_s, w_s, k_s, v_s, ka_s, kb_s)], state_rwkv_wkv[l])
        c_s = _rwkv_post(y_s.reshape(ns_rows, RWKV_WIDTH), r_s, k_s, v_s, g_s, prm, l, precise, tm=ns_rows)
        st_s.append((k_new_s.reshape(dbatch, wbuf, N_KV_HEADS, HEAD_DIM),
                     v_new_s.reshape(dbatch, wbuf, N_KV_HEADS, HEAD_DIM),
                     tmaj(pool_new_t), pc_s3[:, dseq - 1, :], wkv_new_s, tmaj(conv_new_t)))
        br_s = (a_s.reshape(ns_rows, ATTN_WIDTH), tmaj(b_s_t).reshape(ns_rows, POOL_WIDTH), c_s,
                tmaj(d_s_t).reshape(ns_rows, CONV_WIDTH))

        def tail(grp, x, branches, p, mod3, tm_mm, tm_el):
            if precise:
                tm_t = min(tm_mm, 512)
                merged = _merge(branches, p, w_branch, l, tm=tm_t, tn=512, tail=tails(tm_t, grp))
                x1 = _mm_res(grp, merged, w_out, l, x, mod3, _GT1, tm=tm_t, tn=512, tail=tails(tm_t, grp))
            else:
                merged = _merge(branches, p, w_branch, l, tm=min(tm_mm, 1024), tn=512, tail=None)
                x1 = _mm_res(grp, merged, w_out, l, x, mod3, _GT1, tm=tm_mm, tn=512, tail=None)
            return (x1,) + tuple(_normx(grp, x1, g_norm2[l], mod=(mod3, _SC2, _SH2), route=(wr_pad, b_router),
                                        out_dtype=F32, tm=tm_el))

        xp, h2_p, e_p, wt_p = tail(grp_p, xp, (a_p, b_p, c_p, d_p), pp, mod_p, tm_p, te_p)
        xs, h2_s, e_s, wt_s = tail(grp_s, xs, br_s, ps, mod_s, tm_s, tm_s)
        m_all = _moe(jnp.concatenate([h2_p, h2_s]), jnp.concatenate([e_p, e_s]), jnp.concatenate([wt_p, wt_s]),
                     w_gate, w_up, w_down, l)
        pend_p, pend_s = (m_all, 0, mod_p), (m_all, np_rows, mod_s)

    y_p = _normx(grp_p, xp, g_final, add=pend_p + (_GT2,), out_dtype=F32, tm=te_p)[0]
    y_s = _normx(grp_s, xs, g_final, add=pend_s + (_GT2,), out_dtype=F32, tm=tm_s)[0]

    def stack(states, i):
        return jnp.stack([s[i] for s in states])

    return ((y_p.reshape(batch, seq, d), y_s.reshape(dbatch, dseq, d))
            + tuple(stack(st_p, i) for i in range(6)) + tuple(stack(st_s, i) for i in range(6)))
```

```python
import functools

import jax
import jax.numpy as jnp
from jax import lax
from jax.experimental import pallas as pl
from jax.experimental.pallas import tpu as pltpu

F32 = jnp.float32
BF16 = jnp.bfloat16

D_MODEL = 2048
PAST_LEN = 8192
WINDOW = 128
HEAD_DIM = 64
N_HEADS = 16
N_KV_HEADS = 4
GQA_GROUP = N_HEADS // N_KV_HEADS
ATTN_WIDTH = N_HEADS * HEAD_DIM
KV_WIDTH = N_KV_HEADS * HEAD_DIM
ATTN_SCALE = HEAD_DIM ** -0.5
NEG_INF = -1e30
POOL_WINDOWS = (2, 4, 8, 16)
POOL_WIDTH = 768
POOL_GW = POOL_WIDTH // len(POOL_WINDOWS)
POOL_BUF = max(POOL_WINDOWS) - 1
RWKV_HEAD = 64
RWKV_WIDTH = 768
RWKV_HEADS = RWKV_WIDTH // RWKV_HEAD
DECAY_LORA = 64
AAA_LORA = 64
GATE_LORA = 128
RWKV_PROJ = 3 * RWKV_WIDTH + DECAY_LORA + AAA_LORA + GATE_LORA
RWKV_LN_EPS = 64e-5
CONV_WIDTH = 768
CONV_K = 3
N_BRANCH = 4
BRANCH_SECTIONS = (ATTN_WIDTH, POOL_WIDTH, RWKV_WIDTH, CONV_WIDTH)
MIX_WIDTH = sum(BRANCH_SECTIONS)
N_EXPERTS = 16
N_GROUPS = 4
EXP_PER_GROUP = N_EXPERTS // N_GROUPS
TOP_K = 2
D_EXPERT = 1024
RMS_EPS = 1e-6

OFF_Q = 0
OFF_K = OFF_Q + ATTN_WIDTH
OFF_V = OFF_K + KV_WIDTH
OFF_U = OFF_V + KV_WIDTH
OFF_PC = OFF_U + POOL_WIDTH
OFF_CB = OFF_PC + RWKV_PROJ
OFF_CC = OFF_CB + CONV_WIDTH
OFF_CX = OFF_CC + CONV_WIDTH
OFF_GL = OFF_CX + CONV_WIDTH
IN_WIDTH = OFF_GL + N_BRANCH * D_MODEL

PRECISE_TAIL = 2 * WINDOW
LANES = 128
ROUTER_PAD = LANES
VMEM_LIMIT = 60 * 1024 * 1024


def _cp(*sem):
    return pltpu.CompilerParams(dimension_semantics=sem, vmem_limit_bytes=VMEM_LIMIT)


def _sigmoid(x):
    return 1.0 / (1.0 + jnp.exp(-x))


def _split(x):
    hi = x.astype(BF16)
    return hi, (x.astype(F32) - hi.astype(F32)).astype(BF16)


def _mxu(contract, a, b, precise):
    if not precise:
        return contract(a.astype(BF16), b.astype(BF16))
    ah, al = _split(a)
    bh, bl = _split(b)
    return contract(ah, bh) + (contract(ah, bl) + contract(al, bh))


def _dot2(a, b):
    return jnp.dot(a, b, preferred_element_type=F32)


def _bdot(a, b, precise=False):
    return _mxu(_dot2, a, b, precise)


def _act_dtype(precise):
    return F32 if precise else BF16


def _ada_kernel(c_ref, w_ref, b_ref, o_ref, *, precise):
    c = c_ref[...]
    o_ref[...] = _bdot(c * _sigmoid(c), w_ref[...], precise) + b_ref[...]


def _ada(c_all, w_ada, b_ada, layer, precise):
    depth, d, n = w_ada.shape
    nb = c_all.shape[0]
    tn = 1024
    return pl.pallas_call(
        functools.partial(_ada_kernel, precise=precise),
        grid=(n // tn,),
        in_specs=[pl.BlockSpec((nb, d), lambda j: (0, 0)),
                  pl.BlockSpec((None, d, tn), lambda j: (layer, 0, j)),
                  pl.BlockSpec((None, 1, tn), lambda j: (layer, 0, j))],
        out_specs=pl.BlockSpec((nb, tn), lambda j: (0, j)),
        out_shape=jax.ShapeDtypeStruct((nb, n), F32),
        compiler_params=_cp("parallel"),
        name="ada",
    )(c_all, w_ada, b_ada.reshape(depth, 1, n))


class _Group:
    def __init__(self, rows, rpm, mod_rows):
        self.rows, self.rpm, self.mod_rows = rows, rpm, mod_rows

    def mod_spec(self, tm, width, col_of, row_axis=0):
        if self.mod_rows == 1:
            per = self.rpm // tm
            return pl.BlockSpec((None, 1, width), lambda *idx: (idx[row_axis] // per, 0, col_of(idx)))
        assert tm == self.rpm == self.mod_rows
        return pl.BlockSpec((None, tm, width), lambda *idx: (idx[row_axis], 0, col_of(idx)))


def _route(y, wr, br):
    y_hi = y.astype(BF16)
    y_lo = (y - y_hi.astype(F32)).astype(BF16)
    w_hi = wr.astype(BF16)
    w_lo = (wr - w_hi.astype(F32)).astype(BF16)
    logits = (jnp.dot(y_hi, w_hi, preferred_element_type=F32)
              + (jnp.dot(y_hi, w_lo, preferred_element_type=F32) + jnp.dot(y_lo, w_hi, preferred_element_type=F32)))
    logits = logits[:, :N_EXPERTS] + br
    tm = logits.shape[0]
    e = jnp.exp(logits - jnp.max(logits, axis=-1, keepdims=True))
    probs = e / jnp.sum(e, axis=-1, keepdims=True)
    iota_g = lax.broadcasted_iota(jnp.int32, (tm, EXP_PER_GROUP), 1)
    best = None
    for g in range(N_GROUPS):
        pg = probs[:, g * EXP_PER_GROUP:(g + 1) * EXP_PER_GROUP]
        m1 = jnp.max(pg, axis=-1, keepdims=True)
        i1 = jnp.min(jnp.where(pg == m1, iota_g, EXP_PER_GROUP), axis=-1, keepdims=True)
        rest = jnp.where(iota_g == i1, -1.0, pg)
        m2 = jnp.max(rest, axis=-1, keepdims=True)
        i2 = jnp.min(jnp.where(rest == m2, iota_g, EXP_PER_GROUP), axis=-1, keepdims=True)
        cand = (m1 + m2, m1, m2, i1 + g * EXP_PER_GROUP, i2 + g * EXP_PER_GROUP)
        if best is None:
            best = cand
        else:
            take = cand[0] > best[0]
            best = tuple(jnp.where(take, c, b) for c, b in zip(cand, best))
    _, m1, m2, e1, e2 = best
    den = m1 + m2
    slot = lax.broadcasted_iota(jnp.int32, (tm, TOP_K), 1)
    return jnp.where(slot == 0, e1, e2), jnp.where(slot == 0, m1 / den, m2 / den)


def _normx_kernel(*refs, has_add, has_mod, has_route, emit_x, n_h):
    it = iter(refs)
    x_ref = next(it)
    if has_add:
        m_ref, gate_ref = next(it), next(it)
    g_ref = next(it)
    if has_mod:
        sc_ref, sh_ref = next(it), next(it)
    if has_route:
        wr_ref, br_ref = next(it), next(it)
    if emit_x:
        xo_ref = next(it)
    h_refs = [next(it) for _ in range(n_h)]
    if has_route:
        eidx_ref, wts_ref = next(it), next(it)
    x = x_ref[...]
    if has_add:
        x = x + gate_ref[...] * m_ref[...]
    if emit_x:
        xo_ref[...] = x
    y = x * lax.rsqrt(jnp.mean(x * x, axis=-1, keepdims=True) + RMS_EPS) * g_ref[...]
    if has_mod:
        y = y * (1.0 + sc_ref[...]) + sh_ref[...]
    for h_ref in h_refs:
        h_ref[...] = y.astype(h_ref.dtype)
    if has_route:
        eidx_ref[...], wts_ref[...] = _route(y, wr_ref[...], br_ref[...])


def _normx(grp, x, g, *, add=None, mod=None, route=None, emit_x=False, out_dtype=BF16, tm=512):
    out_dtypes = out_dtype if isinstance(out_dtype, tuple) else (out_dtype,)
    n, d = x.shape
    row = pl.BlockSpec((tm, d), lambda i: (i, 0))
    vec = pl.BlockSpec((1, d), lambda i: (0, 0))
    args, specs = [x], [row]
    if add is not None:
        m, m_row0, mod3, chunk = add
        args += [m, mod3]
        specs += [pl.BlockSpec((tm, d), lambda i, o=m_row0 // tm: (i + o, 0)),
                  grp.mod_spec(tm, d, lambda idx, c=chunk: c)]
    args.append(g.reshape(1, d))
    specs.append(vec)
    if mod is not None:
        mod3, c_sc, c_sh = mod
        args += [mod3, mod3]
        specs += [grp.mod_spec(tm, d, lambda idx, c=c_sc: c), grp.mod_spec(tm, d, lambda idx, c=c_sh: c)]
    if route is not None:
        wr, br = route
        args += [wr, br.reshape(1, N_EXPERTS)]
        specs += [pl.BlockSpec((d, ROUTER_PAD), lambda i: (0, 0)), pl.BlockSpec((1, N_EXPERTS), lambda i: (0, 0))]
    out_shape, out_specs = [], []
    if emit_x:
        out_shape.append(jax.ShapeDtypeStruct((n, d), F32))
        out_specs.append(row)
    out_shape += [jax.ShapeDtypeStruct((n, d), dt) for dt in out_dtypes]
    out_specs += [row] * len(out_dtypes)
    if route is not None:
        out_shape += [jax.ShapeDtypeStruct((n, TOP_K), jnp.int32), jax.ShapeDtypeStruct((n, TOP_K), F32)]
        out_specs += [pl.BlockSpec((tm, TOP_K), lambda i: (i, 0))] * 2
    return pl.pallas_call(
        functools.partial(_normx_kernel, has_add=add is not None, has_mod=mod is not None,
                          has_route=route is not None, emit_x=emit_x, n_h=len(out_dtypes)),
        grid=(n // tm,), in_specs=specs, out_specs=out_specs, out_shape=out_shape,
        compiler_params=_cp("parallel"), name="normx",
    )(*args)


def _tail_tile(tail):
    return lax.rem(pl.program_id(0) + 1, tail[1]) == 0


def _tail_fix(a, w, tail):
    a_hi, a_lo = _split(a[a.shape[0] - tail[0]:, :])
    w_hi, w_lo = _split(w)
    return _dot2(a_hi, w_lo) + _dot2(a_lo, w_hi)


def _tail_rows(x, rows):
    return x if x.shape[0] == 1 else x[x.shape[0] - rows:, :]


def _tail_spec(tm, k, tail_rows):
    per = tm // tail_rows
    return pl.BlockSpec((tail_rows, k), lambda i, j: ((i + 1) * per - 1, 0))


def _mm_kernel(*refs, tail_rows):
    if tail_rows:
        a_ref, at_ref, w_ref, o_ref = refs
    else:
        a_ref, w_ref, o_ref = refs
    o_ref[...] = _bdot(a_ref[...], w_ref[...])
    if tail_rows:
        lo = o_ref.shape[0] - tail_rows
        o_ref[lo:, :] += _tail_fix(at_ref[...], w_ref[...], (tail_rows, 1))


def _mm(a, w3, layer, *, tm, tn, a_f32=None, tail_rows=0):
    m, k = a.shape
    n = w3.shape[-1]
    tails = [a_f32] if tail_rows else []
    return pl.pallas_call(
        functools.partial(_mm_kernel, tail_rows=tail_rows),
        grid=(m // tm, n // tn),
        in_specs=[pl.BlockSpec((tm, k), lambda i, j: (i, 0))]
        + ([_tail_spec(tm, k, tail_rows)] if tail_rows else [])
        + [pl.BlockSpec((None, k, tn), lambda i, j: (layer, 0, j))],
        out_specs=pl.BlockSpec((tm, tn), lambda i, j: (i, j)),
        out_shape=jax.ShapeDtypeStruct((m, n), F32),
        compiler_params=_cp("parallel", "parallel"), name="mm_in",
    )(a, *tails, w3)


def _mm_res_kernel(*refs, tail_rows):
    if tail_rows:
        a_ref, at_ref, w_ref, x_ref, gate_ref, o_ref = refs
    else:
        a_ref, w_ref, x_ref, gate_ref, o_ref = refs
    o_ref[...] = x_ref[...] + gate_ref[...] * _bdot(a_ref[...], w_ref[...])
    if tail_rows:
        lo = o_ref.shape[0] - tail_rows
        o_ref[lo:, :] += _tail_rows(gate_ref[...], tail_rows) * _tail_fix(at_ref[...], w_ref[...], (tail_rows, 1))


def _mm_res(grp, a, w3, layer, x, mod3, gate_chunk, *, tm, tn, a_f32=None, tail_rows=0):
    m, k = a.shape
    n = w3.shape[-1]
    per_chunk = n // tn
    tails = [a_f32] if tail_rows else []
    return pl.pallas_call(
        functools.partial(_mm_res_kernel, tail_rows=tail_rows),
        grid=(m // tm, n // tn),
        in_specs=[pl.BlockSpec((tm, k), lambda i, j: (i, 0))]
        + ([_tail_spec(tm, k, tail_rows)] if tail_rows else [])
        + [pl.BlockSpec((None, k, tn), lambda i, j: (layer, 0, j)),
           pl.BlockSpec((tm, tn), lambda i, j: (i, j)),
           grp.mod_spec(tm, tn, lambda idx: gate_chunk * per_chunk + idx[1])],
        out_specs=pl.BlockSpec((tm, tn), lambda i, j: (i, j)),
        out_shape=jax.ShapeDtypeStruct((m, n), F32),
        compiler_params=_cp("parallel", "parallel"), name="mm_out",
    )(a, *tails, w3, x, mod3)


def _merge_kernel(a_ref, b_ref, c_ref, d_ref, g0_ref, g1_ref, g2_ref, g3_ref, w_ref, o_ref, *ob_ref, tail):
    parts = []
    lo = 0
    for br_ref, g_ref, width in zip((a_ref, b_ref, c_ref, d_ref), (g0_ref, g1_ref, g2_ref, g3_ref), BRANCH_SECTIONS):
        parts.append((br_ref, g_ref, lo, width))
        lo += width
    acc = None
    for br_ref, g_ref, lo, width in parts:
        t = _sigmoid(g_ref[...]) * _bdot(br_ref[...], w_ref[lo:lo + width, :])
        acc = t if acc is None else acc + t
    o_ref[...] = acc.astype(o_ref.dtype)
    if tail is not None:
        @pl.when(_tail_tile(tail))
        def _():
            row0 = o_ref.shape[0] - tail[0]
            fix = None
            for br_ref, g_ref, lo, width in parts:
                t = _sigmoid(g_ref[row0:, :]) * _tail_fix(br_ref[...], w_ref[lo:lo + width, :], tail)
                fix = t if fix is None else fix + t
            o_ref[row0:, :] += fix

        ob_ref[0][...] = o_ref[...].astype(BF16)


def _merge(branches, p, w_branch, layer, *, tm, tn, tail):
    m = p.shape[0]
    gl_blk = OFF_GL // tn
    per = D_MODEL // tn
    br_specs = [pl.BlockSpec((tm, w), lambda i, j: (i, 0)) for w in BRANCH_SECTIONS]
    gl_specs = [pl.BlockSpec((tm, tn), lambda i, j, b=b: (i, gl_blk + b * per + j)) for b in range(N_BRANCH)]
    return pl.pallas_call(
        functools.partial(_merge_kernel, tail=tail),
        grid=(m // tm, D_MODEL // tn),
        in_specs=br_specs + gl_specs + [pl.BlockSpec((None, MIX_WIDTH, tn), lambda i, j: (layer, 0, j))],
        out_specs=[pl.BlockSpec((tm, tn), lambda i, j: (i, j))] * (1 if tail is None else 2),
        out_shape=[jax.ShapeDtypeStruct((m, D_MODEL), BF16)] if tail is None else
        [jax.ShapeDtypeStruct((m, D_MODEL), F32), jax.ShapeDtypeStruct((m, D_MODEL), BF16)],
        compiler_params=_cp("parallel", "parallel"), name="merge",
    )(*branches, p, p, p, p, w_branch)


def _sink_col(sink_ref, layer, kh, rows_per_head):
    return jnp.concatenate([jnp.full((rows_per_head, 1), sink_ref[layer, kh * GQA_GROUP + g], F32)
                            for g in range(GQA_GROUP)], axis=0)


def _dot_nt(a, b):
    return lax.dot_general(a, b, (((1,), (1,)), ((), ())), preferred_element_type=F32)


def _swa_prompt_kernel(sink_ref, q_ref, kc_ref, kp_ref, vc_ref, vp_ref, o_ref, *, layer, precise):
    n = pl.program_id(1)
    w = WINDOW
    q = q_ref[...]
    qi = jnp.bitwise_and(lax.broadcasted_iota(jnp.int32, (GQA_GROUP * w, 2 * w), 0), w - 1)
    sj = lax.broadcasted_iota(jnp.int32, (GQA_GROUP * w, 2 * w), 1)
    valid = (sj >= qi) & (sj <= qi + w) & ((sj >= w) | (n > 0))

    def run(three_pass):
        outs = []
        for kh in range(N_KV_HEADS):
            sl = slice(kh * HEAD_DIM, (kh + 1) * HEAD_DIM)
            k2 = jnp.concatenate([kp_ref[:, sl], kc_ref[:, sl]], axis=0)
            v2 = jnp.concatenate([vp_ref[:, sl], vc_ref[:, sl]], axis=0)
            q4 = jnp.concatenate([q[:, (kh * GQA_GROUP + g) * HEAD_DIM:(kh * GQA_GROUP + g + 1) * HEAD_DIM]
                                  for g in range(GQA_GROUP)], axis=0)
            s = _mxu(_dot_nt, q4, k2, three_pass) * ATTN_SCALE
            s = jnp.where(valid, s, NEG_INF)
            sk = _sink_col(sink_ref, layer, kh, w)
            m = jnp.maximum(jnp.max(s, axis=-1, keepdims=True), sk)
            e = jnp.exp(s - m)
            p = e / (jnp.sum(e, axis=-1, keepdims=True) + jnp.exp(sk - m))
            o4 = _bdot(p, v2, three_pass)
            outs += [o4[g * w:(g + 1) * w] for g in range(GQA_GROUP)]
        o_ref[...] = jnp.concatenate(outs, axis=1).astype(o_ref.dtype)

    if not precise:
        run(False)
    else:
        last = n >= pl.num_programs(1) - PRECISE_TAIL // w

        @pl.when(last)
        def _():
            run(True)

        @pl.when(jnp.logical_not(last))
        def _():
            run(False)


def _swa_prompt(p, sinks, layer, batch, seq, precise):
    nb = seq // WINDOW
    kblk, vblk = OFF_K // KV_WIDTH, OFF_V // KV_WIDTH

    def cur(col):
        return lambda b, n: (b * nb + n, col)

    def prev(col):
        return lambda b, n: (b * nb + jnp.maximum(n - 1, 0), col)

    return pl.pallas_call(
        functools.partial(_swa_prompt_kernel, layer=layer, precise=precise),
        grid=(batch, nb),
        in_specs=[pl.BlockSpec(memory_space=pltpu.SMEM),
                  pl.BlockSpec((WINDOW, ATTN_WIDTH), cur(0)),
                  pl.BlockSpec((WINDOW, KV_WIDTH), cur(kblk)), pl.BlockSpec((WINDOW, KV_WIDTH), prev(kblk)),
                  pl.BlockSpec((WINDOW, KV_WIDTH), cur(vblk)), pl.BlockSpec((WINDOW, KV_WIDTH), prev(vblk))],
        out_specs=pl.BlockSpec((WINDOW, ATTN_WIDTH), lambda b, n: (b * nb + n, 0)),
        out_shape=jax.ShapeDtypeStruct((batch * seq, ATTN_WIDTH), _act_dtype(precise)),
        compiler_params=_cp("parallel", "parallel"), name="swa_prompt",
    )(sinks, p, p, p, p, p)


def _qk(a, b):
    return jnp.einsum("bqd,bkd->bqk", a, b, preferred_element_type=F32)


def _pv(a, b):
    return jnp.einsum("bqk,bkd->bqd", a, b, preferred_element_type=F32)


def _swa_sample_kernel(sink_ref, q_ref, kn_ref, vn_ref, kc_ref, vc_ref, o_ref, ko_ref, vo_ref, *, layer, steps, wbuf,
                       precise):
    q = q_ref[...]
    kn, vn = kn_ref[...], vn_ref[...]
    kc, vc = kc_ref[...], vc_ref[...]
    ko_ref[:, :wbuf - steps, :] = kc[:, steps:, :]
    ko_ref[:, wbuf - steps:, :] = kn
    vo_ref[:, :wbuf - steps, :] = vc[:, steps:, :]
    vo_ref[:, wbuf - steps:, :] = vn
    bb = q.shape[0]
    rows = GQA_GROUP * steps
    t_c = lax.rem(lax.broadcasted_iota(jnp.int32, (bb, rows, wbuf), 1), steps)
    j_c = lax.broadcasted_iota(jnp.int32, (bb, rows, wbuf), 2)
    dist_c = t_c + wbuf - j_c
    valid_c = (dist_c >= 0) & (dist_c <= WINDOW)
    t_n = lax.rem(lax.broadcasted_iota(jnp.int32, (bb, rows, steps), 1), steps)
    j_n = lax.broadcasted_iota(jnp.int32, (bb, rows, steps), 2)
    valid_n = (t_n - j_n >= 0) & (t_n - j_n <= WINDOW)
    outs = [None] * N_HEADS
    for kh in range(N_KV_HEADS):
        sl = slice(kh * HEAD_DIM, (kh + 1) * HEAD_DIM)
        qg = jnp.concatenate([q[:, :, (kh * GQA_GROUP + g) * HEAD_DIM:(kh * GQA_GROUP + g + 1) * HEAD_DIM]
                              for g in range(GQA_GROUP)], axis=1)
        s_c = _mxu(_qk, qg, kc[:, :, sl], precise) * ATTN_SCALE
        s_n = _mxu(_qk, qg, kn[:, :, sl], precise) * ATTN_SCALE
        s_c = jnp.where(valid_c, s_c, NEG_INF)
        s_n = jnp.where(valid_n, s_n, NEG_INF)
        sk = _sink_col(sink_ref, layer, kh, steps)[None]
        m = jnp.maximum(jnp.maximum(jnp.max(s_c, axis=-1, keepdims=True), jnp.max(s_n, axis=-1, keepdims=True)), sk)
        e_c, e_n = jnp.exp(s_c - m), jnp.exp(s_n - m)
        den = jnp.sum(e_c, axis=-1, keepdims=True) + jnp.sum(e_n, axis=-1, keepdims=True) + jnp.exp(sk - m)
        o = _mxu(_pv, e_c / den, vc[:, :, sl], precise) + _mxu(_pv, e_n / den, vn[:, :, sl], precise)
        for g in range(GQA_GROUP):
            outs[kh * GQA_GROUP + g] = o[:, g * steps:(g + 1) * steps, :]
    o_ref[...] = jnp.concatenate(outs, axis=2).astype(o_ref.dtype)


def _swa_sample(p3, cache_k, cache_v, sinks, layer, precise, *, bb=8):
    batch, steps, _ = p3.shape
    wbuf = cache_k.shape[2]
    kblk, vblk = OFF_K // KV_WIDTH, OFF_V // KV_WIDTH
    cache_spec = pl.BlockSpec((None, bb, wbuf, KV_WIDTH), lambda i: (layer, i, 0, 0))
    new_spec = pl.BlockSpec((bb, wbuf, KV_WIDTH), lambda i: (i, 0, 0))
    return pl.pallas_call(
        functools.partial(_swa_sample_kernel, layer=layer, steps=steps, wbuf=wbuf, precise=precise),
        grid=(batch // bb,),
        in_specs=[pl.BlockSpec(memory_space=pltpu.SMEM),
                  pl.BlockSpec((bb, steps, ATTN_WIDTH), lambda i: (i, 0, 0)),
                  pl.BlockSpec((bb, steps, KV_WIDTH), lambda i: (i, 0, kblk)),
                  pl.BlockSpec((bb, steps, KV_WIDTH), lambda i: (i, 0, vblk)),
                  cache_spec, cache_spec],
        out_specs=[pl.BlockSpec((bb, steps, ATTN_WIDTH), lambda i: (i, 0, 0)), new_spec, new_spec],
        out_shape=[jax.ShapeDtypeStruct((batch, steps, ATTN_WIDTH), _act_dtype(precise)),
                   jax.ShapeDtypeStruct((batch, wbuf, KV_WIDTH), F32),
                   jax.ShapeDtypeStruct((batch, wbuf, KV_WIDTH), F32)],
        compiler_params=_cp("parallel"), name="swa_sample",
    )(sinks, p3, p3, p3, cache_k, cache_v)


def _shift_rows(x, k):
    rows = lax.broadcasted_iota(jnp.int32, x.shape, 0)
    return jnp.where(rows >= k, pltpu.roll(x, k, axis=0), 0.0)


def _pool_prompt_kernel(u_ref, w_ref, ls_ref, o_ref, *, precise):
    u = u_ref[...]
    t = u.shape[0]
    pos1 = (lax.broadcasted_iota(jnp.int32, (t, 1), 0) + 1).astype(F32)
    sums = {1: u}
    win = 1
    while win < max(POOL_WINDOWS):
        sums[2 * win] = sums[win] + _shift_rows(sums[win], win)
        win *= 2
    outs = []
    for gi, win in enumerate(POOL_WINDOWS):
        sl = slice(gi * POOL_GW, (gi + 1) * POOL_GW)
        cnt = jnp.minimum(float(win), pos1)
        d = sums[win][:, sl] / cnt - u[:, sl]
        outs.append(_bdot(d, w_ref[gi], precise))
    o_ref[...] = (jnp.concatenate(outs, axis=1) * ls_ref[...]).astype(o_ref.dtype)


def _pool_prompt(p, w_pool, ls_pool, layer, batch, seq, precise):
    gw = POOL_GW
    return pl.pallas_call(
        functools.partial(_pool_prompt_kernel, precise=precise),
        grid=(batch,),
        in_specs=[pl.BlockSpec((seq, POOL_WIDTH), lambda b: (b, OFF_U // POOL_WIDTH)),
                  pl.BlockSpec((None, len(POOL_WINDOWS), gw, gw), lambda b: (layer, 0, 0, 0)),
                  pl.BlockSpec((None, 1, POOL_WIDTH), lambda b: (layer, 0, 0))],
        out_specs=pl.BlockSpec((seq, POOL_WIDTH), lambda b: (b, 0)),
        out_shape=jax.ShapeDtypeStruct((batch * seq, POOL_WIDTH), _act_dtype(precise)),
        compiler_params=_cp("parallel"), name="pool_prompt",
    )(p, w_pool, ls_pool.reshape(ls_pool.shape[0], 1, POOL_WIDTH))


def _pool_sample_kernel(u_ref, past_ref, w_ref, ls_ref, o_ref, new_ref, *, pos0, precise):
    steps, hist = u_ref.shape[0], past_ref.shape[0]
    full = [past_ref[i] for i in range(hist)] + [u_ref[i] for i in range(steps)]
    for i in range(hist):
        new_ref[i] = full[steps + i]
    ds = [[] for _ in POOL_WINDOWS]
    for t in range(steps):
        for gi, win in enumerate(POOL_WINDOWS):
            sl = slice(gi * POOL_GW, (gi + 1) * POOL_GW)
            wsum = full[hist + t][:, sl]
            for s in range(1, win):
                wsum = wsum + full[hist + t - s][:, sl]
            cnt = float(min(win, pos0 + t + 1))
            ds[gi].append(wsum / cnt - full[hist + t][:, sl])
    ys = [_bdot(jnp.concatenate(ds[gi], axis=0), w_ref[gi], precise) for gi in range(len(POOL_WINDOWS))]
    y = jnp.concatenate(ys, axis=1) * ls_ref[...]
    nb = u_ref.shape[1]
    for t in range(steps):
        o_ref[t] = y[t * nb:(t + 1) * nb].astype(o_ref.dtype)


def _pool_sample(u_t, past_t, w_pool, ls_pool, layer, pos0, precise):
    steps, nb, _ = u_t.shape
    return pl.pallas_call(
        functools.partial(_pool_sample_kernel, pos0=pos0, precise=precise),
        grid=(1,),
        in_specs=[pl.BlockSpec(u_t.shape, lambda i: (0, 0, 0)),
                  pl.BlockSpec(past_t.shape, lambda i: (0, 0, 0)),
                  pl.BlockSpec((None, len(POOL_WINDOWS), POOL_GW, POOL_GW), lambda i: (layer, 0, 0, 0)),
                  pl.BlockSpec((None, 1, POOL_WIDTH), lambda i: (layer, 0, 0))],
        out_specs=[pl.BlockSpec(u_t.shape, lambda i: (0, 0, 0)), pl.BlockSpec(past_t.shape, lambda i: (0, 0, 0))],
        out_shape=[jax.ShapeDtypeStruct(u_t.shape, _act_dtype(precise)), jax.ShapeDtypeStruct(past_t.shape, F32)],
        compiler_params=_cp("arbitrary"), name="pool_sample",
    )(u_t, past_t, w_pool, ls_pool.reshape(ls_pool.shape[0], 1, POOL_WIDTH))


def _conv_prompt_kernel(cb_ref, cc_ref, cx_ref, w_ref, o_ref, new_ref):
    z = cc_ref[...] * cx_ref[...]
    w = w_ref[...]
    y = w[CONV_K - 1:CONV_K] * z
    for j in range(1, CONV_K):
        y = y + w[CONV_K - 1 - j:CONV_K - j] * _shift_rows(z, j)
    o_ref[...] = (cb_ref[...] * y).astype(o_ref.dtype)
    new_ref[...] = z[z.shape[0] - (CONV_K - 1):]


def _conv_prompt(p, conv_w, layer, batch, seq, precise, *, tc=256):
    nc = CONV_WIDTH // tc

    def col(off):
        return lambda b, c: (b, off // tc + c)

    return pl.pallas_call(
        _conv_prompt_kernel,
        grid=(batch, nc),
        in_specs=[pl.BlockSpec((seq, tc), col(OFF_CB)), pl.BlockSpec((seq, tc), col(OFF_CC)),
                  pl.BlockSpec((seq, tc), col(OFF_CX)),
                  pl.BlockSpec((None, CONV_K, tc), lambda b, c: (layer, 0, c))],
        out_specs=[pl.BlockSpec((seq, tc), lambda b, c: (b, c)),
                   pl.BlockSpec((None, CONV_K - 1, tc), lambda b, c: (b, 0, c))],
        out_shape=[jax.ShapeDtypeStruct((batch * seq, CONV_WIDTH), _act_dtype(precise)),
                   jax.ShapeDtypeStruct((batch, CONV_K - 1, CONV_WIDTH), F32)],
        compiler_params=_cp("parallel", "parallel"), name="conv_prompt",
    )(p, p, p, conv_w)


def _conv_sample_kernel(cb_ref, cc_ref, cx_ref, past_ref, w_ref, o_ref, new_ref):
    steps, hist = cb_ref.shape[0], past_ref.shape[0]
    w = w_ref[...]
    full = [past_ref[i] for i in range(hist)] + [cc_ref[t] * cx_ref[t] for t in range(steps)]
    for t in range(steps):
        y = w[0:1] * full[t]
        for j in range(1, CONV_K):
            y = y + w[j:j + 1] * full[t + j]
        o_ref[t] = (cb_ref[t] * y).astype(o_ref.dtype)
    for i in range(hist):
        new_ref[i] = full[steps + i]


def _conv_sample(cb_t, cc_t, cx_t, past_t, conv_w, layer, precise):
    full3 = lambda shape: pl.BlockSpec(shape, lambda i: (0, 0, 0))
    return pl.pallas_call(
        _conv_sample_kernel,
        grid=(1,),
        in_specs=[full3(cb_t.shape), full3(cc_t.shape), full3(cx_t.shape), full3(past_t.shape),
                  pl.BlockSpec((None, CONV_K, CONV_WIDTH), lambda i: (layer, 0, 0))],
        out_specs=[full3(cb_t.shape), full3(past_t.shape)],
        out_shape=[jax.ShapeDtypeStruct(cb_t.shape, _act_dtype(precise)), jax.ShapeDtypeStruct(past_t.shape, F32)],
        compiler_params=_cp("arbitrary"), name="conv_sample",
    )(cb_t, cc_t, cx_t, past_t, conv_w)


def _head_sum(x):
    rows = x.shape[0]
    return jnp.concatenate(
        [jnp.broadcast_to(jnp.sum(x[:, h * RWKV_HEAD:(h + 1) * RWKV_HEAD], axis=-1, keepdims=True), (rows, RWKV_HEAD))
         for h in range(RWKV_HEADS)], axis=1)


def _softplus(x):
    return jnp.maximum(x, 0.0) + jnp.log(1.0 + jnp.exp(-jnp.abs(x)))


def _rwkv_pre_core(cur, sh, mu, w0, w2, a0, a2, g2, k_k, k_a, outs, precise):
    xr, xk, xv, xwa, xg = [c + (s - c) * m for c, s, m in zip(cur, sh, mu)]
    wd, ad = xwa[:, :DECAY_LORA], xwa[:, DECAY_LORA:]
    w_log = -_softplus(-(w0 + _bdot(jnp.tanh(wd), w2, precise))) - 0.5
    log_decay = -jnp.exp(w_log)
    a = _sigmoid(a0 + _bdot(ad, a2, precise))
    g = _bdot(_sigmoid(xg), g2, precise)
    kk = xk * k_k
    kk = kk / jnp.maximum(jnp.sqrt(_head_sum(kk * kk)), 1e-12)
    kf = xk * (1.0 + (a - 1.0) * k_a)
    r_ref, w_ref, k_ref, v_ref, a_ref, b_ref, g_ref = outs
    r_ref[...] = xr
    w_ref[...] = log_decay
    k_ref[...] = kf
    v_ref[...] = xv
    a_ref[...] = -kk
    b_ref[...] = kk * a
    g_ref[...] = g


_PRE_WIDTHS = (RWKV_WIDTH, RWKV_WIDTH, RWKV_WIDTH, DECAY_LORA + AAA_LORA, GATE_LORA)
_PRE_OFFS = (0, RWKV_WIDTH, 2 * RWKV_WIDTH, 3 * RWKV_WIDTH, 3 * RWKV_WIDTH + DECAY_LORA + AAA_LORA)
_HALO = 8


def _rwkv_pre_prompt_kernel(*refs, precise):
    cur_refs, halo_refs, mu_refs = refs[0:5], refs[5:10], refs[10:15]
    w0, w2, a0, a2, g2, k_k, k_a = [r[...] for r in refs[15:22]]
    outs = refs[22:]
    first = pl.program_id(1) == 0
    cur, sh = [], []
    for c_ref, h_ref in zip(cur_refs, halo_refs):
        c = c_ref[...]
        prev_row = jnp.where(first, 0.0, h_ref[_HALO - 1:_HALO, :])
        rows = lax.broadcasted_iota(jnp.int32, c.shape, 0)
        sh.append(jnp.where(rows == 0, prev_row, pltpu.roll(c, 1, axis=0)))
        cur.append(c)
    _rwkv_pre_core(cur, sh, [m[...] for m in mu_refs], w0, w2, a0, a2, g2, k_k, k_a, outs, precise)


def _rwkv_param_specs(layer):
    def spec(shape):
        return pl.BlockSpec((None,) + shape, lambda *idx: (layer,) + (0,) * len(shape))

    return [spec((1, RWKV_WIDTH)), spec((DECAY_LORA, RWKV_WIDTH)), spec((1, RWKV_WIDTH)),
            spec((AAA_LORA, RWKV_WIDTH)), spec((GATE_LORA, RWKV_WIDTH)), spec((1, RWKV_WIDTH)), spec((1, RWKV_WIDTH))]


def _rwkv_params(prm):
    depth = prm["w0"].shape[0]
    r3 = lambda a: a.reshape(depth, 1, RWKV_WIDTH)
    return [r3(prm["w0"]), prm["w2"], r3(prm["a0"]), prm["a2"], prm["g2"], r3(prm["kk"]), r3(prm["ka"])]


def _rwkv_pre_prompt(p, prm, layer, batch, seq, precise, *, tt=512):
    nt = seq // tt
    cur_specs, halo_specs, mu_specs = [], [], []
    for w, off in zip(_PRE_WIDTHS, _PRE_OFFS):
        cb = (OFF_PC + off) // w
        cur_specs.append(pl.BlockSpec((tt, w), lambda b, t, cb=cb: (b * nt + t, cb)))
        halo_specs.append(pl.BlockSpec(
            (_HALO, w), lambda b, t, cb=cb: (jnp.maximum((b * nt + t) * (tt // _HALO) - 1, 0), cb)))
        mu_specs.append(pl.BlockSpec((None, 1, w), lambda b, t, mb=off // w: (layer, 0, mb)))
    out_spec = pl.BlockSpec((tt, RWKV_WIDTH), lambda b, t: (b * nt + t, 0))
    mu3 = prm["mu"].reshape(prm["mu"].shape[0], 1, RWKV_PROJ)
    return pl.pallas_call(
        functools.partial(_rwkv_pre_prompt_kernel, precise=precise),
        grid=(batch, nt),
        in_specs=cur_specs + halo_specs + mu_specs + _rwkv_param_specs(layer),
        out_specs=[out_spec] * 7,
        out_shape=[jax.ShapeDtypeStruct((batch * seq, RWKV_WIDTH), F32)] * 7,
        compiler_params=_cp("parallel", "parallel"), name="rwkv_pre_prompt",
    )(*([p] * 10), *([mu3] * 5), *_rwkv_params(prm))


def _rwkv_pre_sample_kernel(*refs, precise):
    cur = [r[...] for r in refs[0:5]]
    sh = [r[...] for r in refs[5:10]]
    mu = [r[...] for r in refs[10:15]]
    w0, w2, a0, a2, g2, k_k, k_a = [r[...] for r in refs[15:22]]
    _rwkv_pre_core(cur, sh, mu, w0, w2, a0, a2, g2, k_k, k_a, refs[22:], precise)


def _rwkv_pre_sample(pc, pc_shifted, prm, layer, precise):
    rows = pc.shape[0]
    cur_specs, mu_specs = [], []
    for w, off in zip(_PRE_WIDTHS, _PRE_OFFS):
        cur_specs.append(pl.BlockSpec((rows, w), lambda i, cb=off // w: (0, cb)))
        mu_specs.append(pl.BlockSpec((None, 1, w), lambda i, mb=off // w: (layer, 0, mb)))
    out_spec = pl.BlockSpec((rows, RWKV_WIDTH), lambda i: (0, 0))
    mu3 = prm["mu"].reshape(prm["mu"].shape[0], 1, RWKV_PROJ)
    return pl.pallas_call(
        functools.partial(_rwkv_pre_sample_kernel, precise=precise),
        grid=(1,),
        in_specs=cur_specs + cur_specs + mu_specs + _rwkv_param_specs(layer),
        out_specs=[out_spec] * 7,
        out_shape=[jax.ShapeDtypeStruct((rows, RWKV_WIDTH), F32)] * 7,
        compiler_params=_cp("arbitrary"), name="rwkv_pre_sample",
    )(*([pc] * 5), *([pc_shifted] * 5), *([mu3] * 5), *_rwkv_params(prm))


def _split_heads(x):
    return jnp.stack([x[:, h * RWKV_HEAD:(h + 1) * RWKV_HEAD] for h in range(RWKV_HEADS)], axis=0)


def _join_heads(x):
    return jnp.concatenate([x[h] for h in range(RWKV_HEADS)], axis=1)


def _wkv_steps_kernel(r_ref, ld_ref, k_ref, v_ref, a_ref, b_ref, s0_ref, y_ref, sf_ref, *, steps, bb):
    n = RWKV_HEAD
    eye = lax.broadcasted_iota(jnp.int32, (n, n), 0) == lax.broadcasted_iota(jnp.int32, (n, n), 1)
    for i in range(bb):
        s = s0_ref[i]
        seqs = [_split_heads(ref[i]) for ref in (r_ref, ld_ref, k_ref, v_ref, a_ref, b_ref)]
        out_rows = []
        for t in range(steps):
            r, ld, k, v, a, b = [x[:, t:t + 1, :] for x in seqs]
            sa = jnp.sum(s * a, axis=-1, keepdims=True)
            vcol = jnp.sum(jnp.where(eye, v, 0.0), axis=-1, keepdims=True)
            s = s * jnp.exp(ld) + sa * b + vcol * k
            ycol = jnp.sum(s * r, axis=-1, keepdims=True)
            out_rows.append(jnp.sum(jnp.where(eye, ycol, 0.0), axis=1, keepdims=True))
        y_ref[i] = _join_heads(jnp.concatenate(out_rows, axis=1))
        sf_ref[i] = s


def _wkv_steps(seqs, s0, *, bb=4):
    batch, t, _ = seqs[0].shape
    seq_spec = pl.BlockSpec((bb, t, RWKV_WIDTH), lambda i: (i, 0, 0))
    st_spec = pl.BlockSpec((bb, RWKV_HEADS, RWKV_HEAD, RWKV_HEAD), lambda i: (i, 0, 0, 0))
    return pl.pallas_call(
        functools.partial(_wkv_steps_kernel, steps=t, bb=bb),
        grid=(batch // bb,),
        in_specs=[seq_spec] * 6 + [st_spec],
        out_specs=[seq_spec, st_spec],
        out_shape=[jax.ShapeDtypeStruct((batch, t, RWKV_WIDTH), F32), jax.ShapeDtypeStruct(s0.shape, F32)],
        compiler_params=_cp("parallel"), name="wkv_steps",
    )(*seqs, s0)


WKV_CHUNK = 64


def _e_nt(a, b):
    return jnp.einsum("hqd,hkd->hqk", a, b, preferred_element_type=F32)


def _e_nn(a, b):
    return jnp.einsum("hqk,hkd->hqd", a, b, preferred_element_type=F32)


def _wkv_chunk_kernel(r_ref, ld_ref, k_ref, v_ref, a_ref, b_ref, y_ref, sf_ref, s_scr):
    c = WKV_CHUNK

    @pl.when(pl.program_id(1) == 0)
    def _():
        s_scr[...] = jnp.zeros_like(s_scr)

    ld = ld_ref[...]
    cum = ld
    k = 1
    while k < c:
        cum = cum + _shift_rows(cum, k)
        k *= 2
    e_pos, e_prev, e_neg = jnp.exp(cum), jnp.exp(cum - ld), jnp.exp(-cum)
    at = _split_heads(a_ref[...] * e_prev)
    rt = _split_heads(r_ref[...] * e_pos)
    bt = _split_heads(b_ref[...] * e_neg)
    kt = _split_heads(k_ref[...] * e_neg)
    v = _split_heads(v_ref[...])
    lam = _split_heads(e_pos[c - 1:c, :])

    ti = lax.broadcasted_iota(jnp.int32, (c, c), 0)
    si = lax.broadcasted_iota(jnp.int32, (c, c), 1)
    strict = ti > si

    def blockmask(size):
        same = (ti // size) == (si // size)
        return strict & same & ((ti // (size // 2)) != (si // (size // 2)))

    def run(precise):
        dot_nt = functools.partial(_mxu, _e_nt, precise=precise)
        dot_nn = functools.partial(_mxu, _e_nn, precise=precise)

        def dot_tn(x, y):
            return dot_nn(jnp.swapaxes(x, 1, 2), y)

        ar = jnp.concatenate([at, rt], axis=1)
        g_b = dot_nt(ar, bt)
        g_k = dot_nt(ar, kt)
        n_ab = jnp.where(strict, g_b[:, :c, :], 0.0)
        n_ak = jnp.where(strict, g_k[:, :c, :], 0.0)
        m_rb = jnp.where(ti >= si, g_b[:, c:, :], 0.0)
        m_rk = jnp.where(ti >= si, g_k[:, c:, :], 0.0)

        base = 8
        n8 = jnp.where((ti // base) == (si // base), n_ab, 0.0)
        eye = (ti == si).astype(F32)
        n8_2 = dot_nn(n8, n8)
        n8_4 = dot_nn(n8_2, n8_2)
        t_inv = eye + n8
        t_inv = t_inv + dot_nn(t_inv, n8_2)
        t_inv = t_inv + dot_nn(t_inv, n8_4)
        size = 2 * base
        while size <= c:
            off = jnp.where(blockmask(size), n_ab, 0.0)
            t_inv = t_inv + dot_nn(dot_nn(t_inv, off), t_inv)
            size *= 2

        wv = dot_nn(n_ak, v)
        a_bar = dot_nn(t_inv, at)
        u_bar = dot_nn(t_inv, wv)
        r_bar = rt + dot_nn(m_rb, a_bar)
        y_bar = dot_nn(m_rb, u_bar) + dot_nn(m_rk, v)
        phi = dot_tn(a_bar, bt)
        psi = dot_tn(jnp.concatenate([u_bar, v], axis=1), jnp.concatenate([bt, kt], axis=1))

        s0 = s_scr[...]
        y_ref[...] = _join_heads(dot_nt(r_bar, s0) + y_bar)
        s_scr[...] = (s0 + dot_nn(s0, phi) + psi) * lam

    last = pl.program_id(1) >= pl.num_programs(1) - PRECISE_TAIL // c

    @pl.when(last)
    def _():
        run(True)

    @pl.when(jnp.logical_not(last))
    def _():
        run(False)

    @pl.when(pl.program_id(1) == pl.num_programs(1) - 1)
    def _():
        sf_ref[...] = s_scr[...]


def _wkv_chunked(seqs):
    batch, t, _ = seqs[0].shape
    c = WKV_CHUNK
    seq_spec = pl.BlockSpec((None, c, RWKV_WIDTH), lambda b, i: (b, i, 0))
    st_shape = (batch, RWKV_HEADS, RWKV_HEAD, RWKV_HEAD)
    st_spec = pl.BlockSpec((None,) + st_shape[1:], lambda b, i: (b, 0, 0, 0))
    return pl.pallas_call(
        _wkv_chunk_kernel,
        grid=(batch, t // c),
        in_specs=[seq_spec] * 6,
        out_specs=[seq_spec, st_spec],
        out_shape=[jax.ShapeDtypeStruct((batch, t, RWKV_WIDTH), F32), jax.ShapeDtypeStruct(st_shape, F32)],
        scratch_shapes=[pltpu.VMEM(st_shape[1:], F32)],
        compiler_params=_cp("parallel", "arbitrary"), name="wkv_chunk",
    )(*seqs)


def _rwkv_post_kernel(y_ref, r_ref, k_ref, v_ref, g_ref, rk_ref, lg_ref, lb_ref, o_ref):
    y = y_ref[...]
    inv = 1.0 / RWKV_HEAD
    mean = _head_sum(y) * inv
    yc = y - mean
    var = _head_sum(yc * yc) * inv
    yn = yc * lax.rsqrt(var + RWKV_LN_EPS) * lg_ref[...] + lb_ref[...]
    v = v_ref[...]
    bonus = _head_sum(r_ref[...] * k_ref[...] * rk_ref[...]) * v
    o_ref[...] = ((yn + bonus) * g_ref[...]).astype(o_ref.dtype)


def _rwkv_post(y, r, k, v, g, prm, layer, precise, *, tm):
    rows = y.shape[0]
    depth = prm["rk"].shape[0]
    row = pl.BlockSpec((tm, RWKV_WIDTH), lambda i: (i, 0))
    vec = pl.BlockSpec((None, 1, RWKV_WIDTH), lambda i: (layer, 0, 0))
    r3 = lambda a: a.reshape(depth, 1, RWKV_WIDTH)
    return pl.pallas_call(
        _rwkv_post_kernel,
        grid=(rows // tm,),
        in_specs=[row] * 5 + [vec] * 3,
        out_specs=row,
        out_shape=jax.ShapeDtypeStruct((rows, RWKV_WIDTH), _act_dtype(precise)),
        compiler_params=_cp("parallel"), name="rwkv_post",
    )(y, r, k, v, g, r3(prm["rk"]), r3(prm["ln_g"]), r3(prm["ln_b"]))


MOE_TM = 256


def _dispatch_plan(e_idx, tm):
    n = e_idx.shape[0]
    pairs = n * TOP_K
    e_flat = e_idx.reshape(pairs)
    onehot = (e_flat[:, None] == jnp.arange(N_EXPERTS, dtype=jnp.int32)[None, :]).astype(jnp.int32)
    csum = jnp.cumsum(onehot, axis=0)
    counts = csum[-1]
    padded = ((counts + tm - 1) // tm) * tm
    ends = jnp.cumsum(padded)
    starts = ends - padded
    pos = jnp.sum(onehot * (csum - 1 + starts[None, :]), axis=1)
    nt = (pairs + N_EXPERTS * (tm - 1)) // tm
    tile_start = jnp.arange(nt, dtype=jnp.int32) * tm
    tile_e = jnp.minimum(jnp.sum((tile_start[:, None] >= ends[None, :]).astype(jnp.int32), axis=1), N_EXPERTS - 1)
    tile_nv = jnp.clip(jnp.take(starts + counts, tile_e) - tile_start, 0, tm)
    src = jnp.zeros((nt * tm,), jnp.int32).at[pos].set(jnp.arange(pairs, dtype=jnp.int32) // TOP_K)
    return jnp.stack([tile_e, tile_nv]).astype(jnp.int32), src.reshape(nt, 1, tm), pos


def _moe_gather_kernel(meta_ref, src_ref, h_hbm, o_ref, sem):
    nv = meta_ref[1, pl.program_id(0)]
    tm = o_ref.shape[0]

    @pl.when(nv > 0)
    def _():
        def issue(r, c):
            pltpu.make_async_copy(h_hbm.at[pl.ds(src_ref[0, r], 1)], o_ref.at[pl.ds(r, 1)], sem).start()
            return c

        lax.fori_loop(0, tm, issue, 0, unroll=8)
        pltpu.make_async_copy(h_hbm.at[pl.ds(0, tm)], o_ref, sem).wait()

    rows = lax.broadcasted_iota(jnp.int32, o_ref.shape, 0)
    o_ref[...] = jnp.where(rows < nv, o_ref[...], 0.0)


def _moe_gather(meta, src, h):
    nt, _, tm = src.shape
    d = h.shape[1]
    gs = pltpu.PrefetchScalarGridSpec(
        num_scalar_prefetch=1, grid=(nt,),
        in_specs=[pl.BlockSpec((None, 1, tm), lambda j, m: (j, 0, 0), memory_space=pltpu.SMEM),
                  pl.BlockSpec(memory_space=pl.ANY)],
        out_specs=pl.BlockSpec((tm, d), lambda j, m: (j, 0)),
        scratch_shapes=[pltpu.SemaphoreType.DMA(())])
    return pl.pallas_call(
        _moe_gather_kernel, grid_spec=gs, out_shape=jax.ShapeDtypeStruct((nt * tm, d), F32),
        compiler_params=_cp("arbitrary"), name="moe_gather",
    )(meta, src, h)


def _moe_experts_kernel(meta_ref, xs_ref, wg_ref, wu_ref, wd_ref, y_ref, wg_bf, wu_bf, wd_bf):
    j = pl.program_id(0)
    e = meta_ref[0, j]
    nv = meta_ref[1, j]
    new_expert = (j == 0) | (e != meta_ref[0, jnp.maximum(j - 1, 0)])

    @pl.when(new_expert)
    def _():
        wg_bf[...] = wg_ref[...].astype(BF16)
        wu_bf[...] = wu_ref[...].astype(BF16)
        wd_bf[...] = wd_ref[...].astype(BF16)

    @pl.when(nv > 0)
    def _():
        x = xs_ref[...].astype(BF16)
        gate = _dot2(x, wg_bf[...])
        act = gate * _sigmoid(gate) * _dot2(x, wu_bf[...])
        y_ref[...] = _dot2(act.astype(BF16), wd_bf[...])

    @pl.when(nv == 0)
    def _():
        y_ref[...] = jnp.zeros_like(y_ref)


def _moe_experts(meta, xs, w_gate, w_up, w_down, layer):
    rows, d = xs.shape
    nt = meta.shape[1]
    tm = rows // nt
    once = dict(pipeline_mode=pl.Buffered(1))
    gs = pltpu.PrefetchScalarGridSpec(
        num_scalar_prefetch=1, grid=(nt,),
        in_specs=[pl.BlockSpec((tm, d), lambda j, m: (j, 0)),
                  pl.BlockSpec((None, None, d, D_EXPERT), lambda j, m: (layer, m[0, j], 0, 0), **once),
                  pl.BlockSpec((None, None, d, D_EXPERT), lambda j, m: (layer, m[0, j], 0, 0), **once),
                  pl.BlockSpec((None, None, D_EXPERT, d), lambda j, m: (layer, m[0, j], 0, 0), **once)],
        out_specs=pl.BlockSpec((tm, d), lambda j, m: (j, 0)),
        scratch_shapes=[pltpu.VMEM((d, D_EXPERT), BF16), pltpu.VMEM((d, D_EXPERT), BF16),
                        pltpu.VMEM((D_EXPERT, d), BF16)])
    return pl.pallas_call(
        _moe_experts_kernel, grid_spec=gs, out_shape=jax.ShapeDtypeStruct((rows, d), F32),
        compiler_params=_cp("arbitrary"), name="moe_experts",
    )(meta, xs, w_gate, w_up, w_down)


def _moe_combine_kernel(pos_ref, w_ref, y_hbm, m_ref, buf, sem):
    tt = m_ref.shape[0]

    def issue(r, c):
        pltpu.make_async_copy(y_hbm.at[pl.ds(pos_ref[0, r], 1)], buf.at[pl.ds(r, 1)], sem).start()
        return c

    lax.fori_loop(0, TOP_K * tt, issue, 0, unroll=8)
    pltpu.make_async_copy(y_hbm.at[pl.ds(0, TOP_K * tt)], buf, sem).wait()
    w = w_ref[...]
    m_ref[...] = w[:, 0:1] * buf[0:tt, :] + w[:, 1:2] * buf[tt:2 * tt, :]


def _moe_combine(pos, wts, y, *, tt):
    n = wts.shape[0]
    d = y.shape[1]
    pos_t = jnp.swapaxes(pos.reshape(n // tt, tt, TOP_K), 1, 2).reshape(n // tt, 1, TOP_K * tt)
    return pl.pallas_call(
        _moe_combine_kernel,
        grid=(n // tt,),
        in_specs=[pl.BlockSpec((None, 1, TOP_K * tt), lambda i: (i, 0, 0), memory_space=pltpu.SMEM),
                  pl.BlockSpec((tt, TOP_K), lambda i: (i, 0)),
                  pl.BlockSpec(memory_space=pl.ANY)],
        out_specs=pl.BlockSpec((tt, d), lambda i: (i, 0)),
        out_shape=jax.ShapeDtypeStruct((n, d), F32),
        scratch_shapes=[pltpu.VMEM((TOP_K * tt, d), F32), pltpu.SemaphoreType.DMA(())],
        compiler_params=_cp("arbitrary"), name="moe_combine",
    )(pos_t, wts, y)


def _moe(h_all, e_idx, wts, w_gate, w_up, w_down, layer):
    n = h_all.shape[0]
    meta, src, pos = _dispatch_plan(e_idx, MOE_TM)
    xs = _moe_gather(meta, src, h_all)
    y = _moe_experts(meta, xs, w_gate, w_up, w_down, layer)
    tt = 256
    while n % tt:
        tt //= 2
    return _moe_combine(pos, wts, y, tt=tt)


_SH1, _SC1, _GT1, _SH2, _SC2, _GT2 = range(6)


def kernel(x_prompt, x_sample, cache_swa_k, cache_swa_v, state_pool, state_rwkv_shift, state_rwkv_wkv, state_conv, c_prompt, c_sample, w_ada, b_ada, g_norm1, g_norm2, w_in, sinks, w_pool, ls_pool, rwkv_mu, rwkv_w0, rwkv_w2, rwkv_a0, rwkv_a2, rwkv_g2, rwkv_kk, rwkv_ka, rwkv_rk, rwkv_ln_g, rwkv_ln_b, conv_w, w_branch, w_out, w_router, b_router, w_gate, w_up, w_down, g_final):
    depth = w_in.shape[0]
    batch, seq, d = x_prompt.shape
    dbatch, dseq, _ = x_sample.shape
    wbuf = cache_swa_k.shape[2]
    np_rows, ns_rows = batch * seq, dbatch * dseq

    grp_p = _Group(np_rows, seq, 1)
    grp_s = _Group(ns_rows, ns_rows, ns_rows)
    assert seq <= 2048, "the big matmuls take one whole prompt sequence per row tile"
    tm_p, tm_s = seq, ns_rows
    te_p = min(512, seq)

    c_all = jnp.concatenate([c_prompt, c_sample], axis=0)
    wr_pad = jnp.pad(w_router, ((0, 0), (0, ROUTER_PAD - N_EXPERTS)))
    prm = dict(mu=rwkv_mu, w0=rwkv_w0, w2=rwkv_w2, a0=rwkv_a0, a2=rwkv_a2, g2=rwkv_g2, kk=rwkv_kk, ka=rwkv_ka,
               rk=rwkv_rk.reshape(depth, RWKV_WIDTH), ln_g=rwkv_ln_g, ln_b=rwkv_ln_b)
    cache_k = cache_swa_k.reshape(depth, dbatch, wbuf, KV_WIDTH)
    cache_v = cache_swa_v.reshape(depth, dbatch, wbuf, KV_WIDTH)

    xp = x_prompt.reshape(np_rows, d)
    xs = x_sample.reshape(ns_rows, d)
    st_p, st_s = [], []
    pend_p = pend_s = None
    for l in range(depth):
        precise = l == 0
        hd = (BF16, F32) if precise else (BF16,)
        mod = _ada(c_all, w_ada, b_ada, l, precise)
        mod_p = mod[:batch].reshape(batch, 1, 6 * d)
        mod_s = jnp.repeat(mod[batch:], dseq, axis=0).reshape(1, ns_rows, 6 * d)

        def first_norm(grp, x, pend, mod3, tm):
            if pend is None:
                return (x,) + tuple(_normx(grp, x, g_norm1[l], mod=(mod3, _SC1, _SH1), out_dtype=hd, tm=tm))
            return tuple(_normx(grp, x, g_norm1[l], add=pend + (_GT2,), mod=(mod3, _SC1, _SH1), emit_x=True,
                                out_dtype=hd, tm=tm))

        xp, hp, *hp32 = first_norm(grp_p, xp, pend_p, mod_p, te_p)
        xs, hs, *hs32 = first_norm(grp_s, xs, pend_s, mod_s, tm_s)

        def tails(tm, grp):
            if not precise:
                return None
            return (tm, 1) if grp is grp_s else (min(PRECISE_TAIL, tm), seq // tm)

        tail_p, tail_s = (min(PRECISE_TAIL, tm_p), tm_s) if precise else (0, 0)
        pp = _mm(hp, w_in, l, tm=tm_p, tn=512, a_f32=hp32[0] if precise else None, tail_rows=tail_p)
        ps = _mm(hs, w_in, l, tm=tm_s, tn=512, a_f32=hs32[0] if precise else None, tail_rows=tail_s)

        a_p = _swa_prompt(pp, sinks, l, batch, seq, precise)
        b_p = _pool_prompt(pp, w_pool, ls_pool, l, batch, seq, precise)
        d_p, conv_new_p = _conv_prompt(pp, conv_w, l, batch, seq, precise)
        pre_p = _rwkv_pre_prompt(pp, prm, l, batch, seq, precise, tt=te_p)
        r_p, w_p, k_p, v_p, ka_p, kb_p, g_p = pre_p
        as3 = lambda t: t.reshape(batch, seq, RWKV_WIDTH)
        y_p, wkv_new_p = _wkv_chunked([as3(t) for t in (r_p, w_p, k_p, v_p, ka_p, kb_p)])
        c_p = _rwkv_post(y_p.reshape(np_rows, RWKV_WIDTH), r_p, k_p, v_p, g_p, prm, l, precise, tm=te_p)
        pp3 = pp.reshape(batch, seq, IN_WIDTH)
        kw = min(WINDOW, seq)
        st_p.append((pp3[:, seq - kw:, OFF_K:OFF_K + KV_WIDTH].reshape(batch, kw, N_KV_HEADS, HEAD_DIM),
                     pp3[:, seq - kw:, OFF_V:OFF_V + KV_WIDTH].reshape(batch, kw, N_KV_HEADS, HEAD_DIM),
                     pp3[:, seq - POOL_BUF:, OFF_U:OFF_U + POOL_WIDTH],
                     pp3[:, seq - 1, OFF_PC:OFF_PC + RWKV_PROJ],
                     wkv_new_p, conv_new_p))

        ps3 = ps.reshape(dbatch, dseq, IN_WIDTH)
        a_s, k_new_s, v_new_s = _swa_sample(ps3, cache_k, cache_v, sinks, l, precise)
        tmaj = lambda t: jnp.swapaxes(t, 0, 1)
        b_s_t, pool_new_t = _pool_sample(tmaj(ps3[:, :, OFF_U:OFF_U + POOL_WIDTH]), tmaj(state_pool[l]),
                                         w_pool, ls_pool, l, PAST_LEN, precise)
        d_s_t, conv_new_t = _conv_sample(tmaj(ps3[:, :, OFF_CB:OFF_CB + CONV_WIDTH]),
                                         tmaj(ps3[:, :, OFF_CC:OFF_CC + CONV_WIDTH]),
                                         tmaj(ps3[:, :, OFF_CX:OFF_CX + CONV_WIDTH]), tmaj(state_conv[l]), conv_w, l,
                                         precise)
        pc_s3 = ps3[:, :, OFF_PC:OFF_PC + RWKV_PROJ]
        pc_shift = jnp.concatenate([state_rwkv_shift[l][:, None, :], pc_s3[:, :-1, :]], axis=1)
        pre_s = _rwkv_pre_sample(pc_s3.reshape(ns_rows, RWKV_PROJ), pc_shift.reshape(ns_rows, RWKV_PROJ), prm, l,
                                 precise)
        r_s, w_s, k_s, v_s, ka_s, kb_s, g_s = pre_s
        as3s = lambda t: t.reshape(dbatch, dseq, RWKV_WIDTH)
        y_s, wkv_new_s = _wkv_steps([as3s(t) for t in (r_s, w_s, k_s, v_s, ka_s, kb_s)], state_rwkv_wkv[l])
        c_s = _rwkv_post(y_s.reshape(ns_rows, RWKV_WIDTH), r_s, k_s, v_s, g_s, prm, l, precise, tm=ns_rows)
        st_s.append((k_new_s.reshape(dbatch, wbuf, N_KV_HEADS, HEAD_DIM),
                     v_new_s.reshape(dbatch, wbuf, N_KV_HEADS, HEAD_DIM),
                     tmaj(pool_new_t), pc_s3[:, dseq - 1, :], wkv_new_s, tmaj(conv_new_t)))
        br_s = (a_s.reshape(ns_rows, ATTN_WIDTH), tmaj(b_s_t).reshape(ns_rows, POOL_WIDTH), c_s,
                tmaj(d_s_t).reshape(ns_rows, CONV_WIDTH))

        def tail(grp, x, branches, p, mod3, tm_mm, tm_el):
            if precise:
                tm_t = min(tm_mm, 512)
                merged32, merged = _merge(branches, p, w_branch, l, tm=tm_t, tn=512, tail=tails(tm_t, grp))
                x1 = _mm_res(grp, merged, w_out, l, x, mod3, _GT1, tm=tm_mm, tn=512, a_f32=merged32,
                             tail_rows=tail_s if grp is grp_s else tail_p)
            else:
                merged, = _merge(branches, p, w_branch, l, tm=min(tm_mm, 1024), tn=512, tail=None)
                x1 = _mm_res(grp, merged, w_out, l, x, mod3, _GT1, tm=tm_mm, tn=512)
            return (x1,) + tuple(_normx(grp, x1, g_norm2[l], mod=(mod3, _SC2, _SH2), route=(wr_pad, b_router),
                                        out_dtype=F32, tm=tm_el))

        xp, h2_p, e_p, wt_p = tail(grp_p, xp, (a_p, b_p, c_p, d_p), pp, mod_p, tm_p, te_p)
        xs, h2_s, e_s, wt_s = tail(grp_s, xs, br_s, ps, mod_s, tm_s, tm_s)
        m_all = _moe(jnp.concatenate([h2_p, h2_s]), jnp.concatenate([e_p, e_s]), jnp.concatenate([wt_p, wt_s]),
                     w_gate, w_up, w_down, l)
        pend_p, pend_s = (m_all, 0, mod_p), (m_all, np_rows, mod_s)

    y_p = _normx(grp_p, xp, g_final, add=pend_p + (_GT2,), out_dtype=F32, tm=te_p)[0]
    y_s = _normx(grp_s, xs, g_final, add=pend_s + (_GT2,), out_dtype=F32, tm=tm_s)[0]

    def stack(states, i):
        return jnp.stack([s[i] for s in states])

    return ((y_p.reshape(batch, seq, d), y_s.reshape(dbatch, dseq, d))
            + tuple(stack(st_p, i) for i in range(6)) + tuple(stack(st_s, i) for i in range(6)))
```

```python
import functools

import jax
import jax.numpy as jnp
from jax import lax
from jax.experimental import pallas as pl
from jax.experimental.pallas import tpu as pltpu

F32 = jnp.float32
BF16 = jnp.bfloat16

D_MODEL = 2048
PAST_LEN = 8192
WINDOW = 128
HEAD_DIM = 64
N_HEADS = 16
N_KV_HEADS = 4
GQA_GROUP = N_HEADS // N_KV_HEADS
ATTN_WIDTH = N_HEADS * HEAD_DIM
KV_WIDTH = N_KV_HEADS * HEAD_DIM
ATTN_SCALE = HEAD_DIM ** -0.5
NEG_INF = -1e30
POOL_WINDOWS = (2, 4, 8, 16)
POOL_WIDTH = 768
POOL_GW = POOL_WIDTH // len(POOL_WINDOWS)
POOL_BUF = max(POOL_WINDOWS) - 1
RWKV_HEAD = 64
RWKV_WIDTH = 768
RWKV_HEADS = RWKV_WIDTH // RWKV_HEAD
DECAY_LORA = 64
AAA_LORA = 64
GATE_LORA = 128
RWKV_PROJ = 3 * RWKV_WIDTH + DECAY_LORA + AAA_LORA + GATE_LORA
RWKV_LN_EPS = 64e-5
CONV_WIDTH = 768
CONV_K = 3
N_BRANCH = 4
BRANCH_SECTIONS = (ATTN_WIDTH, POOL_WIDTH, RWKV_WIDTH, CONV_WIDTH)
MIX_WIDTH = sum(BRANCH_SECTIONS)
N_EXPERTS = 16
N_GROUPS = 4
EXP_PER_GROUP = N_EXPERTS // N_GROUPS
TOP_K = 2
D_EXPERT = 1024
RMS_EPS = 1e-6

OFF_Q = 0
OFF_K = OFF_Q + ATTN_WIDTH
OFF_V = OFF_K + KV_WIDTH
OFF_U = OFF_V + KV_WIDTH
OFF_PC = OFF_U + POOL_WIDTH
OFF_CB = OFF_PC + RWKV_PROJ
OFF_CC = OFF_CB + CONV_WIDTH
OFF_CX = OFF_CC + CONV_WIDTH
OFF_GL = OFF_CX + CONV_WIDTH
IN_WIDTH = OFF_GL + N_BRANCH * D_MODEL

PRECISE_TAIL = 2 * WINDOW
LANES = 128
ROUTER_PAD = LANES
VMEM_LIMIT = 60 * 1024 * 1024


def _cp(*sem):
    return pltpu.CompilerParams(dimension_semantics=sem, vmem_limit_bytes=VMEM_LIMIT)


def _sigmoid(x):
    return 1.0 / (1.0 + jnp.exp(-x))


def _split(x):
    hi = x.astype(BF16)
    return hi, (x.astype(F32) - hi.astype(F32)).astype(BF16)


def _mxu(contract, a, b, precise):
    if not precise:
        return contract(a.astype(BF16), b.astype(BF16))
    ah, al = _split(a)
    bh, bl = _split(b)
    return contract(ah, bh) + (contract(ah, bl) + contract(al, bh))


def _dot2(a, b):
    return jnp.dot(a, b, preferred_element_type=F32)


def _bdot(a, b, precise=False):
    return _mxu(_dot2, a, b, precise)


def _act_dtype(precise):
    return F32 if precise else BF16


def _ada_kernel(c_ref, w_ref, b_ref, o_ref, *, precise):
    c = c_ref[...]
    o_ref[...] = _bdot(c * _sigmoid(c), w_ref[...], precise) + b_ref[...]


def _ada(c_all, w_ada, b_ada, layer, precise):
    depth, d, n = w_ada.shape
    nb = c_all.shape[0]
    tn = 1024
    return pl.pallas_call(
        functools.partial(_ada_kernel, precise=precise),
        grid=(n // tn,),
        in_specs=[pl.BlockSpec((nb, d), lambda j: (0, 0)),
                  pl.BlockSpec((None, d, tn), lambda j: (layer, 0, j)),
                  pl.BlockSpec((None, 1, tn), lambda j: (layer, 0, j))],
        out_specs=pl.BlockSpec((nb, tn), lambda j: (0, j)),
        out_shape=jax.ShapeDtypeStruct((nb, n), F32),
        compiler_params=_cp("parallel"),
        name="ada",
    )(c_all, w_ada, b_ada.reshape(depth, 1, n))


class _Group:
    def __init__(self, rows, rpm, mod_rows):
        self.rows, self.rpm, self.mod_rows = rows, rpm, mod_rows

    def mod_spec(self, tm, width, col_of, row_axis=0):
        if self.mod_rows == 1:
            per = self.rpm // tm
            return pl.BlockSpec((None, 1, width), lambda *idx: (idx[row_axis] // per, 0, col_of(idx)))
        assert tm == self.rpm == self.mod_rows
        return pl.BlockSpec((None, tm, width), lambda *idx: (idx[row_axis], 0, col_of(idx)))


def _route(y, wr, br):
    y_hi = y.astype(BF16)
    y_lo = (y - y_hi.astype(F32)).astype(BF16)
    w_hi = wr.astype(BF16)
    w_lo = (wr - w_hi.astype(F32)).astype(BF16)
    logits = (jnp.dot(y_hi, w_hi, preferred_element_type=F32)
              + (jnp.dot(y_hi, w_lo, preferred_element_type=F32) + jnp.dot(y_lo, w_hi, preferred_element_type=F32)))
    logits = logits[:, :N_EXPERTS] + br
    tm = logits.shape[0]
    e = jnp.exp(logits - jnp.max(logits, axis=-1, keepdims=True))
    probs = e / jnp.sum(e, axis=-1, keepdims=True)
    iota_g = lax.broadcasted_iota(jnp.int32, (tm, EXP_PER_GROUP), 1)
    best = None
    for g in range(N_GROUPS):
        pg = probs[:, g * EXP_PER_GROUP:(g + 1) * EXP_PER_GROUP]
        m1 = jnp.max(pg, axis=-1, keepdims=True)
        i1 = jnp.min(jnp.where(pg == m1, iota_g, EXP_PER_GROUP), axis=-1, keepdims=True)
        rest = jnp.where(iota_g == i1, -1.0, pg)
        m2 = jnp.max(rest, axis=-1, keepdims=True)
        i2 = jnp.min(jnp.where(rest == m2, iota_g, EXP_PER_GROUP), axis=-1, keepdims=True)
        cand = (m1 + m2, m1, m2, i1 + g * EXP_PER_GROUP, i2 + g * EXP_PER_GROUP)
        if best is None:
            best = cand
        else:
            take = cand[0] > best[0]
            best = tuple(jnp.where(take, c, b) for c, b in zip(cand, best))
    _, m1, m2, e1, e2 = best
    den = m1 + m2
    slot = lax.broadcasted_iota(jnp.int32, (tm, TOP_K), 1)
    return jnp.where(slot == 0, e1, e2), jnp.where(slot == 0, m1 / den, m2 / den)


def _normx_kernel(*refs, has_add, has_mod, has_route, emit_x, n_h):
    it = iter(refs)
    x_ref = next(it)
    if has_add:
        m_ref, gate_ref = next(it), next(it)
    g_ref = next(it)
    if has_mod:
        sc_ref, sh_ref = next(it), next(it)
    if has_route:
        wr_ref, br_ref = next(it), next(it)
    if emit_x:
        xo_ref = next(it)
    h_refs = [next(it) for _ in range(n_h)]
    if has_route:
        eidx_ref, wts_ref = next(it), next(it)
    x = x_ref[...]
    if has_add:
        x = x + gate_ref[...] * m_ref[...]
    if emit_x:
        xo_ref[...] = x
    y = x * lax.rsqrt(jnp.mean(x * x, axis=-1, keepdims=True) + RMS_EPS) * g_ref[...]
    if has_mod:
        y = y * (1.0 + sc_ref[...]) + sh_ref[...]
    for h_ref in h_refs:
        h_ref[...] = y.astype(h_ref.dtype)
    if has_route:
        eidx_ref[...], wts_ref[...] = _route(y, wr_ref[...], br_ref[...])


def _normx(grp, x, g, *, add=None, mod=None, route=None, emit_x=False, out_dtype=BF16, tm=512):
    out_dtypes = out_dtype if isinstance(out_dtype, tuple) else (out_dtype,)
    n, d = x.shape
    row = pl.BlockSpec((tm, d), lambda i: (i, 0))
    vec = pl.BlockSpec((1, d), lambda i: (0, 0))
    args, specs = [x], [row]
    if add is not None:
        m, m_row0, mod3, chunk = add
        args += [m, mod3]
        specs += [pl.BlockSpec((tm, d), lambda i, o=m_row0 // tm: (i + o, 0)),
                  grp.mod_spec(tm, d, lambda idx, c=chunk: c)]
    args.append(g.reshape(1, d))
    specs.append(vec)
    if mod is not None:
        mod3, c_sc, c_sh = mod
        args += [mod3, mod3]
        specs += [grp.mod_spec(tm, d, lambda idx, c=c_sc: c), grp.mod_spec(tm, d, lambda idx, c=c_sh: c)]
    if route is not None:
        wr, br = route
        args += [wr, br.reshape(1, N_EXPERTS)]
        specs += [pl.BlockSpec((d, ROUTER_PAD), lambda i: (0, 0)), pl.BlockSpec((1, N_EXPERTS), lambda i: (0, 0))]
    out_shape, out_specs = [], []
    if emit_x:
        out_shape.append(jax.ShapeDtypeStruct((n, d), F32))
        out_specs.append(row)
    out_shape += [jax.ShapeDtypeStruct((n, d), dt) for dt in out_dtypes]
    out_specs += [row] * len(out_dtypes)
    if route is not None:
        out_shape += [jax.ShapeDtypeStruct((n, TOP_K), jnp.int32), jax.ShapeDtypeStruct((n, TOP_K), F32)]
        out_specs += [pl.BlockSpec((tm, TOP_K), lambda i: (i, 0))] * 2
    return pl.pallas_call(
        functools.partial(_normx_kernel, has_add=add is not None, has_mod=mod is not None,
                          has_route=route is not None, emit_x=emit_x, n_h=len(out_dtypes)),
        grid=(n // tm,), in_specs=specs, out_specs=out_specs, out_shape=out_shape,
        compiler_params=_cp("parallel"), name="normx",
    )(*args)


def _tail_tile(tail):
    return lax.rem(pl.program_id(0) + 1, tail[1]) == 0


def _tail_fix(a, w, tail):
    a_hi, a_lo = _split(a[a.shape[0] - tail[0]:, :])
    w_hi, w_lo = _split(w)
    return _dot2(a_hi, w_lo) + _dot2(a_lo, w_hi)


def _tail_rows(x, rows):
    return x if x.shape[0] == 1 else x[x.shape[0] - rows:, :]


def _tail_spec(tm, k, tail_rows):
    per = tm // tail_rows
    return pl.BlockSpec((tail_rows, k), lambda i, j: ((i + 1) * per - 1, 0))


def _mm_kernel(*refs, tail_rows):
    if tail_rows:
        a_ref, at_ref, w_ref, o_ref = refs
    else:
        a_ref, w_ref, o_ref = refs
    o_ref[...] = _bdot(a_ref[...], w_ref[...])
    if tail_rows:
        lo = o_ref.shape[0] - tail_rows
        o_ref[lo:, :] += _tail_fix(at_ref[...], w_ref[...], (tail_rows, 1))


def _mm(a, w3, layer, *, tm, tn, a_f32=None, tail_rows=0):
    m, k = a.shape
    n = w3.shape[-1]
    tails = [a_f32] if tail_rows else []
    return pl.pallas_call(
        functools.partial(_mm_kernel, tail_rows=tail_rows),
        grid=(m // tm, n // tn),
        in_specs=[pl.BlockSpec((tm, k), lambda i, j: (i, 0))]
        + ([_tail_spec(tm, k, tail_rows)] if tail_rows else [])
        + [pl.BlockSpec((None, k, tn), lambda i, j: (layer, 0, j))],
        out_specs=pl.BlockSpec((tm, tn), lambda i, j: (i, j)),
        out_shape=jax.ShapeDtypeStruct((m, n), F32),
        compiler_params=_cp("parallel", "parallel"), name="mm_in",
    )(a, *tails, w3)


def _mm_res_kernel(*refs, tail_rows):
    if tail_rows:
        a_ref, at_ref, w_ref, x_ref, gate_ref, o_ref = refs
    else:
        a_ref, w_ref, x_ref, gate_ref, o_ref = refs
    o_ref[...] = x_ref[...] + gate_ref[...] * _bdot(a_ref[...], w_ref[...])
    if tail_rows:
        lo = o_ref.shape[0] - tail_rows
        o_ref[lo:, :] += _tail_rows(gate_ref[...], tail_rows) * _tail_fix(at_ref[...], w_ref[...], (tail_rows, 1))


def _mm_res(grp, a, w3, layer, x, mod3, gate_chunk, *, tm, tn, a_f32=None, tail_rows=0):
    m, k = a.shape
    n = w3.shape[-1]
    per_chunk = n // tn
    tails = [a_f32] if tail_rows else []
    return pl.pallas_call(
        functools.partial(_mm_res_kernel, tail_rows=tail_rows),
        grid=(m // tm, n // tn),
        in_specs=[pl.BlockSpec((tm, k), lambda i, j: (i, 0))]
        + ([_tail_spec(tm, k, tail_rows)] if tail_rows else [])
        + [pl.BlockSpec((None, k, tn), lambda i, j: (layer, 0, j)),
           pl.BlockSpec((tm, tn), lambda i, j: (i, j)),
           grp.mod_spec(tm, tn, lambda idx: gate_chunk * per_chunk + idx[1])],
        out_specs=pl.BlockSpec((tm, tn), lambda i, j: (i, j)),
        out_shape=jax.ShapeDtypeStruct((m, n), F32),
        compiler_params=_cp("parallel", "parallel"), name="mm_out",
    )(a, *tails, w3, x, mod3)


def _merge_kernel(a_ref, b_ref, c_ref, d_ref, g0_ref, g1_ref, g2_ref, g3_ref, w_ref, o_ref, *ob_ref, tail):
    parts = []
    lo = 0
    for br_ref, g_ref, width in zip((a_ref, b_ref, c_ref, d_ref), (g0_ref, g1_ref, g2_ref, g3_ref), BRANCH_SECTIONS):
        parts.append((br_ref, g_ref, lo, width))
        lo += width
    acc = None
    for br_ref, g_ref, lo, width in parts:
        t = _sigmoid(g_ref[...]) * _bdot(br_ref[...], w_ref[lo:lo + width, :])
        acc = t if acc is None else acc + t
    o_ref[...] = acc.astype(o_ref.dtype)
    if tail is not None:
        @pl.when(_tail_tile(tail))
        def _():
            row0 = o_ref.shape[0] - tail[0]
            fix = None
            for br_ref, g_ref, lo, width in parts:
                t = _sigmoid(g_ref[row0:, :]) * _tail_fix(br_ref[...], w_ref[lo:lo + width, :], tail)
                fix = t if fix is None else fix + t
            o_ref[row0:, :] += fix

        ob_ref[0][...] = o_ref[...].astype(BF16)


def _merge(branches, p, w_branch, layer, *, tm, tn, tail):
    m = p.shape[0]
    gl_blk = OFF_GL // tn
    per = D_MODEL // tn
    br_specs = [pl.BlockSpec((tm, w), lambda i, j: (i, 0)) for w in BRANCH_SECTIONS]
    gl_specs = [pl.BlockSpec((tm, tn), lambda i, j, b=b: (i, gl_blk + b * per + j)) for b in range(N_BRANCH)]
    return pl.pallas_call(
        functools.partial(_merge_kernel, tail=tail),
        grid=(m // tm, D_MODEL // tn),
        in_specs=br_specs + gl_specs + [pl.BlockSpec((None, MIX_WIDTH, tn), lambda i, j: (layer, 0, j))],
        out_specs=[pl.BlockSpec((tm, tn), lambda i, j: (i, j))] * (1 if tail is None else 2),
        out_shape=[jax.ShapeDtypeStruct((m, D_MODEL), BF16)] if tail is None else
        [jax.ShapeDtypeStruct((m, D_MODEL), F32), jax.ShapeDtypeStruct((m, D_MODEL), BF16)],
        compiler_params=_cp("parallel", "parallel"), name="merge",
    )(*branches, p, p, p, p, w_branch)


def _sink_col(sink_ref, layer, kh, rows_per_head):
    return jnp.concatenate([jnp.full((rows_per_head, 1), sink_ref[layer, kh * GQA_GROUP + g], F32)
                            for g in range(GQA_GROUP)], axis=0)


def _dot_nt(a, b):
    return lax.dot_general(a, b, (((1,), (1,)), ((), ())), preferred_element_type=F32)


def _swa_prompt_kernel(sink_ref, q_ref, kc_ref, kp_ref, vc_ref, vp_ref, o_ref, *, layer, precise):
    n = pl.program_id(1)
    w = WINDOW
    q = q_ref[...]
    qi = jnp.bitwise_and(lax.broadcasted_iota(jnp.int32, (GQA_GROUP * w, 2 * w), 0), w - 1)
    sj = lax.broadcasted_iota(jnp.int32, (GQA_GROUP * w, 2 * w), 1)
    valid = (sj >= qi) & (sj <= qi + w) & ((sj >= w) | (n > 0))

    def run(three_pass):
        outs = []
        for kh in range(N_KV_HEADS):
            sl = slice(kh * HEAD_DIM, (kh + 1) * HEAD_DIM)
            k2 = jnp.concatenate([kp_ref[:, sl], kc_ref[:, sl]], axis=0)
            v2 = jnp.concatenate([vp_ref[:, sl], vc_ref[:, sl]], axis=0)
            q4 = jnp.concatenate([q[:, (kh * GQA_GROUP + g) * HEAD_DIM:(kh * GQA_GROUP + g + 1) * HEAD_DIM]
                                  for g in range(GQA_GROUP)], axis=0)
            s = _mxu(_dot_nt, q4, k2, three_pass) * ATTN_SCALE
            s = jnp.where(valid, s, NEG_INF)
            sk = _sink_col(sink_ref, layer, kh, w)
            m = jnp.maximum(jnp.max(s, axis=-1, keepdims=True), sk)
            e = jnp.exp(s - m)
            p = e / (jnp.sum(e, axis=-1, keepdims=True) + jnp.exp(sk - m))
            o4 = _bdot(p, v2, three_pass)
            outs += [o4[g * w:(g + 1) * w] for g in range(GQA_GROUP)]
        o_ref[...] = jnp.concatenate(outs, axis=1).astype(o_ref.dtype)

    if not precise:
        run(False)
    else:
        last = n >= pl.num_programs(1) - PRECISE_TAIL // w

        @pl.when(last)
        def _():
            run(True)

        @pl.when(jnp.logical_not(last))
        def _():
            run(False)


def _swa_prompt(p, sinks, layer, batch, seq, precise):
    nb = seq // WINDOW
    kblk, vblk = OFF_K // KV_WIDTH, OFF_V // KV_WIDTH

    def cur(col):
        return lambda b, n: (b * nb + n, col)

    def prev(col):
        return lambda b, n: (b * nb + jnp.maximum(n - 1, 0), col)

    return pl.pallas_call(
        functools.partial(_swa_prompt_kernel, layer=layer, precise=precise),
        grid=(batch, nb),
        in_specs=[pl.BlockSpec(memory_space=pltpu.SMEM),
                  pl.BlockSpec((WINDOW, ATTN_WIDTH), cur(0)),
                  pl.BlockSpec((WINDOW, KV_WIDTH), cur(kblk)), pl.BlockSpec((WINDOW, KV_WIDTH), prev(kblk)),
                  pl.BlockSpec((WINDOW, KV_WIDTH), cur(vblk)), pl.BlockSpec((WINDOW, KV_WIDTH), prev(vblk))],
        out_specs=pl.BlockSpec((WINDOW, ATTN_WIDTH), lambda b, n: (b * nb + n, 0)),
        out_shape=jax.ShapeDtypeStruct((batch * seq, ATTN_WIDTH), _act_dtype(precise)),
        compiler_params=_cp("parallel", "parallel"), name="swa_prompt",
    )(sinks, p, p, p, p, p)


def _qk(a, b):
    return jnp.einsum("bqd,bkd->bqk", a, b, preferred_element_type=F32)


def _pv(a, b):
    return jnp.einsum("bqk,bkd->bqd", a, b, preferred_element_type=F32)


def _swa_sample_kernel(sink_ref, q_ref, kn_ref, vn_ref, kc_ref, vc_ref, o_ref, ko_ref, vo_ref, *, layer, steps, wbuf,
                       precise):
    q = q_ref[...]
    kn, vn = kn_ref[...], vn_ref[...]
    kc, vc = kc_ref[...], vc_ref[...]
    ko_ref[:, :wbuf - steps, :] = kc[:, steps:, :]
    ko_ref[:, wbuf - steps:, :] = kn
    vo_ref[:, :wbuf - steps, :] = vc[:, steps:, :]
    vo_ref[:, wbuf - steps:, :] = vn
    bb = q.shape[0]
    rows = GQA_GROUP * steps
    t_c = lax.rem(lax.broadcasted_iota(jnp.int32, (bb, rows, wbuf), 1), steps)
    j_c = lax.broadcasted_iota(jnp.int32, (bb, rows, wbuf), 2)
    dist_c = t_c + wbuf - j_c
    valid_c = (dist_c >= 0) & (dist_c <= WINDOW)
    t_n = lax.rem(lax.broadcasted_iota(jnp.int32, (bb, rows, steps), 1), steps)
    j_n = lax.broadcasted_iota(jnp.int32, (bb, rows, steps), 2)
    valid_n = (t_n - j_n >= 0) & (t_n - j_n <= WINDOW)
    outs = [None] * N_HEADS
    for kh in range(N_KV_HEADS):
        sl = slice(kh * HEAD_DIM, (kh + 1) * HEAD_DIM)
        qg = jnp.concatenate([q[:, :, (kh * GQA_GROUP + g) * HEAD_DIM:(kh * GQA_GROUP + g + 1) * HEAD_DIM]
                              for g in range(GQA_GROUP)], axis=1)
        s_c = _mxu(_qk, qg, kc[:, :, sl], precise) * ATTN_SCALE
        s_n = _mxu(_qk, qg, kn[:, :, sl], precise) * ATTN_SCALE
        s_c = jnp.where(valid_c, s_c, NEG_INF)
        s_n = jnp.where(valid_n, s_n, NEG_INF)
        sk = _sink_col(sink_ref, layer, kh, steps)[None]
        m = jnp.maximum(jnp.maximum(jnp.max(s_c, axis=-1, keepdims=True), jnp.max(s_n, axis=-1, keepdims=True)), sk)
        e_c, e_n = jnp.exp(s_c - m), jnp.exp(s_n - m)
        den = jnp.sum(e_c, axis=-1, keepdims=True) + jnp.sum(e_n, axis=-1, keepdims=True) + jnp.exp(sk - m)
        o = _mxu(_pv, e_c / den, vc[:, :, sl], precise) + _mxu(_pv, e_n / den, vn[:, :, sl], precise)
        for g in range(GQA_GROUP):
            outs[kh * GQA_GROUP + g] = o[:, g * steps:(g + 1) * steps, :]
    o_ref[...] = jnp.concatenate(outs, axis=2).astype(o_ref.dtype)


def _swa_sample(p3, cache_k, cache_v, sinks, layer, precise, *, bb=8):
    batch, steps, _ = p3.shape
    wbuf = cache_k.shape[2]
    kblk, vblk = OFF_K // KV_WIDTH, OFF_V // KV_WIDTH
    cache_spec = pl.BlockSpec((None, bb, wbuf, KV_WIDTH), lambda i: (layer, i, 0, 0))
    new_spec = pl.BlockSpec((bb, wbuf, KV_WIDTH), lambda i: (i, 0, 0))
    return pl.pallas_call(
        functools.partial(_swa_sample_kernel, layer=layer, steps=steps, wbuf=wbuf, precise=precise),
        grid=(batch // bb,),
        in_specs=[pl.BlockSpec(memory_space=pltpu.SMEM),
                  pl.BlockSpec((bb, steps, ATTN_WIDTH), lambda i: (i, 0, 0)),
                  pl.BlockSpec((bb, steps, KV_WIDTH), lambda i: (i, 0, kblk)),
                  pl.BlockSpec((bb, steps, KV_WIDTH), lambda i: (i, 0, vblk)),
                  cache_spec, cache_spec],
        out_specs=[pl.BlockSpec((bb, steps, ATTN_WIDTH), lambda i: (i, 0, 0)), new_spec, new_spec],
        out_shape=[jax.ShapeDtypeStruct((batch, steps, ATTN_WIDTH), _act_dtype(precise)),
                   jax.ShapeDtypeStruct((batch, wbuf, KV_WIDTH), F32),
                   jax.ShapeDtypeStruct((batch, wbuf, KV_WIDTH), F32)],
        compiler_params=_cp("parallel"), name="swa_sample",
    )(sinks, p3, p3, p3, cache_k, cache_v)


def _shift_rows(x, k):
    rows = lax.broadcasted_iota(jnp.int32, x.shape, 0)
    return jnp.where(rows >= k, pltpu.roll(x, k, axis=0), 0.0)


def _pool_prompt_kernel(u_ref, w_ref, ls_ref, o_ref, *, precise):
    u = u_ref[...]
    t = u.shape[0]
    pos1 = (lax.broadcasted_iota(jnp.int32, (t, 1), 0) + 1).astype(F32)
    sums = {1: u}
    win = 1
    while win < max(POOL_WINDOWS):
        sums[2 * win] = sums[win] + _shift_rows(sums[win], win)
        win *= 2
    outs = []
    for gi, win in enumerate(POOL_WINDOWS):
        sl = slice(gi * POOL_GW, (gi + 1) * POOL_GW)
        cnt = jnp.minimum(float(win), pos1)
        d = sums[win][:, sl] / cnt - u[:, sl]
        outs.append(_bdot(d, w_ref[gi], precise))
    o_ref[...] = (jnp.concatenate(outs, axis=1) * ls_ref[...]).astype(o_ref.dtype)


def _pool_prompt(p, w_pool, ls_pool, layer, batch, seq, precise):
    gw = POOL_GW
    return pl.pallas_call(
        functools.partial(_pool_prompt_kernel, precise=precise),
        grid=(batch,),
        in_specs=[pl.BlockSpec((seq, POOL_WIDTH), lambda b: (b, OFF_U // POOL_WIDTH)),
                  pl.BlockSpec((None, len(POOL_WINDOWS), gw, gw), lambda b: (layer, 0, 0, 0)),
                  pl.BlockSpec((None, 1, POOL_WIDTH), lambda b: (layer, 0, 0))],
        out_specs=pl.BlockSpec((seq, POOL_WIDTH), lambda b: (b, 0)),
        out_shape=jax.ShapeDtypeStruct((batch * seq, POOL_WIDTH), _act_dtype(precise)),
        compiler_params=_cp("parallel"), name="pool_prompt",
    )(p, w_pool, ls_pool.reshape(ls_pool.shape[0], 1, POOL_WIDTH))


def _pool_sample_kernel(u_ref, past_ref, w_ref, ls_ref, o_ref, new_ref, *, pos0, precise):
    steps, hist = u_ref.shape[0], past_ref.shape[0]
    full = [past_ref[i] for i in range(hist)] + [u_ref[i] for i in range(steps)]
    for i in range(hist):
        new_ref[i] = full[steps + i]
    ds = [[] for _ in POOL_WINDOWS]
    for t in range(steps):
        for gi, win in enumerate(POOL_WINDOWS):
            sl = slice(gi * POOL_GW, (gi + 1) * POOL_GW)
            wsum = full[hist + t][:, sl]
            for s in range(1, win):
                wsum = wsum + full[hist + t - s][:, sl]
            cnt = float(min(win, pos0 + t + 1))
            ds[gi].append(wsum / cnt - full[hist + t][:, sl])
    ys = [_bdot(jnp.concatenate(ds[gi], axis=0), w_ref[gi], precise) for gi in range(len(POOL_WINDOWS))]
    y = jnp.concatenate(ys, axis=1) * ls_ref[...]
    nb = u_ref.shape[1]
    for t in range(steps):
        o_ref[t] = y[t * nb:(t + 1) * nb].astype(o_ref.dtype)


def _pool_sample(u_t, past_t, w_pool, ls_pool, layer, pos0, precise):
    steps, nb, _ = u_t.shape
    return pl.pallas_call(
        functools.partial(_pool_sample_kernel, pos0=pos0, precise=precise),
        grid=(1,),
        in_specs=[pl.BlockSpec(u_t.shape, lambda i: (0, 0, 0)),
                  pl.BlockSpec(past_t.shape, lambda i: (0, 0, 0)),
                  pl.BlockSpec((None, len(POOL_WINDOWS), POOL_GW, POOL_GW), lambda i: (layer, 0, 0, 0)),
                  pl.BlockSpec((None, 1, POOL_WIDTH), lambda i: (layer, 0, 0))],
        out_specs=[pl.BlockSpec(u_t.shape, lambda i: (0, 0, 0)), pl.BlockSpec(past_t.shape, lambda i: (0, 0, 0))],
        out_shape=[jax.ShapeDtypeStruct(u_t.shape, _act_dtype(precise)), jax.ShapeDtypeStruct(past_t.shape, F32)],
        compiler_params=_cp("arbitrary"), name="pool_sample",
    )(u_t, past_t, w_pool, ls_pool.reshape(ls_pool.shape[0], 1, POOL_WIDTH))


def _conv_prompt_kernel(cb_ref, cc_ref, cx_ref, w_ref, o_ref, new_ref):
    z = cc_ref[...] * cx_ref[...]
    w = w_ref[...]
    y = w[CONV_K - 1:CONV_K] * z
    for j in range(1, CONV_K):
        y = y + w[CONV_K - 1 - j:CONV_K - j] * _shift_rows(z, j)
    o_ref[...] = (cb_ref[...] * y).astype(o_ref.dtype)
    new_ref[...] = z[z.shape[0] - (CONV_K - 1):]


def _conv_prompt(p, conv_w, layer, batch, seq, precise, *, tc=256):
    nc = CONV_WIDTH // tc

    def col(off):
        return lambda b, c: (b, off // tc + c)

    return pl.pallas_call(
        _conv_prompt_kernel,
        grid=(batch, nc),
        in_specs=[pl.BlockSpec((seq, tc), col(OFF_CB)), pl.BlockSpec((seq, tc), col(OFF_CC)),
                  pl.BlockSpec((seq, tc), col(OFF_CX)),
                  pl.BlockSpec((None, CONV_K, tc), lambda b, c: (layer, 0, c))],
        out_specs=[pl.BlockSpec((seq, tc), lambda b, c: (b, c)),
                   pl.BlockSpec((None, CONV_K - 1, tc), lambda b, c: (b, 0, c))],
        out_shape=[jax.ShapeDtypeStruct((batch * seq, CONV_WIDTH), _act_dtype(precise)),
                   jax.ShapeDtypeStruct((batch, CONV_K - 1, CONV_WIDTH), F32)],
        compiler_params=_cp("parallel", "parallel"), name="conv_prompt",
    )(p, p, p, conv_w)


def _conv_sample_kernel(cb_ref, cc_ref, cx_ref, past_ref, w_ref, o_ref, new_ref):
    steps, hist = cb_ref.shape[0], past_ref.shape[0]
    w = w_ref[...]
    full = [past_ref[i] for i in range(hist)] + [cc_ref[t] * cx_ref[t] for t in range(steps)]
    for t in range(steps):
        y = w[0:1] * full[t]
        for j in range(1, CONV_K):
            y = y + w[j:j + 1] * full[t + j]
        o_ref[t] = (cb_ref[t] * y).astype(o_ref.dtype)
    for i in range(hist):
        new_ref[i] = full[steps + i]


def _conv_sample(cb_t, cc_t, cx_t, past_t, conv_w, layer, precise):
    full3 = lambda shape: pl.BlockSpec(shape, lambda i: (0, 0, 0))
    return pl.pallas_call(
        _conv_sample_kernel,
        grid=(1,),
        in_specs=[full3(cb_t.shape), full3(cc_t.shape), full3(cx_t.shape), full3(past_t.shape),
                  pl.BlockSpec((None, CONV_K, CONV_WIDTH), lambda i: (layer, 0, 0))],
        out_specs=[full3(cb_t.shape), full3(past_t.shape)],
        out_shape=[jax.ShapeDtypeStruct(cb_t.shape, _act_dtype(precise)), jax.ShapeDtypeStruct(past_t.shape, F32)],
        compiler_params=_cp("arbitrary"), name="conv_sample",
    )(cb_t, cc_t, cx_t, past_t, conv_w)


def _head_sum(x):
    rows = x.shape[0]
    return jnp.concatenate(
        [jnp.broadcast_to(jnp.sum(x[:, h * RWKV_HEAD:(h + 1) * RWKV_HEAD], axis=-1, keepdims=True), (rows, RWKV_HEAD))
         for h in range(RWKV_HEADS)], axis=1)


def _softplus(x):
    return jnp.maximum(x, 0.0) + jnp.log(1.0 + jnp.exp(-jnp.abs(x)))


def _rwkv_pre_values(cur, sh, mu, w0, w2, a0, a2, g2, k_k, k_a, precise):
    xr, xk, xv, xwa, xg = [c + (s - c) * m for c, s, m in zip(cur, sh, mu)]
    wd, ad = xwa[:, :DECAY_LORA], xwa[:, DECAY_LORA:]
    w_log = -_softplus(-(w0 + _bdot(jnp.tanh(wd), w2, precise))) - 0.5
    log_decay = -jnp.exp(w_log)
    a = _sigmoid(a0 + _bdot(ad, a2, precise))
    g = _bdot(_sigmoid(xg), g2, precise)
    kk = xk * k_k
    kk = kk / jnp.maximum(jnp.sqrt(_head_sum(kk * kk)), 1e-12)
    kf = xk * (1.0 + (a - 1.0) * k_a)
    return xr, log_decay, kf, xv, -kk, kk * a, g


def _rwkv_pre_core(cur, sh, mu, w0, w2, a0, a2, g2, k_k, k_a, outs, precise):
    for o_ref, val in zip(outs, _rwkv_pre_values(cur, sh, mu, w0, w2, a0, a2, g2, k_k, k_a, precise)):
        o_ref[...] = val


_PRE_WIDTHS = (RWKV_WIDTH, RWKV_WIDTH, RWKV_WIDTH, DECAY_LORA + AAA_LORA, GATE_LORA)
_PRE_OFFS = (0, RWKV_WIDTH, 2 * RWKV_WIDTH, 3 * RWKV_WIDTH, 3 * RWKV_WIDTH + DECAY_LORA + AAA_LORA)


def _rwkv_param_specs(layer):
    def spec(shape):
        return pl.BlockSpec((None,) + shape, lambda *idx: (layer,) + (0,) * len(shape))

    return [spec((1, RWKV_WIDTH)), spec((DECAY_LORA, RWKV_WIDTH)), spec((1, RWKV_WIDTH)),
            spec((AAA_LORA, RWKV_WIDTH)), spec((GATE_LORA, RWKV_WIDTH)), spec((1, RWKV_WIDTH)), spec((1, RWKV_WIDTH))]


def _rwkv_params(prm):
    depth = prm["w0"].shape[0]
    r3 = lambda a: a.reshape(depth, 1, RWKV_WIDTH)
    return [r3(prm["w0"]), prm["w2"], r3(prm["a0"]), prm["a2"], prm["g2"], r3(prm["kk"]), r3(prm["ka"])]


def _rwkv_pre_sample_kernel(*refs, precise):
    cur = [r[...] for r in refs[0:5]]
    sh = [r[...] for r in refs[5:10]]
    mu = [r[...] for r in refs[10:15]]
    w0, w2, a0, a2, g2, k_k, k_a = [r[...] for r in refs[15:22]]
    _rwkv_pre_core(cur, sh, mu, w0, w2, a0, a2, g2, k_k, k_a, refs[22:], precise)


def _rwkv_pre_sample(pc, pc_shifted, prm, layer, precise):
    rows = pc.shape[0]
    cur_specs, mu_specs = [], []
    for w, off in zip(_PRE_WIDTHS, _PRE_OFFS):
        cur_specs.append(pl.BlockSpec((rows, w), lambda i, cb=off // w: (0, cb)))
        mu_specs.append(pl.BlockSpec((None, 1, w), lambda i, mb=off // w: (layer, 0, mb)))
    out_spec = pl.BlockSpec((rows, RWKV_WIDTH), lambda i: (0, 0))
    mu3 = prm["mu"].reshape(prm["mu"].shape[0], 1, RWKV_PROJ)
    return pl.pallas_call(
        functools.partial(_rwkv_pre_sample_kernel, precise=precise),
        grid=(1,),
        in_specs=cur_specs + cur_specs + mu_specs + _rwkv_param_specs(layer),
        out_specs=[out_spec] * 7,
        out_shape=[jax.ShapeDtypeStruct((rows, RWKV_WIDTH), F32)] * 7,
        compiler_params=_cp("arbitrary"), name="rwkv_pre_sample",
    )(*([pc] * 5), *([pc_shifted] * 5), *([mu3] * 5), *_rwkv_params(prm))


def _split_heads(x):
    return jnp.stack([x[:, h * RWKV_HEAD:(h + 1) * RWKV_HEAD] for h in range(RWKV_HEADS)], axis=0)


def _join_heads(x):
    return jnp.concatenate([x[h] for h in range(RWKV_HEADS)], axis=1)


def _wkv_steps_kernel(r_ref, ld_ref, k_ref, v_ref, a_ref, b_ref, s0_ref, y_ref, sf_ref, *, steps, bb):
    n = RWKV_HEAD
    eye = lax.broadcasted_iota(jnp.int32, (n, n), 0) == lax.broadcasted_iota(jnp.int32, (n, n), 1)
    for i in range(bb):
        s = s0_ref[i]
        seqs = [_split_heads(ref[i]) for ref in (r_ref, ld_ref, k_ref, v_ref, a_ref, b_ref)]
        out_rows = []
        for t in range(steps):
            r, ld, k, v, a, b = [x[:, t:t + 1, :] for x in seqs]
            sa = jnp.sum(s * a, axis=-1, keepdims=True)
            vcol = jnp.sum(jnp.where(eye, v, 0.0), axis=-1, keepdims=True)
            s = s * jnp.exp(ld) + sa * b + vcol * k
            ycol = jnp.sum(s * r, axis=-1, keepdims=True)
            out_rows.append(jnp.sum(jnp.where(eye, ycol, 0.0), axis=1, keepdims=True))
        y_ref[i] = _join_heads(jnp.concatenate(out_rows, axis=1))
        sf_ref[i] = s


def _wkv_steps(seqs, s0, *, bb=4):
    batch, t, _ = seqs[0].shape
    seq_spec = pl.BlockSpec((bb, t, RWKV_WIDTH), lambda i: (i, 0, 0))
    st_spec = pl.BlockSpec((bb, RWKV_HEADS, RWKV_HEAD, RWKV_HEAD), lambda i: (i, 0, 0, 0))
    return pl.pallas_call(
        functools.partial(_wkv_steps_kernel, steps=t, bb=bb),
        grid=(batch // bb,),
        in_specs=[seq_spec] * 6 + [st_spec],
        out_specs=[seq_spec, st_spec],
        out_shape=[jax.ShapeDtypeStruct((batch, t, RWKV_WIDTH), F32), jax.ShapeDtypeStruct(s0.shape, F32)],
        compiler_params=_cp("parallel"), name="wkv_steps",
    )(*seqs, s0)


WKV_CHUNK = 64


def _e_nt(a, b):
    return jnp.einsum("hqd,hkd->hqk", a, b, preferred_element_type=F32)


def _e_nn(a, b):
    return jnp.einsum("hqk,hkd->hqd", a, b, preferred_element_type=F32)


def _rwkv_prompt_kernel(*refs, precise):
    cur_refs, mu_refs = refs[0:5], refs[5:10]
    w0, w2, a0, a2, g2, k_k, k_a = [r[...] for r in refs[10:17]]
    rk_ref, lg_ref, lb_ref = refs[17:20]
    o_ref, sf_ref = refs[20:22]
    s_scr, prev_scrs = refs[22], refs[23:28]
    c = WKV_CHUNK

    @pl.when(pl.program_id(1) == 0)
    def _():
        s_scr[...] = jnp.zeros_like(s_scr)
        for p_scr in prev_scrs:
            p_scr[...] = jnp.zeros_like(p_scr)

    cur, sh = [], []
    for c_ref, p_scr in zip(cur_refs, prev_scrs):
        x = c_ref[...]
        rows = lax.broadcasted_iota(jnp.int32, x.shape, 0)
        sh.append(jnp.where(rows == 0, p_scr[...], pltpu.roll(x, 1, axis=0)))
        cur.append(x)
    r, ld, kf, v2, a2d, b2d, g = _rwkv_pre_values(cur, sh, [m[...] for m in mu_refs], w0, w2, a0, a2, g2, k_k, k_a,
                                                    precise)
    for c_ref, p_scr in zip(cur_refs, prev_scrs):
        p_scr[...] = c_ref[c - 1:c, :]

    cum = ld
    k = 1
    while k < c:
        cum = cum + _shift_rows(cum, k)
        k *= 2
    e_pos, e_prev, e_neg = jnp.exp(cum), jnp.exp(cum - ld), jnp.exp(-cum)
    at = _split_heads(a2d * e_prev)
    rt = _split_heads(r * e_pos)
    bt = _split_heads(b2d * e_neg)
    kt = _split_heads(kf * e_neg)
    v = _split_heads(v2)
    lam = _split_heads(e_pos[c - 1:c, :])
    bonus = _join_heads(jnp.sum(_split_heads(r * kf * rk_ref[...]), axis=-1, keepdims=True) * v)

    ti = lax.broadcasted_iota(jnp.int32, (c, c), 0)
    si = lax.broadcasted_iota(jnp.int32, (c, c), 1)
    strict = ti > si

    def blockmask(size):
        same = (ti // size) == (si // size)
        return strict & same & ((ti // (size // 2)) != (si // (size // 2)))

    def run(three_pass):
        dot_nt = functools.partial(_mxu, _e_nt, precise=three_pass)
        dot_nn = functools.partial(_mxu, _e_nn, precise=three_pass)

        def dot_tn(x, y):
            return dot_nn(jnp.swapaxes(x, 1, 2), y)

        ar = jnp.concatenate([at, rt], axis=1)
        g_b = dot_nt(ar, bt)
        g_k = dot_nt(ar, kt)
        n_ab = jnp.where(strict, g_b[:, :c, :], 0.0)
        n_ak = jnp.where(strict, g_k[:, :c, :], 0.0)
        m_rb = jnp.where(ti >= si, g_b[:, c:, :], 0.0)
        m_rk = jnp.where(ti >= si, g_k[:, c:, :], 0.0)

        base = 8
        n8 = jnp.where((ti // base) == (si // base), n_ab, 0.0)
        eye = (ti == si).astype(F32)
        n8_2 = dot_nn(n8, n8)
        n8_4 = dot_nn(n8_2, n8_2)
        t_inv = eye + n8
        t_inv = t_inv + dot_nn(t_inv, n8_2)
        t_inv = t_inv + dot_nn(t_inv, n8_4)
        size = 2 * base
        while size <= c:
            off = jnp.where(blockmask(size), n_ab, 0.0)
            t_inv = t_inv + dot_nn(dot_nn(t_inv, off), t_inv)
            size *= 2

        wv = dot_nn(n_ak, v)
        a_bar = dot_nn(t_inv, at)
        u_bar = dot_nn(t_inv, wv)
        r_bar = rt + dot_nn(m_rb, a_bar)
        y_bar = dot_nn(m_rb, u_bar) + dot_nn(m_rk, v)
        phi = dot_tn(a_bar, bt)
        psi = dot_tn(jnp.concatenate([u_bar, v], axis=1), jnp.concatenate([bt, kt], axis=1))

        s0 = s_scr[...]
        y = dot_nt(r_bar, s0) + y_bar
        s_scr[...] = (s0 + dot_nn(s0, phi) + psi) * lam
        yc = y - jnp.mean(y, axis=-1, keepdims=True)
        yn = _join_heads(yc * lax.rsqrt(jnp.mean(yc * yc, axis=-1, keepdims=True) + RWKV_LN_EPS))
        o_ref[...] = ((yn * lg_ref[...] + lb_ref[...] + bonus) * g).astype(o_ref.dtype)

    last = pl.program_id(1) >= pl.num_programs(1) - PRECISE_TAIL // c

    @pl.when(last)
    def _():
        run(True)

    @pl.when(jnp.logical_not(last))
    def _():
        run(False)

    @pl.when(pl.program_id(1) == pl.num_programs(1) - 1)
    def _():
        sf_ref[...] = s_scr[...]


def _rwkv_prompt(p, prm, layer, batch, seq, precise):
    c = WKV_CHUNK
    nc = seq // c
    depth = prm["rk"].shape[0]
    cur_specs, mu_specs, prev_scratch = [], [], []
    for w, off in zip(_PRE_WIDTHS, _PRE_OFFS):
        cur_specs.append(pl.BlockSpec((c, w), lambda b, i, cb=(OFF_PC + off) // w: (b * nc + i, cb)))
        mu_specs.append(pl.BlockSpec((None, 1, w), lambda b, i, mb=off // w: (layer, 0, mb)))
        prev_scratch.append(pltpu.VMEM((1, w), F32))
    vec = pl.BlockSpec((None, 1, RWKV_WIDTH), lambda b, i: (layer, 0, 0))
    r3 = lambda a: a.reshape(depth, 1, RWKV_WIDTH)
    mu3 = prm["mu"].reshape(depth, 1, RWKV_PROJ)
    st_shape = (batch, RWKV_HEADS, RWKV_HEAD, RWKV_HEAD)
    return pl.pallas_call(
        functools.partial(_rwkv_prompt_kernel, precise=precise),
        grid=(batch, nc),
        in_specs=cur_specs + mu_specs + _rwkv_param_specs(layer) + [vec] * 3,
        out_specs=[pl.BlockSpec((c, RWKV_WIDTH), lambda b, i: (b * nc + i, 0)),
                   pl.BlockSpec((None,) + st_shape[1:], lambda b, i: (b, 0, 0, 0))],
        out_shape=[jax.ShapeDtypeStruct((batch * seq, RWKV_WIDTH), _act_dtype(precise)),
                   jax.ShapeDtypeStruct(st_shape, F32)],
        scratch_shapes=[pltpu.VMEM(st_shape[1:], F32)] + prev_scratch,
        compiler_params=_cp("parallel", "arbitrary"), name="rwkv_prompt",
    )(*([p] * 5), *([mu3] * 5), *_rwkv_params(prm), r3(prm["rk"]), r3(prm["ln_g"]), r3(prm["ln_b"]))


def _rwkv_post_kernel(y_ref, r_ref, k_ref, v_ref, g_ref, rk_ref, lg_ref, lb_ref, o_ref):
    y = y_ref[...]
    inv = 1.0 / RWKV_HEAD
    mean = _head_sum(y) * inv
    yc = y - mean
    var = _head_sum(yc * yc) * inv
    yn = yc * lax.rsqrt(var + RWKV_LN_EPS) * lg_ref[...] + lb_ref[...]
    v = v_ref[...]
    bonus = _head_sum(r_ref[...] * k_ref[...] * rk_ref[...]) * v
    o_ref[...] = ((yn + bonus) * g_ref[...]).astype(o_ref.dtype)


def _rwkv_post(y, r, k, v, g, prm, layer, precise, *, tm):
    rows = y.shape[0]
    depth = prm["rk"].shape[0]
    row = pl.BlockSpec((tm, RWKV_WIDTH), lambda i: (i, 0))
    vec = pl.BlockSpec((None, 1, RWKV_WIDTH), lambda i: (layer, 0, 0))
    r3 = lambda a: a.reshape(depth, 1, RWKV_WIDTH)
    return pl.pallas_call(
        _rwkv_post_kernel,
        grid=(rows // tm,),
        in_specs=[row] * 5 + [vec] * 3,
        out_specs=row,
        out_shape=jax.ShapeDtypeStruct((rows, RWKV_WIDTH), _act_dtype(precise)),
        compiler_params=_cp("parallel"), name="rwkv_post",
    )(y, r, k, v, g, r3(prm["rk"]), r3(prm["ln_g"]), r3(prm["ln_b"]))


MOE_TM = 256


def _dispatch_plan(e_idx, tm):
    n = e_idx.shape[0]
    pairs = n * TOP_K
    e_flat = e_idx.reshape(pairs)
    onehot = (e_flat[:, None] == jnp.arange(N_EXPERTS, dtype=jnp.int32)[None, :]).astype(jnp.int32)
    csum = jnp.cumsum(onehot, axis=0)
    counts = csum[-1]
    padded = ((counts + tm - 1) // tm) * tm
    ends = jnp.cumsum(padded)
    starts = ends - padded
    pos = jnp.sum(onehot * (csum - 1 + starts[None, :]), axis=1)
    nt = (pairs + N_EXPERTS * (tm - 1)) // tm
    tile_start = jnp.arange(nt, dtype=jnp.int32) * tm
    tile_e = jnp.minimum(jnp.sum((tile_start[:, None] >= ends[None, :]).astype(jnp.int32), axis=1), N_EXPERTS - 1)
    tile_nv = jnp.clip(jnp.take(starts + counts, tile_e) - tile_start, 0, tm)
    src = jnp.zeros((nt * tm,), jnp.int32).at[pos].set(jnp.arange(pairs, dtype=jnp.int32) // TOP_K)
    return jnp.stack([tile_e, tile_nv]).astype(jnp.int32), src.reshape(nt, 1, tm), pos


def _moe_gather_kernel(meta_ref, src_ref, h_hbm, o_ref, sem):
    nv = meta_ref[1, pl.program_id(0)]
    tm = o_ref.shape[0]

    @pl.when(nv > 0)
    def _():
        def issue(r, c):
            pltpu.make_async_copy(h_hbm.at[pl.ds(src_ref[0, r], 1)], o_ref.at[pl.ds(r, 1)], sem).start()
            return c

        lax.fori_loop(0, tm, issue, 0, unroll=8)
        pltpu.make_async_copy(h_hbm.at[pl.ds(0, tm)], o_ref, sem).wait()

    rows = lax.broadcasted_iota(jnp.int32, o_ref.shape, 0)
    o_ref[...] = jnp.where(rows < nv, o_ref[...], 0.0)


def _moe_gather(meta, src, h):
    nt, _, tm = src.shape
    d = h.shape[1]
    gs = pltpu.PrefetchScalarGridSpec(
        num_scalar_prefetch=1, grid=(nt,),
        in_specs=[pl.BlockSpec((None, 1, tm), lambda j, m: (j, 0, 0), memory_space=pltpu.SMEM),
                  pl.BlockSpec(memory_space=pl.ANY)],
        out_specs=pl.BlockSpec((tm, d), lambda j, m: (j, 0)),
        scratch_shapes=[pltpu.SemaphoreType.DMA(())])
    return pl.pallas_call(
        _moe_gather_kernel, grid_spec=gs, out_shape=jax.ShapeDtypeStruct((nt * tm, d), F32),
        compiler_params=_cp("arbitrary"), name="moe_gather",
    )(meta, src, h)


def _moe_experts_kernel(meta_ref, xs_ref, wg_ref, wu_ref, wd_ref, y_ref, wg_bf, wu_bf, wd_bf):
    j = pl.program_id(0)
    e = meta_ref[0, j]
    nv = meta_ref[1, j]
    new_expert = (j == 0) | (e != meta_ref[0, jnp.maximum(j - 1, 0)])

    @pl.when(new_expert)
    def _():
        wg_bf[...] = wg_ref[...].astype(BF16)
        wu_bf[...] = wu_ref[...].astype(BF16)
        wd_bf[...] = wd_ref[...].astype(BF16)

    @pl.when(nv > 0)
    def _():
        x = xs_ref[...].astype(BF16)
        gate = _dot2(x, wg_bf[...])
        act = gate * _sigmoid(gate) * _dot2(x, wu_bf[...])
        y_ref[...] = _dot2(act.astype(BF16), wd_bf[...])

    @pl.when(nv == 0)
    def _():
        y_ref[...] = jnp.zeros_like(y_ref)


def _moe_experts(meta, xs, w_gate, w_up, w_down, layer):
    rows, d = xs.shape
    nt = meta.shape[1]
    tm = rows // nt
    once = dict(pipeline_mode=pl.Buffered(1))
    gs = pltpu.PrefetchScalarGridSpec(
        num_scalar_prefetch=1, grid=(nt,),
        in_specs=[pl.BlockSpec((tm, d), lambda j, m: (j, 0)),
                  pl.BlockSpec((None, None, d, D_EXPERT), lambda j, m: (layer, m[0, j], 0, 0), **once),
                  pl.BlockSpec((None, None, d, D_EXPERT), lambda j, m: (layer, m[0, j], 0, 0), **once),
                  pl.BlockSpec((None, None, D_EXPERT, d), lambda j, m: (layer, m[0, j], 0, 0), **once)],
        out_specs=pl.BlockSpec((tm, d), lambda j, m: (j, 0)),
        scratch_shapes=[pltpu.VMEM((d, D_EXPERT), BF16), pltpu.VMEM((d, D_EXPERT), BF16),
                        pltpu.VMEM((D_EXPERT, d), BF16)])
    return pl.pallas_call(
        _moe_experts_kernel, grid_spec=gs, out_shape=jax.ShapeDtypeStruct((rows, d), F32),
        compiler_params=_cp("arbitrary"), name="moe_experts",
    )(meta, xs, w_gate, w_up, w_down)


def _moe_combine_kernel(pos_ref, w_ref, y_hbm, m_ref, buf, sem):
    tt = m_ref.shape[0]

    def issue(r, c):
        pltpu.make_async_copy(y_hbm.at[pl.ds(pos_ref[0, r], 1)], buf.at[pl.ds(r, 1)], sem).start()
        return c

    lax.fori_loop(0, TOP_K * tt, issue, 0, unroll=8)
    pltpu.make_async_copy(y_hbm.at[pl.ds(0, TOP_K * tt)], buf, sem).wait()
    w = w_ref[...]
    m_ref[...] = w[:, 0:1] * buf[0:tt, :] + w[:, 1:2] * buf[tt:2 * tt, :]


def _moe_combine(pos, wts, y, *, tt):
    n = wts.shape[0]
    d = y.shape[1]
    pos_t = jnp.swapaxes(pos.reshape(n // tt, tt, TOP_K), 1, 2).reshape(n // tt, 1, TOP_K * tt)
    return pl.pallas_call(
        _moe_combine_kernel,
        grid=(n // tt,),
        in_specs=[pl.BlockSpec((None, 1, TOP_K * tt), lambda i: (i, 0, 0), memory_space=pltpu.SMEM),
                  pl.BlockSpec((tt, TOP_K), lambda i: (i, 0)),
                  pl.BlockSpec(memory_space=pl.ANY)],
        out_specs=pl.BlockSpec((tt, d), lambda i: (i, 0)),
        out_shape=jax.ShapeDtypeStruct((n, d), F32),
        scratch_shapes=[pltpu.VMEM((TOP_K * tt, d), F32), pltpu.SemaphoreType.DMA(())],
        compiler_params=_cp("arbitrary"), name="moe_combine",
    )(pos_t, wts, y)


def _moe(h_all, e_idx, wts, w_gate, w_up, w_down, layer):
    n = h_all.shape[0]
    meta, src, pos = _dispatch_plan(e_idx, MOE_TM)
    xs = _moe_gather(meta, src, h_all)
    y = _moe_experts(meta, xs, w_gate, w_up, w_down, layer)
    tt = 256
    while n % tt:
        tt //= 2
    return _moe_combine(pos, wts, y, tt=tt)


_SH1, _SC1, _GT1, _SH2, _SC2, _GT2 = range(6)


def kernel(x_prompt, x_sample, cache_swa_k, cache_swa_v, state_pool, state_rwkv_shift, state_rwkv_wkv, state_conv, c_prompt, c_sample, w_ada, b_ada, g_norm1, g_norm2, w_in, sinks, w_pool, ls_pool, rwkv_mu, rwkv_w0, rwkv_w2, rwkv_a0, rwkv_a2, rwkv_g2, rwkv_kk, rwkv_ka, rwkv_rk, rwkv_ln_g, rwkv_ln_b, conv_w, w_branch, w_out, w_router, b_router, w_gate, w_up, w_down, g_final):
    depth = w_in.shape[0]
    batch, seq, d = x_prompt.shape
    dbatch, dseq, _ = x_sample.shape
    wbuf = cache_swa_k.shape[2]
    np_rows, ns_rows = batch * seq, dbatch * dseq

    grp_p = _Group(np_rows, seq, 1)
    grp_s = _Group(ns_rows, ns_rows, ns_rows)
    assert seq <= 2048, "the big matmuls take one whole prompt sequence per row tile"
    tm_p, tm_s = seq, ns_rows
    te_p = min(512, seq)

    c_all = jnp.concatenate([c_prompt, c_sample], axis=0)
    wr_pad = jnp.pad(w_router, ((0, 0), (0, ROUTER_PAD - N_EXPERTS)))
    prm = dict(mu=rwkv_mu, w0=rwkv_w0, w2=rwkv_w2, a0=rwkv_a0, a2=rwkv_a2, g2=rwkv_g2, kk=rwkv_kk, ka=rwkv_ka,
               rk=rwkv_rk.reshape(depth, RWKV_WIDTH), ln_g=rwkv_ln_g, ln_b=rwkv_ln_b)
    cache_k = cache_swa_k.reshape(depth, dbatch, wbuf, KV_WIDTH)
    cache_v = cache_swa_v.reshape(depth, dbatch, wbuf, KV_WIDTH)

    xp = x_prompt.reshape(np_rows, d)
    xs = x_sample.reshape(ns_rows, d)
    st_p, st_s = [], []
    pend_p = pend_s = None
    for l in range(depth):
        precise = l == 0
        hd = (BF16, F32) if precise else (BF16,)
        mod = _ada(c_all, w_ada, b_ada, l, precise)
        mod_p = mod[:batch].reshape(batch, 1, 6 * d)
        mod_s = jnp.repeat(mod[batch:], dseq, axis=0).reshape(1, ns_rows, 6 * d)

        def first_norm(grp, x, pend, mod3, tm):
            if pend is None:
                return (x,) + tuple(_normx(grp, x, g_norm1[l], mod=(mod3, _SC1, _SH1), out_dtype=hd, tm=tm))
            return tuple(_normx(grp, x, g_norm1[l], add=pend + (_GT2,), mod=(mod3, _SC1, _SH1), emit_x=True,
                                out_dtype=hd, tm=tm))

        xp, hp, *hp32 = first_norm(grp_p, xp, pend_p, mod_p, te_p)
        xs, hs, *hs32 = first_norm(grp_s, xs, pend_s, mod_s, tm_s)

        def tails(tm, grp):
            if not precise:
                return None
            return (tm, 1) if grp is grp_s else (min(PRECISE_TAIL, tm), seq // tm)

        tail_p, tail_s = (min(PRECISE_TAIL, tm_p), tm_s) if precise else (0, 0)
        pp = _mm(hp, w_in, l, tm=tm_p, tn=512, a_f32=hp32[0] if precise else None, tail_rows=tail_p)
        ps = _mm(hs, w_in, l, tm=tm_s, tn=512, a_f32=hs32[0] if precise else None, tail_rows=tail_s)

        a_p = _swa_prompt(pp, sinks, l, batch, seq, precise)
        b_p = _pool_prompt(pp, w_pool, ls_pool, l, batch, seq, precise)
        d_p, conv_new_p = _conv_prompt(pp, conv_w, l, batch, seq, precise)
        c_p, wkv_new_p = _rwkv_prompt(pp, prm, l, batch, seq, precise)
        pp3 = pp.reshape(batch, seq, IN_WIDTH)
        kw = min(WINDOW, seq)
        st_p.append((pp3[:, seq - kw:, OFF_K:OFF_K + KV_WIDTH].reshape(batch, kw, N_KV_HEADS, HEAD_DIM),
                     pp3[:, seq - kw:, OFF_V:OFF_V + KV_WIDTH].reshape(batch, kw, N_KV_HEADS, HEAD_DIM),
                     pp3[:, seq - POOL_BUF:, OFF_U:OFF_U + POOL_WIDTH],
                     pp3[:, seq - 1, OFF_PC:OFF_PC + RWKV_PROJ],
                     wkv_new_p, conv_new_p))

        ps3 = ps.reshape(dbatch, dseq, IN_WIDTH)
        a_s, k_new_s, v_new_s = _swa_sample(ps3, cache_k, cache_v, sinks, l, precise)
        tmaj = lambda t: jnp.swapaxes(t, 0, 1)
        b_s_t, pool_new_t = _pool_sample(tmaj(ps3[:, :, OFF_U:OFF_U + POOL_WIDTH]), tmaj(state_pool[l]),
                                         w_pool, ls_pool, l, PAST_LEN, precise)
        d_s_t, conv_new_t = _conv_sample(tmaj(ps3[:, :, OFF_CB:OFF_CB + CONV_WIDTH]),
                                         tmaj(ps3[:, :, OFF_CC:OFF_CC + CONV_WIDTH]),
                                         tmaj(ps3[:, :, OFF_CX:OFF_CX + CONV_WIDTH]), tmaj(state_conv[l]), conv_w, l,
                                         precise)
        pc_s3 = ps3[:, :, OFF_PC:OFF_PC + RWKV_PROJ]
        pc_shift = jnp.concatenate([state_rwkv_shift[l][:, None, :], pc_s3[:, :-1, :]], axis=1)
        pre_s = _rwkv_pre_sample(pc_s3.reshape(ns_rows, RWKV_PROJ), pc_shift.reshape(ns_rows, RWKV_PROJ), prm, l,
                                 precise)
        r_s, w_s, k_s, v_s, ka_s, kb_s, g_s = pre_s
        as3s = lambda t: t.reshape(dbatch, dseq, RWKV_WIDTH)
        y_s, wkv_new_s = _wkv_steps([as3s(t) for t in (r_s, w_s, k_s, v_s, ka_s, kb_s)], state_rwkv_wkv[l])
        c_s = _rwkv_post(y_s.reshape(ns_rows, RWKV_WIDTH), r_s, k_s, v_s, g_s, prm, l, precise, tm=ns_rows)
        st_s.append((k_new_s.reshape(dbatch, wbuf, N_KV_HEADS, HEAD_DIM),
                     v_new_s.reshape(dbatch, wbuf, N_KV_HEADS, HEAD_DIM),
                     tmaj(pool_new_t), pc_s3[:, dseq - 1, :], wkv_new_s, tmaj(conv_new_t)))
        br_s = (a_s.reshape(ns_rows, ATTN_WIDTH), tmaj(b_s_t).reshape(ns_rows, POOL_WIDTH), c_s,
                tmaj(d_s_t).reshape(ns_rows, CONV_WIDTH))

        def tail(grp, x, branches, p, mod3, tm_mm, tm_el):
            if precise:
                tm_t = min(tm_mm, 512)
                merged32, merged = _merge(branches, p, w_branch, l, tm=tm_t, tn=512, tail=tails(tm_t, grp))
                x1 = _mm_res(grp, merged, w_out, l, x, mod3, _GT1, tm=tm_mm, tn=512, a_f32=merged32,
                             tail_rows=tail_s if grp is grp_s else tail_p)
            else:
                merged, = _merge(branches, p, w_branch, l, tm=min(tm_mm, 1024), tn=512, tail=None)
                x1 = _mm_res(grp, merged, w_out, l, x, mod3, _GT1, tm=tm_mm, tn=512)
            return (x1,) + tuple(_normx(grp, x1, g_norm2[l], mod=(mod3, _SC2, _SH2), route=(wr_pad, b_router),
                                        out_dtype=F32, tm=tm_el))

        xp, h2_p, e_p, wt_p = tail(grp_p, xp, (a_p, b_p, c_p, d_p), pp, mod_p, tm_p, te_p)
        xs, h2_s, e_s, wt_s = tail(grp_s, xs, br_s, ps, mod_s, tm_s, tm_s)
        m_all = _moe(jnp.concatenate([h2_p, h2_s]), jnp.concatenate([e_p, e_s]), jnp.concatenate([wt_p, wt_s]),
                     w_gate, w_up, w_down, l)
        pend_p, pend_s = (m_all, 0, mod_p), (m_all, np_rows, mod_s)

    y_p = _normx(grp_p, xp, g_final, add=pend_p + (_GT2,), out_dtype=F32, tm=te_p)[0]
    y_s = _normx(grp_s, xs, g_final, add=pend_s + (_GT2,), out_dtype=F32, tm=tm_s)[0]

    def stack(states, i):
        return jnp.stack([s[i] for s in states])

    return ((y_p.reshape(batch, seq, d), y_s.reshape(dbatch, dseq, d))
            + tuple(stack(st_p, i) for i in range(6)) + tuple(stack(st_s, i) for i in range(6)))
```

```python
import functools

import jax
import jax.numpy as jnp
from jax import lax
from jax.experimental import pallas as pl
from jax.experimental.pallas import tpu as pltpu

F32 = jnp.float32
BF16 = jnp.bfloat16

D_MODEL = 2048
PAST_LEN = 8192
WINDOW = 128
HEAD_DIM = 64
N_HEADS = 16
N_KV_HEADS = 4
GQA_GROUP = N_HEADS // N_KV_HEADS
ATTN_WIDTH = N_HEADS * HEAD_DIM
KV_WIDTH = N_KV_HEADS * HEAD_DIM
ATTN_SCALE = HEAD_DIM ** -0.5
NEG_INF = -1e30
POOL_WINDOWS = (2, 4, 8, 16)
POOL_WIDTH = 768
POOL_GW = POOL_WIDTH // len(POOL_WINDOWS)
POOL_BUF = max(POOL_WINDOWS) - 1
RWKV_HEAD = 64
RWKV_WIDTH = 768
RWKV_HEADS = RWKV_WIDTH // RWKV_HEAD
DECAY_LORA = 64
AAA_LORA = 64
GATE_LORA = 128
RWKV_PROJ = 3 * RWKV_WIDTH + DECAY_LORA + AAA_LORA + GATE_LORA
RWKV_LN_EPS = 64e-5
CONV_WIDTH = 768
CONV_K = 3
N_BRANCH = 4
BRANCH_SECTIONS = (ATTN_WIDTH, POOL_WIDTH, RWKV_WIDTH, CONV_WIDTH)
MIX_WIDTH = sum(BRANCH_SECTIONS)
N_EXPERTS = 16
N_GROUPS = 4
EXP_PER_GROUP = N_EXPERTS // N_GROUPS
TOP_K = 2
D_EXPERT = 1024
RMS_EPS = 1e-6

OFF_Q = 0
OFF_K = OFF_Q + ATTN_WIDTH
OFF_V = OFF_K + KV_WIDTH
OFF_U = OFF_V + KV_WIDTH
OFF_PC = OFF_U + POOL_WIDTH
OFF_CB = OFF_PC + RWKV_PROJ
OFF_CC = OFF_CB + CONV_WIDTH
OFF_CX = OFF_CC + CONV_WIDTH
OFF_GL = OFF_CX + CONV_WIDTH
IN_WIDTH = OFF_GL + N_BRANCH * D_MODEL

PRECISE_TAIL = 2 * WINDOW
LANES = 128
ROUTER_PAD = LANES
VMEM_LIMIT = 60 * 1024 * 1024


def _cp(*sem):
    return pltpu.CompilerParams(dimension_semantics=sem, vmem_limit_bytes=VMEM_LIMIT)


def _sigmoid(x):
    return 1.0 / (1.0 + jnp.exp(-x))


def _split(x):
    hi = x.astype(BF16)
    return hi, (x.astype(F32) - hi.astype(F32)).astype(BF16)


def _mxu(contract, a, b, precise):
    if not precise:
        return contract(a.astype(BF16), b.astype(BF16))
    ah, al = _split(a)
    bh, bl = _split(b)
    return contract(ah, bh) + (contract(ah, bl) + contract(al, bh))


def _dot2(a, b):
    return jnp.dot(a, b, preferred_element_type=F32)


def _bdot(a, b, precise=False):
    return _mxu(_dot2, a, b, precise)


def _act_dtype(precise):
    return F32 if precise else BF16


def _ada_kernel(c_ref, w_ref, b_ref, o_ref, *, precise):
    c = c_ref[...]
    o_ref[...] = _bdot(c * _sigmoid(c), w_ref[...], precise) + b_ref[...]


def _ada(c_all, w_ada, b_ada, layer, precise):
    depth, d, n = w_ada.shape
    nb = c_all.shape[0]
    tn = 1024
    return pl.pallas_call(
        functools.partial(_ada_kernel, precise=precise),
        grid=(n // tn,),
        in_specs=[pl.BlockSpec((nb, d), lambda j: (0, 0)),
                  pl.BlockSpec((None, d, tn), lambda j: (layer, 0, j)),
                  pl.BlockSpec((None, 1, tn), lambda j: (layer, 0, j))],
        out_specs=pl.BlockSpec((nb, tn), lambda j: (0, j)),
        out_shape=jax.ShapeDtypeStruct((nb, n), F32),
        compiler_params=_cp("parallel"),
        name="ada",
    )(c_all, w_ada, b_ada.reshape(depth, 1, n))


class _Group:
    def __init__(self, rows, rpm, mod_rows):
        self.rows, self.rpm, self.mod_rows = rows, rpm, mod_rows

    def mod_spec(self, tm, width, col_of, row_axis=0):
        if self.mod_rows == 1:
            per = self.rpm // tm
            return pl.BlockSpec((None, 1, width), lambda *idx: (idx[row_axis] // per, 0, col_of(idx)))
        assert tm == self.rpm == self.mod_rows
        return pl.BlockSpec((None, tm, width), lambda *idx: (idx[row_axis], 0, col_of(idx)))


def _route(y, wr, br):
    y_hi = y.astype(BF16)
    y_lo = (y - y_hi.astype(F32)).astype(BF16)
    w_hi = wr.astype(BF16)
    w_lo = (wr - w_hi.astype(F32)).astype(BF16)
    logits = (jnp.dot(y_hi, w_hi, preferred_element_type=F32)
              + (jnp.dot(y_hi, w_lo, preferred_element_type=F32) + jnp.dot(y_lo, w_hi, preferred_element_type=F32)))
    logits = logits[:, :N_EXPERTS] + br
    tm = logits.shape[0]
    e = jnp.exp(logits - jnp.max(logits, axis=-1, keepdims=True))
    probs = e / jnp.sum(e, axis=-1, keepdims=True)
    iota_g = lax.broadcasted_iota(jnp.int32, (tm, EXP_PER_GROUP), 1)
    best = None
    for g in range(N_GROUPS):
        pg = probs[:, g * EXP_PER_GROUP:(g + 1) * EXP_PER_GROUP]
        m1 = jnp.max(pg, axis=-1, keepdims=True)
        i1 = jnp.min(jnp.where(pg == m1, iota_g, EXP_PER_GROUP), axis=-1, keepdims=True)
        rest = jnp.where(iota_g == i1, -1.0, pg)
        m2 = jnp.max(rest, axis=-1, keepdims=True)
        i2 = jnp.min(jnp.where(rest == m2, iota_g, EXP_PER_GROUP), axis=-1, keepdims=True)
        cand = (m1 + m2, m1, m2, i1 + g * EXP_PER_GROUP, i2 + g * EXP_PER_GROUP)
        if best is None:
            best = cand
        else:
            take = cand[0] > best[0]
            best = tuple(jnp.where(take, c, b) for c, b in zip(cand, best))
    _, m1, m2, e1, e2 = best
    den = m1 + m2
    slot = lax.broadcasted_iota(jnp.int32, (tm, TOP_K), 1)
    return jnp.where(slot == 0, e1, e2), jnp.where(slot == 0, m1 / den, m2 / den)


def _normx_kernel(*refs, has_add, has_mod, has_route, emit_x, n_h):
    it = iter(refs)
    x_ref = next(it)
    if has_add:
        m_ref, gate_ref = next(it), next(it)
    g_ref = next(it)
    if has_mod:
        sc_ref, sh_ref = next(it), next(it)
    if has_route:
        wr_ref, br_ref = next(it), next(it)
    if emit_x:
        xo_ref = next(it)
    h_refs = [next(it) for _ in range(n_h)]
    if has_route:
        eidx_ref, wts_ref = next(it), next(it)
    x = x_ref[...]
    if has_add:
        x = x + gate_ref[...] * m_ref[...]
    if emit_x:
        xo_ref[...] = x
    y = x * lax.rsqrt(jnp.mean(x * x, axis=-1, keepdims=True) + RMS_EPS) * g_ref[...]
    if has_mod:
        y = y * (1.0 + sc_ref[...]) + sh_ref[...]
    for h_ref in h_refs:
        h_ref[...] = y.astype(h_ref.dtype)
    if has_route:
        eidx_ref[...], wts_ref[...] = _route(y, wr_ref[...], br_ref[...])


def _normx(grp, x, g, *, add=None, mod=None, route=None, emit_x=False, out_dtype=BF16, tm=512):
    out_dtypes = out_dtype if isinstance(out_dtype, tuple) else (out_dtype,)
    n, d = x.shape
    row = pl.BlockSpec((tm, d), lambda i: (i, 0))
    vec = pl.BlockSpec((1, d), lambda i: (0, 0))
    args, specs = [x], [row]
    if add is not None:
        m, m_row0, mod3, chunk = add
        args += [m, mod3]
        specs += [pl.BlockSpec((tm, d), lambda i, o=m_row0 // tm: (i + o, 0)),
                  grp.mod_spec(tm, d, lambda idx, c=chunk: c)]
    args.append(g.reshape(1, d))
    specs.append(vec)
    if mod is not None:
        mod3, c_sc, c_sh = mod
        args += [mod3, mod3]
        specs += [grp.mod_spec(tm, d, lambda idx, c=c_sc: c), grp.mod_spec(tm, d, lambda idx, c=c_sh: c)]
    if route is not None:
        wr, br = route
        args += [wr, br.reshape(1, N_EXPERTS)]
        specs += [pl.BlockSpec((d, ROUTER_PAD), lambda i: (0, 0)), pl.BlockSpec((1, N_EXPERTS), lambda i: (0, 0))]
    out_shape, out_specs = [], []
    if emit_x:
        out_shape.append(jax.ShapeDtypeStruct((n, d), F32))
        out_specs.append(row)
    out_shape += [jax.ShapeDtypeStruct((n, d), dt) for dt in out_dtypes]
    out_specs += [row] * len(out_dtypes)
    if route is not None:
        out_shape += [jax.ShapeDtypeStruct((n, TOP_K), jnp.int32), jax.ShapeDtypeStruct((n, TOP_K), F32)]
        out_specs += [pl.BlockSpec((tm, TOP_K), lambda i: (i, 0))] * 2
    return pl.pallas_call(
        functools.partial(_normx_kernel, has_add=add is not None, has_mod=mod is not None,
                          has_route=route is not None, emit_x=emit_x, n_h=len(out_dtypes)),
        grid=(n // tm,), in_specs=specs, out_specs=out_specs, out_shape=out_shape,
        compiler_params=_cp("parallel"), name="normx",
    )(*args)


def _tail_tile(tail):
    return lax.rem(pl.program_id(0) + 1, tail[1]) == 0


def _tail_fix(a, w, tail):
    a_hi, a_lo = _split(a[a.shape[0] - tail[0]:, :])
    w_hi, w_lo = _split(w)
    return _dot2(a_hi, w_lo) + _dot2(a_lo, w_hi)


def _tail_rows(x, rows):
    return x if x.shape[0] == 1 else x[x.shape[0] - rows:, :]


def _tail_spec(tm, k, tail_rows):
    per = tm // tail_rows
    return pl.BlockSpec((tail_rows, k), lambda i, j: ((i + 1) * per - 1, 0))


def _mm_kernel(*refs, tail_rows):
    if tail_rows:
        a_ref, at_ref, w_ref, o_ref = refs
    else:
        a_ref, w_ref, o_ref = refs
    o_ref[...] = _bdot(a_ref[...], w_ref[...])
    if tail_rows:
        lo = o_ref.shape[0] - tail_rows
        o_ref[lo:, :] += _tail_fix(at_ref[...], w_ref[...], (tail_rows, 1))


def _mm(a, w3, layer, *, tm, tn, a_f32=None, tail_rows=0):
    m, k = a.shape
    n = w3.shape[-1]
    tails = [a_f32] if tail_rows else []
    return pl.pallas_call(
        functools.partial(_mm_kernel, tail_rows=tail_rows),
        grid=(m // tm, n // tn),
        in_specs=[pl.BlockSpec((tm, k), lambda i, j: (i, 0))]
        + ([_tail_spec(tm, k, tail_rows)] if tail_rows else [])
        + [pl.BlockSpec((None, k, tn), lambda i, j: (layer, 0, j))],
        out_specs=pl.BlockSpec((tm, tn), lambda i, j: (i, j)),
        out_shape=jax.ShapeDtypeStruct((m, n), F32),
        compiler_params=_cp("parallel", "parallel"), name="mm_in",
    )(a, *tails, w3)


def _mm_res_kernel(*refs, tail_rows):
    if tail_rows:
        a_ref, at_ref, w_ref, x_ref, gate_ref, o_ref = refs
    else:
        a_ref, w_ref, x_ref, gate_ref, o_ref = refs
    o_ref[...] = x_ref[...] + gate_ref[...] * _bdot(a_ref[...], w_ref[...])
    if tail_rows:
        lo = o_ref.shape[0] - tail_rows
        o_ref[lo:, :] += _tail_rows(gate_ref[...], tail_rows) * _tail_fix(at_ref[...], w_ref[...], (tail_rows, 1))


def _mm_res(grp, a, w3, layer, x, mod3, gate_chunk, *, tm, tn, a_f32=None, tail_rows=0):
    m, k = a.shape
    n = w3.shape[-1]
    per_chunk = n // tn
    tails = [a_f32] if tail_rows else []
    return pl.pallas_call(
        functools.partial(_mm_res_kernel, tail_rows=tail_rows),
        grid=(m // tm, n // tn),
        in_specs=[pl.BlockSpec((tm, k), lambda i, j: (i, 0))]
        + ([_tail_spec(tm, k, tail_rows)] if tail_rows else [])
        + [pl.BlockSpec((None, k, tn), lambda i, j: (layer, 0, j)),
           pl.BlockSpec((tm, tn), lambda i, j: (i, j)),
           grp.mod_spec(tm, tn, lambda idx: gate_chunk * per_chunk + idx[1])],
        out_specs=pl.BlockSpec((tm, tn), lambda i, j: (i, j)),
        out_shape=jax.ShapeDtypeStruct((m, n), F32),
        compiler_params=_cp("parallel", "parallel"), name="mm_out",
    )(a, *tails, w3, x, mod3)


def _merge_kernel(a_ref, b_ref, c_ref, d_ref, g0_ref, g1_ref, g2_ref, g3_ref, w_ref, o_ref, *ob_ref, tail):
    parts = []
    lo = 0
    for br_ref, g_ref, width in zip((a_ref, b_ref, c_ref, d_ref), (g0_ref, g1_ref, g2_ref, g3_ref), BRANCH_SECTIONS):
        parts.append((br_ref, g_ref, lo, width))
        lo += width
    acc = None
    for br_ref, g_ref, lo, width in parts:
        t = _sigmoid(g_ref[...]) * _bdot(br_ref[...], w_ref[lo:lo + width, :])
        acc = t if acc is None else acc + t
    o_ref[...] = acc.astype(o_ref.dtype)
    if tail is not None:
        @pl.when(_tail_tile(tail))
        def _():
            row0 = o_ref.shape[0] - tail[0]
            fix = None
            for br_ref, g_ref, lo, width in parts:
                t = _sigmoid(g_ref[row0:, :]) * _tail_fix(br_ref[...], w_ref[lo:lo + width, :], tail)
                fix = t if fix is None else fix + t
            o_ref[row0:, :] += fix

        ob_ref[0][...] = o_ref[...].astype(BF16)


def _merge(branches, p, w_branch, layer, *, tm, tn, tail):
    m = p.shape[0]
    gl_blk = OFF_GL // tn
    per = D_MODEL // tn
    br_specs = [pl.BlockSpec((tm, w), lambda i, j: (i, 0)) for w in BRANCH_SECTIONS]
    gl_specs = [pl.BlockSpec((tm, tn), lambda i, j, b=b: (i, gl_blk + b * per + j)) for b in range(N_BRANCH)]
    return pl.pallas_call(
        functools.partial(_merge_kernel, tail=tail),
        grid=(m // tm, D_MODEL // tn),
        in_specs=br_specs + gl_specs + [pl.BlockSpec((None, MIX_WIDTH, tn), lambda i, j: (layer, 0, j))],
        out_specs=[pl.BlockSpec((tm, tn), lambda i, j: (i, j))] * (1 if tail is None else 2),
        out_shape=[jax.ShapeDtypeStruct((m, D_MODEL), BF16)] if tail is None else
        [jax.ShapeDtypeStruct((m, D_MODEL), F32), jax.ShapeDtypeStruct((m, D_MODEL), BF16)],
        compiler_params=_cp("parallel", "parallel"), name="merge",
    )(*branches, p, p, p, p, w_branch)


def _sink_col(sink_ref, layer, kh, rows_per_head):
    return jnp.concatenate([jnp.full((rows_per_head, 1), sink_ref[layer, kh * GQA_GROUP + g], F32)
                            for g in range(GQA_GROUP)], axis=0)


def _dot_nt(a, b):
    return lax.dot_general(a, b, (((1,), (1,)), ((), ())), preferred_element_type=F32)


def _swa_prompt_kernel(sink_ref, q_ref, kc_ref, kp_ref, vc_ref, vp_ref, o_ref, *, layer, precise):
    n = pl.program_id(1)
    w = WINDOW
    q = q_ref[...]
    qi = jnp.bitwise_and(lax.broadcasted_iota(jnp.int32, (GQA_GROUP * w, 2 * w), 0), w - 1)
    sj = lax.broadcasted_iota(jnp.int32, (GQA_GROUP * w, 2 * w), 1)
    valid = (sj >= qi) & (sj <= qi + w) & ((sj >= w) | (n > 0))

    def run(three_pass):
        outs = []
        for kh in range(N_KV_HEADS):
            sl = slice(kh * HEAD_DIM, (kh + 1) * HEAD_DIM)
            k2 = jnp.concatenate([kp_ref[:, sl], kc_ref[:, sl]], axis=0)
            v2 = jnp.concatenate([vp_ref[:, sl], vc_ref[:, sl]], axis=0)
            q4 = jnp.concatenate([q[:, (kh * GQA_GROUP + g) * HEAD_DIM:(kh * GQA_GROUP + g + 1) * HEAD_DIM]
                                  for g in range(GQA_GROUP)], axis=0) * ATTN_SCALE
            s = _mxu(_dot_nt, q4, k2, three_pass)
            s = jnp.where(valid, s, NEG_INF)
            sk = _sink_col(sink_ref, layer, kh, w)
            m = jnp.maximum(jnp.max(s, axis=-1, keepdims=True), sk)
            e = jnp.exp(s - m)
            p = e / (jnp.sum(e, axis=-1, keepdims=True) + jnp.exp(sk - m))
            o4 = _bdot(p, v2, three_pass)
            outs += [o4[g * w:(g + 1) * w] for g in range(GQA_GROUP)]
        o_ref[...] = jnp.concatenate(outs, axis=1).astype(o_ref.dtype)

    if not precise:
        run(False)
    else:
        last = n >= pl.num_programs(1) - PRECISE_TAIL // w

        @pl.when(last)
        def _():
            run(True)

        @pl.when(jnp.logical_not(last))
        def _():
            run(False)


def _swa_prompt(p, sinks, layer, batch, seq, precise):
    nb = seq // WINDOW
    kblk, vblk = OFF_K // KV_WIDTH, OFF_V // KV_WIDTH

    def cur(col):
        return lambda b, n: (b * nb + n, col)

    def prev(col):
        return lambda b, n: (b * nb + jnp.maximum(n - 1, 0), col)

    return pl.pallas_call(
        functools.partial(_swa_prompt_kernel, layer=layer, precise=precise),
        grid=(batch, nb),
        in_specs=[pl.BlockSpec(memory_space=pltpu.SMEM),
                  pl.BlockSpec((WINDOW, ATTN_WIDTH), cur(0)),
                  pl.BlockSpec((WINDOW, KV_WIDTH), cur(kblk)), pl.BlockSpec((WINDOW, KV_WIDTH), prev(kblk)),
                  pl.BlockSpec((WINDOW, KV_WIDTH), cur(vblk)), pl.BlockSpec((WINDOW, KV_WIDTH), prev(vblk))],
        out_specs=pl.BlockSpec((WINDOW, ATTN_WIDTH), lambda b, n: (b * nb + n, 0)),
        out_shape=jax.ShapeDtypeStruct((batch * seq, ATTN_WIDTH), _act_dtype(precise)),
        compiler_params=_cp("parallel", "parallel"), name="swa_prompt",
    )(sinks, p, p, p, p, p)


def _qk(a, b):
    return jnp.einsum("bqd,bkd->bqk", a, b, preferred_element_type=F32)


def _pv(a, b):
    return jnp.einsum("bqk,bkd->bqd", a, b, preferred_element_type=F32)


def _swa_sample_kernel(sink_ref, q_ref, kn_ref, vn_ref, kc_ref, vc_ref, o_ref, ko_ref, vo_ref, *, layer, steps, wbuf,
                       precise):
    q = q_ref[...]
    kn, vn = kn_ref[...], vn_ref[...]
    kc, vc = kc_ref[...], vc_ref[...]
    ko_ref[:, :wbuf - steps, :] = kc[:, steps:, :]
    ko_ref[:, wbuf - steps:, :] = kn
    vo_ref[:, :wbuf - steps, :] = vc[:, steps:, :]
    vo_ref[:, wbuf - steps:, :] = vn
    bb = q.shape[0]
    rows = GQA_GROUP * steps
    t_c = lax.rem(lax.broadcasted_iota(jnp.int32, (bb, rows, wbuf), 1), steps)
    j_c = lax.broadcasted_iota(jnp.int32, (bb, rows, wbuf), 2)
    dist_c = t_c + wbuf - j_c
    valid_c = (dist_c >= 0) & (dist_c <= WINDOW)
    t_n = lax.rem(lax.broadcasted_iota(jnp.int32, (bb, rows, steps), 1), steps)
    j_n = lax.broadcasted_iota(jnp.int32, (bb, rows, steps), 2)
    valid_n = (t_n - j_n >= 0) & (t_n - j_n <= WINDOW)
    outs = [None] * N_HEADS
    for kh in range(N_KV_HEADS):
        sl = slice(kh * HEAD_DIM, (kh + 1) * HEAD_DIM)
        qg = jnp.concatenate([q[:, :, (kh * GQA_GROUP + g) * HEAD_DIM:(kh * GQA_GROUP + g + 1) * HEAD_DIM]
                              for g in range(GQA_GROUP)], axis=1)
        s_c = _mxu(_qk, qg, kc[:, :, sl], precise) * ATTN_SCALE
        s_n = _mxu(_qk, qg, kn[:, :, sl], precise) * ATTN_SCALE
        s_c = jnp.where(valid_c, s_c, NEG_INF)
        s_n = jnp.where(valid_n, s_n, NEG_INF)
        sk = _sink_col(sink_ref, layer, kh, steps)[None]
        m = jnp.maximum(jnp.maximum(jnp.max(s_c, axis=-1, keepdims=True), jnp.max(s_n, axis=-1, keepdims=True)), sk)
        e_c, e_n = jnp.exp(s_c - m), jnp.exp(s_n - m)
        den = jnp.sum(e_c, axis=-1, keepdims=True) + jnp.sum(e_n, axis=-1, keepdims=True) + jnp.exp(sk - m)
        o = _mxu(_pv, e_c / den, vc[:, :, sl], precise) + _mxu(_pv, e_n / den, vn[:, :, sl], precise)
        for g in range(GQA_GROUP):
            outs[kh * GQA_GROUP + g] = o[:, g * steps:(g + 1) * steps, :]
    o_ref[...] = jnp.concatenate(outs, axis=2).astype(o_ref.dtype)


def _swa_sample(p3, cache_k, cache_v, sinks, layer, precise, *, bb=8):
    batch, steps, _ = p3.shape
    wbuf = cache_k.shape[2]
    kblk, vblk = OFF_K // KV_WIDTH, OFF_V // KV_WIDTH
    cache_spec = pl.BlockSpec((None, bb, wbuf, KV_WIDTH), lambda i: (layer, i, 0, 0))
    new_spec = pl.BlockSpec((bb, wbuf, KV_WIDTH), lambda i: (i, 0, 0))
    return pl.pallas_call(
        functools.partial(_swa_sample_kernel, layer=layer, steps=steps, wbuf=wbuf, precise=precise),
        grid=(batch // bb,),
        in_specs=[pl.BlockSpec(memory_space=pltpu.SMEM),
                  pl.BlockSpec((bb, steps, ATTN_WIDTH), lambda i: (i, 0, 0)),
                  pl.BlockSpec((bb, steps, KV_WIDTH), lambda i: (i, 0, kblk)),
                  pl.BlockSpec((bb, steps, KV_WIDTH), lambda i: (i, 0, vblk)),
                  cache_spec, cache_spec],
        out_specs=[pl.BlockSpec((bb, steps, ATTN_WIDTH), lambda i: (i, 0, 0)), new_spec, new_spec],
        out_shape=[jax.ShapeDtypeStruct((batch, steps, ATTN_WIDTH), _act_dtype(precise)),
                   jax.ShapeDtypeStruct((batch, wbuf, KV_WIDTH), F32),
                   jax.ShapeDtypeStruct((batch, wbuf, KV_WIDTH), F32)],
        compiler_params=_cp("parallel"), name="swa_sample",
    )(sinks, p3, p3, p3, cache_k, cache_v)


def _shift_rows(x, k):
    rows = lax.broadcasted_iota(jnp.int32, x.shape, 0)
    return jnp.where(rows >= k, pltpu.roll(x, k, axis=0), 0.0)


def _pool_prompt_kernel(u_ref, w_ref, ls_ref, o_ref, *, precise):
    u = u_ref[...]
    t = u.shape[0]
    pos1 = (lax.broadcasted_iota(jnp.int32, (t, 1), 0) + 1).astype(F32)
    sums = {1: u}
    win = 1
    while win < max(POOL_WINDOWS):
        sums[2 * win] = sums[win] + _shift_rows(sums[win], win)
        win *= 2
    outs = []
    for gi, win in enumerate(POOL_WINDOWS):
        sl = slice(gi * POOL_GW, (gi + 1) * POOL_GW)
        cnt = jnp.minimum(float(win), pos1)
        d = sums[win][:, sl] / cnt - u[:, sl]
        outs.append(_bdot(d, w_ref[gi], precise))
    o_ref[...] = (jnp.concatenate(outs, axis=1) * ls_ref[...]).astype(o_ref.dtype)


def _pool_prompt(p, w_pool, ls_pool, layer, batch, seq, precise):
    gw = POOL_GW
    return pl.pallas_call(
        functools.partial(_pool_prompt_kernel, precise=precise),
        grid=(batch,),
        in_specs=[pl.BlockSpec((seq, POOL_WIDTH), lambda b: (b, OFF_U // POOL_WIDTH)),
                  pl.BlockSpec((None, len(POOL_WINDOWS), gw, gw), lambda b: (layer, 0, 0, 0)),
                  pl.BlockSpec((None, 1, POOL_WIDTH), lambda b: (layer, 0, 0))],
        out_specs=pl.BlockSpec((seq, POOL_WIDTH), lambda b: (b, 0)),
        out_shape=jax.ShapeDtypeStruct((batch * seq, POOL_WIDTH), _act_dtype(precise)),
        compiler_params=_cp("parallel"), name="pool_prompt",
    )(p, w_pool, ls_pool.reshape(ls_pool.shape[0], 1, POOL_WIDTH))


def _pool_sample_kernel(u_ref, past_ref, w_ref, ls_ref, o_ref, new_ref, *, pos0, precise):
    steps, hist = u_ref.shape[0], past_ref.shape[0]
    full = [past_ref[i] for i in range(hist)] + [u_ref[i] for i in range(steps)]
    for i in range(hist):
        new_ref[i] = full[steps + i]
    ds = [[] for _ in POOL_WINDOWS]
    for t in range(steps):
        for gi, win in enumerate(POOL_WINDOWS):
            sl = slice(gi * POOL_GW, (gi + 1) * POOL_GW)
            wsum = full[hist + t][:, sl]
            for s in range(1, win):
                wsum = wsum + full[hist + t - s][:, sl]
            cnt = float(min(win, pos0 + t + 1))
            ds[gi].append(wsum / cnt - full[hist + t][:, sl])
    ys = [_bdot(jnp.concatenate(ds[gi], axis=0), w_ref[gi], precise) for gi in range(len(POOL_WINDOWS))]
    y = jnp.concatenate(ys, axis=1) * ls_ref[...]
    nb = u_ref.shape[1]
    for t in range(steps):
        o_ref[t] = y[t * nb:(t + 1) * nb].astype(o_ref.dtype)


def _pool_sample(u_t, past_t, w_pool, ls_pool, layer, pos0, precise):
    steps, nb, _ = u_t.shape
    return pl.pallas_call(
        functools.partial(_pool_sample_kernel, pos0=pos0, precise=precise),
        grid=(1,),
        in_specs=[pl.BlockSpec(u_t.shape, lambda i: (0, 0, 0)),
                  pl.BlockSpec(past_t.shape, lambda i: (0, 0, 0)),
                  pl.BlockSpec((None, len(POOL_WINDOWS), POOL_GW, POOL_GW), lambda i: (layer, 0, 0, 0)),
                  pl.BlockSpec((None, 1, POOL_WIDTH), lambda i: (layer, 0, 0))],
        out_specs=[pl.BlockSpec(u_t.shape, lambda i: (0, 0, 0)), pl.BlockSpec(past_t.shape, lambda i: (0, 0, 0))],
        out_shape=[jax.ShapeDtypeStruct(u_t.shape, _act_dtype(precise)), jax.ShapeDtypeStruct(past_t.shape, F32)],
        compiler_params=_cp("arbitrary"), name="pool_sample",
    )(u_t, past_t, w_pool, ls_pool.reshape(ls_pool.shape[0], 1, POOL_WIDTH))


def _conv_prompt_kernel(cb_ref, cc_ref, cx_ref, w_ref, o_ref, new_ref):
    z = cc_ref[...] * cx_ref[...]
    w = w_ref[...]
    y = w[CONV_K - 1:CONV_K] * z
    for j in range(1, CONV_K):
        y = y + w[CONV_K - 1 - j:CONV_K - j] * _shift_rows(z, j)
    o_ref[...] = (cb_ref[...] * y).astype(o_ref.dtype)
    new_ref[...] = z[z.shape[0] - (CONV_K - 1):]


def _conv_prompt(p, conv_w, layer, batch, seq, precise, *, tc=256):
    nc = CONV_WIDTH // tc

    def col(off):
        return lambda b, c: (b, off // tc + c)

    return pl.pallas_call(
        _conv_prompt_kernel,
        grid=(batch, nc),
        in_specs=[pl.BlockSpec((seq, tc), col(OFF_CB)), pl.BlockSpec((seq, tc), col(OFF_CC)),
                  pl.BlockSpec((seq, tc), col(OFF_CX)),
                  pl.BlockSpec((None, CONV_K, tc), lambda b, c: (layer, 0, c))],
        out_specs=[pl.BlockSpec((seq, tc), lambda b, c: (b, c)),
                   pl.BlockSpec((None, CONV_K - 1, tc), lambda b, c: (b, 0, c))],
        out_shape=[jax.ShapeDtypeStruct((batch * seq, CONV_WIDTH), _act_dtype(precise)),
                   jax.ShapeDtypeStruct((batch, CONV_K - 1, CONV_WIDTH), F32)],
        compiler_params=_cp("parallel", "parallel"), name="conv_prompt",
    )(p, p, p, conv_w)


def _conv_sample_kernel(cb_ref, cc_ref, cx_ref, past_ref, w_ref, o_ref, new_ref):
    steps, hist = cb_ref.shape[0], past_ref.shape[0]
    w = w_ref[...]
    full = [past_ref[i] for i in range(hist)] + [cc_ref[t] * cx_ref[t] for t in range(steps)]
    for t in range(steps):
        y = w[0:1] * full[t]
        for j in range(1, CONV_K):
            y = y + w[j:j + 1] * full[t + j]
        o_ref[t] = (cb_ref[t] * y).astype(o_ref.dtype)
    for i in range(hist):
        new_ref[i] = full[steps + i]


def _conv_sample(cb_t, cc_t, cx_t, past_t, conv_w, layer, precise):
    full3 = lambda shape: pl.BlockSpec(shape, lambda i: (0, 0, 0))
    return pl.pallas_call(
        _conv_sample_kernel,
        grid=(1,),
        in_specs=[full3(cb_t.shape), full3(cc_t.shape), full3(cx_t.shape), full3(past_t.shape),
                  pl.BlockSpec((None, CONV_K, CONV_WIDTH), lambda i: (layer, 0, 0))],
        out_specs=[full3(cb_t.shape), full3(past_t.shape)],
        out_shape=[jax.ShapeDtypeStruct(cb_t.shape, _act_dtype(precise)), jax.ShapeDtypeStruct(past_t.shape, F32)],
        compiler_params=_cp("arbitrary"), name="conv_sample",
    )(cb_t, cc_t, cx_t, past_t, conv_w)


def _head_sum(x):
    rows = x.shape[0]
    return jnp.concatenate(
        [jnp.broadcast_to(jnp.sum(x[:, h * RWKV_HEAD:(h + 1) * RWKV_HEAD], axis=-1, keepdims=True), (rows, RWKV_HEAD))
         for h in range(RWKV_HEADS)], axis=1)


def _softplus(x):
    return jnp.maximum(x, 0.0) + jnp.log(1.0 + jnp.exp(-jnp.abs(x)))


def _rwkv_pre_core(cur, sh, mu, w0, w2, a0, a2, g2, k_k, k_a, outs, precise):
    xr, xk, xv, xwa, xg = [c + (s - c) * m for c, s, m in zip(cur, sh, mu)]
    wd, ad = xwa[:, :DECAY_LORA], xwa[:, DECAY_LORA:]
    w_log = -_softplus(-(w0 + _bdot(jnp.tanh(wd), w2, precise))) - 0.5
    log_decay = -jnp.exp(w_log)
    a = _sigmoid(a0 + _bdot(ad, a2, precise))
    g = _bdot(_sigmoid(xg), g2, precise)
    kk = xk * k_k
    kk = kk / jnp.maximum(jnp.sqrt(_head_sum(kk * kk)), 1e-12)
    kf = xk * (1.0 + (a - 1.0) * k_a)
    r_ref, w_ref, k_ref, v_ref, a_ref, b_ref, g_ref = outs
    r_ref[...] = xr
    w_ref[...] = log_decay
    k_ref[...] = kf
    v_ref[...] = xv
    a_ref[...] = -kk
    b_ref[...] = kk * a
    g_ref[...] = g


_PRE_WIDTHS = (RWKV_WIDTH, RWKV_WIDTH, RWKV_WIDTH, DECAY_LORA + AAA_LORA, GATE_LORA)
_PRE_OFFS = (0, RWKV_WIDTH, 2 * RWKV_WIDTH, 3 * RWKV_WIDTH, 3 * RWKV_WIDTH + DECAY_LORA + AAA_LORA)
_HALO = 8


def _rwkv_pre_prompt_kernel(*refs, precise):
    cur_refs, halo_refs, mu_refs = refs[0:5], refs[5:10], refs[10:15]
    w0, w2, a0, a2, g2, k_k, k_a = [r[...] for r in refs[15:22]]
    outs = refs[22:]
    first = pl.program_id(1) == 0
    cur, sh = [], []
    for c_ref, h_ref in zip(cur_refs, halo_refs):
        c = c_ref[...]
        prev_row = jnp.where(first, 0.0, h_ref[_HALO - 1:_HALO, :])
        rows = lax.broadcasted_iota(jnp.int32, c.shape, 0)
        sh.append(jnp.where(rows == 0, prev_row, pltpu.roll(c, 1, axis=0)))
        cur.append(c)
    _rwkv_pre_core(cur, sh, [m[...] for m in mu_refs], w0, w2, a0, a2, g2, k_k, k_a, outs, precise)


def _rwkv_param_specs(layer):
    def spec(shape):
        return pl.BlockSpec((None,) + shape, lambda *idx: (layer,) + (0,) * len(shape))

    return [spec((1, RWKV_WIDTH)), spec((DECAY_LORA, RWKV_WIDTH)), spec((1, RWKV_WIDTH)),
            spec((AAA_LORA, RWKV_WIDTH)), spec((GATE_LORA, RWKV_WIDTH)), spec((1, RWKV_WIDTH)), spec((1, RWKV_WIDTH))]


def _rwkv_params(prm):
    depth = prm["w0"].shape[0]
    r3 = lambda a: a.reshape(depth, 1, RWKV_WIDTH)
    return [r3(prm["w0"]), prm["w2"], r3(prm["a0"]), prm["a2"], prm["g2"], r3(prm["kk"]), r3(prm["ka"])]


def _rwkv_pre_prompt(p, prm, layer, batch, seq, precise, *, tt=512):
    nt = seq // tt
    cur_specs, halo_specs, mu_specs = [], [], []
    for w, off in zip(_PRE_WIDTHS, _PRE_OFFS):
        cb = (OFF_PC + off) // w
        cur_specs.append(pl.BlockSpec((tt, w), lambda b, t, cb=cb: (b * nt + t, cb)))
        halo_specs.append(pl.BlockSpec(
            (_HALO, w), lambda b, t, cb=cb: (jnp.maximum((b * nt + t) * (tt // _HALO) - 1, 0), cb)))
        mu_specs.append(pl.BlockSpec((None, 1, w), lambda b, t, mb=off // w: (layer, 0, mb)))
    out_spec = pl.BlockSpec((tt, RWKV_WIDTH), lambda b, t: (b * nt + t, 0))
    mu3 = prm["mu"].reshape(prm["mu"].shape[0], 1, RWKV_PROJ)
    return pl.pallas_call(
        functools.partial(_rwkv_pre_prompt_kernel, precise=precise),
        grid=(batch, nt),
        in_specs=cur_specs + halo_specs + mu_specs + _rwkv_param_specs(layer),
        out_specs=[out_spec] * 7,
        out_shape=[jax.ShapeDtypeStruct((batch * seq, RWKV_WIDTH), F32)] * 7,
        compiler_params=_cp("parallel", "parallel"), name="rwkv_pre_prompt",
    )(*([p] * 10), *([mu3] * 5), *_rwkv_params(prm))


def _rwkv_pre_sample_kernel(*refs, precise):
    cur = [r[...] for r in refs[0:5]]
    sh = [r[...] for r in refs[5:10]]
    mu = [r[...] for r in refs[10:15]]
    w0, w2, a0, a2, g2, k_k, k_a = [r[...] for r in refs[15:22]]
    _rwkv_pre_core(cur, sh, mu, w0, w2, a0, a2, g2, k_k, k_a, refs[22:], precise)


def _rwkv_pre_sample(pc, pc_shifted, prm, layer, precise):
    rows = pc.shape[0]
    cur_specs, mu_specs = [], []
    for w, off in zip(_PRE_WIDTHS, _PRE_OFFS):
        cur_specs.append(pl.BlockSpec((rows, w), lambda i, cb=off // w: (0, cb)))
        mu_specs.append(pl.BlockSpec((None, 1, w), lambda i, mb=off // w: (layer, 0, mb)))
    out_spec = pl.BlockSpec((rows, RWKV_WIDTH), lambda i: (0, 0))
    mu3 = prm["mu"].reshape(prm["mu"].shape[0], 1, RWKV_PROJ)
    return pl.pallas_call(
        functools.partial(_rwkv_pre_sample_kernel, precise=precise),
        grid=(1,),
        in_specs=cur_specs + cur_specs + mu_specs + _rwkv_param_specs(layer),
        out_specs=[out_spec] * 7,
        out_shape=[jax.ShapeDtypeStruct((rows, RWKV_WIDTH), F32)] * 7,
        compiler_params=_cp("arbitrary"), name="rwkv_pre_sample",
    )(*([pc] * 5), *([pc_shifted] * 5), *([mu3] * 5), *_rwkv_params(prm))


def _split_heads(x):
    return jnp.stack([x[:, h * RWKV_HEAD:(h + 1) * RWKV_HEAD] for h in range(RWKV_HEADS)], axis=0)


def _join_heads(x):
    return jnp.concatenate([x[h] for h in range(RWKV_HEADS)], axis=1)


def _wkv_steps_kernel(r_ref, ld_ref, k_ref, v_ref, a_ref, b_ref, s0_ref, y_ref, sf_ref, *, steps, bb):
    n = RWKV_HEAD
    eye = lax.broadcasted_iota(jnp.int32, (n, n), 0) == lax.broadcasted_iota(jnp.int32, (n, n), 1)
    for i in range(bb):
        s = s0_ref[i]
        seqs = [_split_heads(ref[i]) for ref in (r_ref, ld_ref, k_ref, v_ref, a_ref, b_ref)]
        out_rows = []
        for t in range(steps):
            r, ld, k, v, a, b = [x[:, t:t + 1, :] for x in seqs]
            sa = jnp.sum(s * a, axis=-1, keepdims=True)
            vcol = jnp.sum(jnp.where(eye, v, 0.0), axis=-1, keepdims=True)
            s = s * jnp.exp(ld) + sa * b + vcol * k
            ycol = jnp.sum(s * r, axis=-1, keepdims=True)
            out_rows.append(jnp.sum(jnp.where(eye, ycol, 0.0), axis=1, keepdims=True))
        y_ref[i] = _join_heads(jnp.concatenate(out_rows, axis=1))
        sf_ref[i] = s


def _wkv_steps(seqs, s0, *, bb=4):
    batch, t, _ = seqs[0].shape
    seq_spec = pl.BlockSpec((bb, t, RWKV_WIDTH), lambda i: (i, 0, 0))
    st_spec = pl.BlockSpec((bb, RWKV_HEADS, RWKV_HEAD, RWKV_HEAD), lambda i: (i, 0, 0, 0))
    return pl.pallas_call(
        functools.partial(_wkv_steps_kernel, steps=t, bb=bb),
        grid=(batch // bb,),
        in_specs=[seq_spec] * 6 + [st_spec],
        out_specs=[seq_spec, st_spec],
        out_shape=[jax.ShapeDtypeStruct((batch, t, RWKV_WIDTH), F32), jax.ShapeDtypeStruct(s0.shape, F32)],
        compiler_params=_cp("parallel"), name="wkv_steps",
    )(*seqs, s0)


WKV_CHUNK = 64


def _e_nt(a, b):
    return jnp.einsum("hqd,hkd->hqk", a, b, preferred_element_type=F32)


def _e_nn(a, b):
    return jnp.einsum("hqk,hkd->hqd", a, b, preferred_element_type=F32)


def _wkv_chunk_kernel(r_ref, ld_ref, k_ref, v_ref, a_ref, b_ref, y_ref, sf_ref, s_scr):
    c = WKV_CHUNK

    @pl.when(pl.program_id(1) == 0)
    def _():
        s_scr[...] = jnp.zeros_like(s_scr)

    ld = ld_ref[...]
    cum = ld
    k = 1
    while k < c:
        cum = cum + _shift_rows(cum, k)
        k *= 2
    e_pos, e_prev, e_neg = jnp.exp(cum), jnp.exp(cum - ld), jnp.exp(-cum)
    at = _split_heads(a_ref[...] * e_prev)
    rt = _split_heads(r_ref[...] * e_pos)
    bt = _split_heads(b_ref[...] * e_neg)
    kt = _split_heads(k_ref[...] * e_neg)
    v = _split_heads(v_ref[...])
    lam = _split_heads(e_pos[c - 1:c, :])

    ti = lax.broadcasted_iota(jnp.int32, (c, c), 0)
    si = lax.broadcasted_iota(jnp.int32, (c, c), 1)
    strict = ti > si

    def blockmask(size):
        same = (ti // size) == (si // size)
        return strict & same & ((ti // (size // 2)) != (si // (size // 2)))

    def run(precise):
        dot_nt = functools.partial(_mxu, _e_nt, precise=precise)
        dot_nn = functools.partial(_mxu, _e_nn, precise=precise)

        def dot_tn(x, y):
            return dot_nn(jnp.swapaxes(x, 1, 2), y)

        ar = jnp.concatenate([at, rt], axis=1)
        g_b = dot_nt(ar, bt)
        g_k = dot_nt(ar, kt)
        n_ab = jnp.where(strict, g_b[:, :c, :], 0.0)
        n_ak = jnp.where(strict, g_k[:, :c, :], 0.0)
        m_rb = jnp.where(ti >= si, g_b[:, c:, :], 0.0)
        m_rk = jnp.where(ti >= si, g_k[:, c:, :], 0.0)

        base = 8
        n8 = jnp.where((ti // base) == (si // base), n_ab, 0.0)
        eye = (ti == si).astype(F32)
        n8_2 = dot_nn(n8, n8)
        n8_4 = dot_nn(n8_2, n8_2)
        t_inv = eye + n8
        t_inv = t_inv + dot_nn(t_inv, n8_2)
        t_inv = t_inv + dot_nn(t_inv, n8_4)
        size = 2 * base
        while size <= c:
            off = jnp.where(blockmask(size), n_ab, 0.0)
            t_inv = t_inv + dot_nn(dot_nn(t_inv, off), t_inv)
            size *= 2

        wv = dot_nn(n_ak, v)
        a_bar = dot_nn(t_inv, at)
        u_bar = dot_nn(t_inv, wv)
        r_bar = rt + dot_nn(m_rb, a_bar)
        y_bar = dot_nn(m_rb, u_bar) + dot_nn(m_rk, v)
        phi = dot_tn(a_bar, bt)
        psi = dot_tn(jnp.concatenate([u_bar, v], axis=1), jnp.concatenate([bt, kt], axis=1))

        s0 = s_scr[...]
        y_ref[...] = _join_heads(dot_nt(r_bar, s0) + y_bar)
        s_scr[...] = (s0 + dot_nn(s0, phi) + psi) * lam

    last = pl.program_id(1) >= pl.num_programs(1) - PRECISE_TAIL // c

    @pl.when(last)
    def _():
        run(True)

    @pl.when(jnp.logical_not(last))
    def _():
        run(False)

    @pl.when(pl.program_id(1) == pl.num_programs(1) - 1)
    def _():
        sf_ref[...] = s_scr[...]


def _wkv_chunked(seqs):
    batch, t, _ = seqs[0].shape
    c = WKV_CHUNK
    seq_spec = pl.BlockSpec((None, c, RWKV_WIDTH), lambda b, i: (b, i, 0))
    st_shape = (batch, RWKV_HEADS, RWKV_HEAD, RWKV_HEAD)
    st_spec = pl.BlockSpec((None,) + st_shape[1:], lambda b, i: (b, 0, 0, 0))
    return pl.pallas_call(
        _wkv_chunk_kernel,
        grid=(batch, t // c),
        in_specs=[seq_spec] * 6,
        out_specs=[seq_spec, st_spec],
        out_shape=[jax.ShapeDtypeStruct((batch, t, RWKV_WIDTH), F32), jax.ShapeDtypeStruct(st_shape, F32)],
        scratch_shapes=[pltpu.VMEM(st_shape[1:], F32)],
        compiler_params=_cp("parallel", "arbitrary"), name="wkv_chunk",
    )(*seqs)


def _rwkv_post_kernel(y_ref, r_ref, k_ref, v_ref, g_ref, rk_ref, lg_ref, lb_ref, o_ref):
    y = y_ref[...]
    inv = 1.0 / RWKV_HEAD
    mean = _head_sum(y) * inv
    yc = y - mean
    var = _head_sum(yc * yc) * inv
    yn = yc * lax.rsqrt(var + RWKV_LN_EPS) * lg_ref[...] + lb_ref[...]
    v = v_ref[...]
    bonus = _head_sum(r_ref[...] * k_ref[...] * rk_ref[...]) * v
    o_ref[...] = ((yn + bonus) * g_ref[...]).astype(o_ref.dtype)


def _rwkv_post(y, r, k, v, g, prm, layer, precise, *, tm):
    rows = y.shape[0]
    depth = prm["rk"].shape[0]
    row = pl.BlockSpec((tm, RWKV_WIDTH), lambda i: (i, 0))
    vec = pl.BlockSpec((None, 1, RWKV_WIDTH), lambda i: (layer, 0, 0))
    r3 = lambda a: a.reshape(depth, 1, RWKV_WIDTH)
    return pl.pallas_call(
        _rwkv_post_kernel,
        grid=(rows // tm,),
        in_specs=[row] * 5 + [vec] * 3,
        out_specs=row,
        out_shape=jax.ShapeDtypeStruct((rows, RWKV_WIDTH), _act_dtype(precise)),
        compiler_params=_cp("parallel"), name="rwkv_post",
    )(y, r, k, v, g, r3(prm["rk"]), r3(prm["ln_g"]), r3(prm["ln_b"]))


MOE_TM = 256


def _dispatch_plan(e_idx, tm):
    n = e_idx.shape[0]
    pairs = n * TOP_K
    e_flat = e_idx.reshape(pairs)
    onehot = (e_flat[:, None] == jnp.arange(N_EXPERTS, dtype=jnp.int32)[None, :]).astype(jnp.int32)
    csum = jnp.cumsum(onehot, axis=0)
    counts = csum[-1]
    padded = ((counts + tm - 1) // tm) * tm
    ends = jnp.cumsum(padded)
    starts = ends - padded
    pos = jnp.sum(onehot * (csum - 1 + starts[None, :]), axis=1)
    nt = (pairs + N_EXPERTS * (tm - 1)) // tm
    tile_start = jnp.arange(nt, dtype=jnp.int32) * tm
    tile_e = jnp.minimum(jnp.sum((tile_start[:, None] >= ends[None, :]).astype(jnp.int32), axis=1), N_EXPERTS - 1)
    tile_nv = jnp.clip(jnp.take(starts + counts, tile_e) - tile_start, 0, tm)
    src = jnp.zeros((nt * tm,), jnp.int32).at[pos].set(jnp.arange(pairs, dtype=jnp.int32) // TOP_K)
    return jnp.stack([tile_e, tile_nv]).astype(jnp.int32), src.reshape(nt, 1, tm), pos


def _moe_gather_kernel(meta_ref, src_ref, h_hbm, o_ref, sem):
    nv = meta_ref[1, pl.program_id(0)]
    tm = o_ref.shape[0]

    @pl.when(nv > 0)
    def _():
        def issue(r, c):
            pltpu.make_async_copy(h_hbm.at[pl.ds(src_ref[0, r], 1)], o_ref.at[pl.ds(r, 1)], sem).start()
            return c

        lax.fori_loop(0, tm, issue, 0, unroll=8)
        pltpu.make_async_copy(h_hbm.at[pl.ds(0, tm)], o_ref, sem).wait()

    @pl.when(nv < tm)
    def _():
        rows = lax.broadcasted_iota(jnp.int32, o_ref.shape, 0)
        o_ref[...] = jnp.where(rows < nv, o_ref[...], 0.0)


def _moe_gather(meta, src, h):
    nt, _, tm = src.shape
    d = h.shape[1]
    gs = pltpu.PrefetchScalarGridSpec(
        num_scalar_prefetch=1, grid=(nt,),
        in_specs=[pl.BlockSpec((None, 1, tm), lambda j, m: (j, 0, 0), memory_space=pltpu.SMEM),
                  pl.BlockSpec(memory_space=pl.ANY)],
        out_specs=pl.BlockSpec((tm, d), lambda j, m: (j, 0)),
        scratch_shapes=[pltpu.SemaphoreType.DMA(())])
    return pl.pallas_call(
        _moe_gather_kernel, grid_spec=gs, out_shape=jax.ShapeDtypeStruct((nt * tm, d), F32),
        compiler_params=_cp("arbitrary"), name="moe_gather",
    )(meta, src, h)


def _moe_experts_kernel(meta_ref, xs_ref, wg_ref, wu_ref, wd_ref, y_ref, wg_bf, wu_bf, wd_bf):
    j = pl.program_id(0)
    e = meta_ref[0, j]
    nv = meta_ref[1, j]
    new_expert = (j == 0) | (e != meta_ref[0, jnp.maximum(j - 1, 0)])

    @pl.when(new_expert)
    def _():
        wg_bf[...] = wg_ref[...].astype(BF16)
        wu_bf[...] = wu_ref[...].astype(BF16)
        wd_bf[...] = wd_ref[...].astype(BF16)

    @pl.when(nv > 0)
    def _():
        x = xs_ref[...].astype(BF16)
        gate = _dot2(x, wg_bf[...])
        act = gate * _sigmoid(gate) * _dot2(x, wu_bf[...])
        y_ref[...] = _dot2(act.astype(BF16), wd_bf[...])

    @pl.when(nv == 0)
    def _():
        y_ref[...] = jnp.zeros_like(y_ref)


def _moe_experts(meta, xs, w_gate, w_up, w_down, layer):
    rows, d = xs.shape
    nt = meta.shape[1]
    tm = rows // nt
    once = dict(pipeline_mode=pl.Buffered(1))
    gs = pltpu.PrefetchScalarGridSpec(
        num_scalar_prefetch=1, grid=(nt,),
        in_specs=[pl.BlockSpec((tm, d), lambda j, m: (j, 0)),
                  pl.BlockSpec((None, None, d, D_EXPERT), lambda j, m: (layer, m[0, j], 0, 0), **once),
                  pl.BlockSpec((None, None, d, D_EXPERT), lambda j, m: (layer, m[0, j], 0, 0), **once),
                  pl.BlockSpec((None, None, D_EXPERT, d), lambda j, m: (layer, m[0, j], 0, 0), **once)],
        out_specs=pl.BlockSpec((tm, d), lambda j, m: (j, 0)),
        scratch_shapes=[pltpu.VMEM((d, D_EXPERT), BF16), pltpu.VMEM((d, D_EXPERT), BF16),
                        pltpu.VMEM((D_EXPERT, d), BF16)])
    return pl.pallas_call(
        _moe_experts_kernel, grid_spec=gs, out_shape=jax.ShapeDtypeStruct((rows, d), F32),
        compiler_params=_cp("arbitrary"), name="moe_experts",
    )(meta, xs, w_gate, w_up, w_down)


def _moe_combine_kernel(pos_ref, w_ref, y_hbm, m_ref, buf, sem):
    tt = m_ref.shape[0]

    def issue(r, c):
        pltpu.make_async_copy(y_hbm.at[pl.ds(pos_ref[0, r], 1)], buf.at[pl.ds(r, 1)], sem).start()
        return c

    lax.fori_loop(0, TOP_K * tt, issue, 0, unroll=8)
    pltpu.make_async_copy(y_hbm.at[pl.ds(0, TOP_K * tt)], buf, sem).wait()
    w = w_ref[...]
    m_ref[...] = w[:, 0:1] * buf[0:tt, :] + w[:, 1:2] * buf[tt:2 * tt, :]


def _moe_combine(pos, wts, y, *, tt):
    n = wts.shape[0]
    d = y.shape[1]
    pos_t = jnp.swapaxes(pos.reshape(n // tt, tt, TOP_K), 1, 2).reshape(n // tt, 1, TOP_K * tt)
    return pl.pallas_call(
        _moe_combine_kernel,
        grid=(n // tt,),
        in_specs=[pl.BlockSpec((None, 1, TOP_K * tt), lambda i: (i, 0, 0), memory_space=pltpu.SMEM),
                  pl.BlockSpec((tt, TOP_K), lambda i: (i, 0)),
                  pl.BlockSpec(memory_space=pl.ANY)],
        out_specs=pl.BlockSpec((tt, d), lambda i: (i, 0)),
        out_shape=jax.ShapeDtypeStruct((n, d), F32),
        scratch_shapes=[pltpu.VMEM((TOP_K * tt, d), F32), pltpu.SemaphoreType.DMA(())],
        compiler_params=_cp("arbitrary"), name="moe_combine",
    )(pos_t, wts, y)


def _moe(h_all, e_idx, wts, w_gate, w_up, w_down, layer):
    n = h_all.shape[0]
    meta, src, pos = _dispatch_plan(e_idx, MOE_TM)
    xs = _moe_gather(meta, src, h_all)
    y = _moe_experts(meta, xs, w_gate, w_up, w_down, layer)
    tt = 256
    while n % tt:
        tt //= 2
    return _moe_combine(pos, wts, y, tt=tt)


_SH1, _SC1, _GT1, _SH2, _SC2, _GT2 = range(6)


def kernel(x_prompt, x_sample, cache_swa_k, cache_swa_v, state_pool, state_rwkv_shift, state_rwkv_wkv, state_conv, c_prompt, c_sample, w_ada, b_ada, g_norm1, g_norm2, w_in, sinks, w_pool, ls_pool, rwkv_mu, rwkv_w0, rwkv_w2, rwkv_a0, rwkv_a2, rwkv_g2, rwkv_kk, rwkv_ka, rwkv_rk, rwkv_ln_g, rwkv_ln_b, conv_w, w_branch, w_out, w_router, b_router, w_gate, w_up, w_down, g_final):
    depth = w_in.shape[0]
    batch, seq, d = x_prompt.shape
    dbatch, dseq, _ = x_sample.shape
    wbuf = cache_swa_k.shape[2]
    np_rows, ns_rows = batch * seq, dbatch * dseq

    grp_p = _Group(np_rows, seq, 1)
    grp_s = _Group(ns_rows, ns_rows, ns_rows)
    assert seq <= 2048, "the big matmuls take one whole prompt sequence per row tile"
    tm_p, tm_s = seq, ns_rows
    te_p = min(512, seq)

    c_all = jnp.concatenate([c_prompt, c_sample], axis=0)
    wr_pad = jnp.pad(w_router, ((0, 0), (0, ROUTER_PAD - N_EXPERTS)))
    prm = dict(mu=rwkv_mu, w0=rwkv_w0, w2=rwkv_w2, a0=rwkv_a0, a2=rwkv_a2, g2=rwkv_g2, kk=rwkv_kk, ka=rwkv_ka,
               rk=rwkv_rk.reshape(depth, RWKV_WIDTH), ln_g=rwkv_ln_g, ln_b=rwkv_ln_b)
    cache_k = cache_swa_k.reshape(depth, dbatch, wbuf, KV_WIDTH)
    cache_v = cache_swa_v.reshape(depth, dbatch, wbuf, KV_WIDTH)

    xp = x_prompt.reshape(np_rows, d)
    xs = x_sample.reshape(ns_rows, d)
    st_p, st_s = [], []
    pend_p = pend_s = None
    for l in range(depth):
        precise = l == 0
        hd = (BF16, F32) if precise else (BF16,)
        mod = _ada(c_all, w_ada, b_ada, l, precise)
        mod_p = mod[:batch].reshape(batch, 1, 6 * d)
        mod_s = jnp.repeat(mod[batch:], dseq, axis=0).reshape(1, ns_rows, 6 * d)

        def first_norm(grp, x, pend, mod3, tm):
            if pend is None:
                return (x,) + tuple(_normx(grp, x, g_norm1[l], mod=(mod3, _SC1, _SH1), out_dtype=hd, tm=tm))
            return tuple(_normx(grp, x, g_norm1[l], add=pend + (_GT2,), mod=(mod3, _SC1, _SH1), emit_x=True,
                                out_dtype=hd, tm=tm))

        xp, hp, *hp32 = first_norm(grp_p, xp, pend_p, mod_p, te_p)
        xs, hs, *hs32 = first_norm(grp_s, xs, pend_s, mod_s, tm_s)

        def tails(tm, grp):
            if not precise:
                return None
            return (tm, 1) if grp is grp_s else (min(PRECISE_TAIL, tm), seq // tm)

        tail_p, tail_s = (min(PRECISE_TAIL, tm_p), tm_s) if precise else (0, 0)
        pp = _mm(hp, w_in, l, tm=tm_p, tn=512, a_f32=hp32[0] if precise else None, tail_rows=tail_p)
        ps = _mm(hs, w_in, l, tm=tm_s, tn=512, a_f32=hs32[0] if precise else None, tail_rows=tail_s)

        a_p = _swa_prompt(pp, sinks, l, batch, seq, precise)
        b_p = _pool_prompt(pp, w_pool, ls_pool, l, batch, seq, precise)
        d_p, conv_new_p = _conv_prompt(pp, conv_w, l, batch, seq, precise)
        pre_p = _rwkv_pre_prompt(pp, prm, l, batch, seq, precise, tt=te_p)
        r_p, w_p, k_p, v_p, ka_p, kb_p, g_p = pre_p
        as3 = lambda t: t.reshape(batch, seq, RWKV_WIDTH)
        y_p, wkv_new_p = _wkv_chunked([as3(t) for t in (r_p, w_p, k_p, v_p, ka_p, kb_p)])
        c_p = _rwkv_post(y_p.reshape(np_rows, RWKV_WIDTH), r_p, k_p, v_p, g_p, prm, l, precise, tm=te_p)
        pp3 = pp.reshape(batch, seq, IN_WIDTH)
        kw = min(WINDOW, seq)
        st_p.append((pp3[:, seq - kw:, OFF_K:OFF_K + KV_WIDTH].reshape(batch, kw, N_KV_HEADS, HEAD_DIM),
                     pp3[:, seq - kw:, OFF_V:OFF_V + KV_WIDTH].reshape(batch, kw, N_KV_HEADS, HEAD_DIM),
                     pp3[:, seq - POOL_BUF:, OFF_U:OFF_U + POOL_WIDTH],
                     pp3[:, seq - 1, OFF_PC:OFF_PC + RWKV_PROJ],
                     wkv_new_p, conv_new_p))

        ps3 = ps.reshape(dbatch, dseq, IN_WIDTH)
        a_s, k_new_s, v_new_s = _swa_sample(ps3, cache_k, cache_v, sinks, l, precise)
        tmaj = lambda t: jnp.swapaxes(t, 0, 1)
        b_s_t, pool_new_t = _pool_sample(tmaj(ps3[:, :, OFF_U:OFF_U + POOL_WIDTH]), tmaj(state_pool[l]),
                                         w_pool, ls_pool, l, PAST_LEN, precise)
        d_s_t, conv_new_t = _conv_sample(tmaj(ps3[:, :, OFF_CB:OFF_CB + CONV_WIDTH]),
                                         tmaj(ps3[:, :, OFF_CC:OFF_CC + CONV_WIDTH]),
                                         tmaj(ps3[:, :, OFF_CX:OFF_CX + CONV_WIDTH]), tmaj(state_conv[l]), conv_w, l,
                                         precise)
        pc_s3 = ps3[:, :, OFF_PC:OFF_PC + RWKV_PROJ]
        pc_shift = jnp.concatenate([state_rwkv_shift[l][:, None, :], pc_s3[:, :-1, :]], axis=1)
        pre_s = _rwkv_pre_sample(pc_s3.reshape(ns_rows, RWKV_PROJ), pc_shift.reshape(ns_rows, RWKV_PROJ), prm, l,
                                 precise)
        r_s, w_s, k_s, v_s, ka_s, kb_s, g_s = pre_s
        as3s = lambda t: t.reshape(dbatch, dseq, RWKV_WIDTH)
        y_s, wkv_new_s = _wkv_steps([as3s(t) for t in (r_s, w_s, k_s, v_s, ka_s, kb_s)], state_rwkv_wkv[l])
        c_s = _rwkv_post(y_s.reshape(ns_rows, RWKV_WIDTH), r_s, k_s, v_s, g_s, prm, l, precise, tm=ns_rows)
        st_s.append((k_new_s.reshape(dbatch, wbuf, N_KV_HEADS, HEAD_DIM),
                     v_new_s.reshape(dbatch, wbuf, N_KV_HEADS, HEAD_DIM),
                     tmaj(pool_new_t), pc_s3[:, dseq - 1, :], wkv_new_s, tmaj(conv_new_t)))
        br_s = (a_s.reshape(ns_rows, ATTN_WIDTH), tmaj(b_s_t).reshape(ns_rows, POOL_WIDTH), c_s,
                tmaj(d_s_t).reshape(ns_rows, CONV_WIDTH))

        def tail(grp, x, branches, p, mod3, tm_mm, tm_el):
            if precise:
                tm_t = min(tm_mm, 512)
                merged32, merged = _merge(branches, p, w_branch, l, tm=tm_t, tn=512, tail=tails(tm_t, grp))
                x1 = _mm_res(grp, merged, w_out, l, x, mod3, _GT1, tm=tm_mm, tn=512, a_f32=merged32,
                             tail_rows=tail_s if grp is grp_s else tail_p)
            else:
                merged, = _merge(branches, p, w_branch, l, tm=min(tm_mm, 1024), tn=512, tail=None)
                x1 = _mm_res(grp, merged, w_out, l, x, mod3, _GT1, tm=tm_mm, tn=512)
            return (x1,) + tuple(_normx(grp, x1, g_norm2[l], mod=(mod3, _SC2, _SH2), route=(wr_pad, b_router),
                                        out_dtype=F32, tm=tm_el))

        xp, h2_p, e_p, wt_p = tail(grp_p, xp, (a_p, b_p, c_p, d_p), pp, mod_p, tm_p, te_p)
        xs, h2_s, e_s, wt_s = tail(grp_s, xs, br_s, ps, mod_s, tm_s, tm_s)
        m_all = _moe(jnp.concatenate([h2_p, h2_s]), jnp.concatenate([e_p, e_s]), jnp.concatenate([wt_p, wt_s]),
                     w_gate, w_up, w_down, l)
        pend_p, pend_s = (m_all, 0, mod_p), (m_all, np_rows, mod_s)

    y_p = _normx(grp_p, xp, g_final, add=pend_p + (_GT2,), out_dtype=F32, tm=te_p)[0]
    y_s = _normx(grp_s, xs, g_final, add=pend_s + (_GT2,), out_dtype=F32, tm=tm_s)[0]

    def stack(states, i):
        return jnp.stack([s[i] for s in states])

    return ((y_p.reshape(batch, seq, d), y_s.reshape(dbatch, dseq, d))
            + tuple(stack(st_p, i) for i in range(6)) + tuple(stack(st_s, i) for i in range(6)))
```

```python
import functools

import jax
import jax.numpy as jnp
from jax import lax
from jax.experimental import pallas as pl
from jax.experimental.pallas import tpu as pltpu

F32 = jnp.float32
BF16 = jnp.bfloat16

D_MODEL = 2048
PAST_LEN = 8192
WINDOW = 128
HEAD_DIM = 64
N_HEADS = 16
N_KV_HEADS = 4
GQA_GROUP = N_HEADS // N_KV_HEADS
ATTN_WIDTH = N_HEADS * HEAD_DIM
KV_WIDTH = N_KV_HEADS * HEAD_DIM
ATTN_SCALE = HEAD_DIM ** -0.5
NEG_INF = -1e30
POOL_WINDOWS = (2, 4, 8, 16)
POOL_WIDTH = 768
POOL_GW = POOL_WIDTH // len(POOL_WINDOWS)
POOL_BUF = max(POOL_WINDOWS) - 1
RWKV_HEAD = 64
RWKV_WIDTH = 768
RWKV_HEADS = RWKV_WIDTH // RWKV_HEAD
DECAY_LORA = 64
AAA_LORA = 64
GATE_LORA = 128
RWKV_PROJ = 3 * RWKV_WIDTH + DECAY_LORA + AAA_LORA + GATE_LORA
RWKV_LN_EPS = 64e-5
CONV_WIDTH = 768
CONV_K = 3
N_BRANCH = 4
BRANCH_SECTIONS = (ATTN_WIDTH, POOL_WIDTH, RWKV_WIDTH, CONV_WIDTH)
MIX_WIDTH = sum(BRANCH_SECTIONS)
N_EXPERTS = 16
N_GROUPS = 4
EXP_PER_GROUP = N_EXPERTS // N_GROUPS
TOP_K = 2
D_EXPERT = 1024
RMS_EPS = 1e-6

OFF_Q = 0
OFF_K = OFF_Q + ATTN_WIDTH
OFF_V = OFF_K + KV_WIDTH
OFF_U = OFF_V + KV_WIDTH
OFF_PC = OFF_U + POOL_WIDTH
OFF_CB = OFF_PC + RWKV_PROJ
OFF_CC = OFF_CB + CONV_WIDTH
OFF_CX = OFF_CC + CONV_WIDTH
OFF_GL = OFF_CX + CONV_WIDTH
IN_WIDTH = OFF_GL + N_BRANCH * D_MODEL

PRECISE_TAIL = 2 * WINDOW
LANES = 128
ROUTER_PAD = LANES
VMEM_LIMIT = 60 * 1024 * 1024


def _cp(*sem):
    return pltpu.CompilerParams(dimension_semantics=sem, vmem_limit_bytes=VMEM_LIMIT)


def _sigmoid(x):
    return 1.0 / (1.0 + jnp.exp(-x))


def _split(x):
    hi = x.astype(BF16)
    return hi, (x.astype(F32) - hi.astype(F32)).astype(BF16)


def _mxu(contract, a, b, precise):
    if not precise:
        return contract(a.astype(BF16), b.astype(BF16))
    ah, al = _split(a)
    bh, bl = _split(b)
    return contract(ah, bh) + (contract(ah, bl) + contract(al, bh))


def _dot2(a, b):
    return jnp.dot(a, b, preferred_element_type=F32)


def _bdot(a, b, precise=False):
    return _mxu(_dot2, a, b, precise)


def _act_dtype(precise):
    return F32 if precise else BF16


def _ada_kernel(c_ref, w_ref, b_ref, o_ref, *, precise):
    c = c_ref[...]
    o_ref[...] = _bdot(c * _sigmoid(c), w_ref[...], precise) + b_ref[...]


def _ada(c_all, w_ada, b_ada, layer, precise):
    depth, d, n = w_ada.shape
    nb = c_all.shape[0]
    tn = 1024
    return pl.pallas_call(
        functools.partial(_ada_kernel, precise=precise),
        grid=(n // tn,),
        in_specs=[pl.BlockSpec((nb, d), lambda j: (0, 0)),
                  pl.BlockSpec((None, d, tn), lambda j: (layer, 0, j)),
                  pl.BlockSpec((None, 1, tn), lambda j: (layer, 0, j))],
        out_specs=pl.BlockSpec((nb, tn), lambda j: (0, j)),
        out_shape=jax.ShapeDtypeStruct((nb, n), F32),
        compiler_params=_cp("parallel"),
        name="ada",
    )(c_all, w_ada, b_ada.reshape(depth, 1, n))


class _Group:
    def __init__(self, rows, rpm, mod_rows):
        self.rows, self.rpm, self.mod_rows = rows, rpm, mod_rows

    def mod_spec(self, tm, width, col_of, row_axis=0):
        if self.mod_rows == 1:
            per = self.rpm // tm
            return pl.BlockSpec((None, 1, width), lambda *idx: (idx[row_axis] // per, 0, col_of(idx)))
        assert tm == self.rpm == self.mod_rows
        return pl.BlockSpec((None, tm, width), lambda *idx: (idx[row_axis], 0, col_of(idx)))


def _route(y, wr, br):
    y_hi = y.astype(BF16)
    y_lo = (y - y_hi.astype(F32)).astype(BF16)
    w_hi = wr.astype(BF16)
    w_lo = (wr - w_hi.astype(F32)).astype(BF16)
    logits = (jnp.dot(y_hi, w_hi, preferred_element_type=F32)
              + (jnp.dot(y_hi, w_lo, preferred_element_type=F32) + jnp.dot(y_lo, w_hi, preferred_element_type=F32)))
    logits = logits[:, :N_EXPERTS] + br
    tm = logits.shape[0]
    e = jnp.exp(logits - jnp.max(logits, axis=-1, keepdims=True))
    probs = e / jnp.sum(e, axis=-1, keepdims=True)
    iota_g = lax.broadcasted_iota(jnp.int32, (tm, EXP_PER_GROUP), 1)
    best = None
    for g in range(N_GROUPS):
        pg = probs[:, g * EXP_PER_GROUP:(g + 1) * EXP_PER_GROUP]
        m1 = jnp.max(pg, axis=-1, keepdims=True)
        i1 = jnp.min(jnp.where(pg == m1, iota_g, EXP_PER_GROUP), axis=-1, keepdims=True)
        rest = jnp.where(iota_g == i1, -1.0, pg)
        m2 = jnp.max(rest, axis=-1, keepdims=True)
        i2 = jnp.min(jnp.where(rest == m2, iota_g, EXP_PER_GROUP), axis=-1, keepdims=True)
        cand = (m1 + m2, m1, m2, i1 + g * EXP_PER_GROUP, i2 + g * EXP_PER_GROUP)
        if best is None:
            best = cand
        else:
            take = cand[0] > best[0]
            best = tuple(jnp.where(take, c, b) for c, b in zip(cand, best))
    _, m1, m2, e1, e2 = best
    den = m1 + m2
    slot = lax.broadcasted_iota(jnp.int32, (tm, TOP_K), 1)
    return jnp.where(slot == 0, e1, e2), jnp.where(slot == 0, m1 / den, m2 / den)


def _normx_kernel(*refs, has_add, has_mod, has_route, emit_x, n_h):
    it = iter(refs)
    x_ref = next(it)
    if has_add:
        m_ref, gate_ref = next(it), next(it)
    g_ref = next(it)
    if has_mod:
        sc_ref, sh_ref = next(it), next(it)
    if has_route:
        wr_ref, br_ref = next(it), next(it)
    if emit_x:
        xo_ref = next(it)
    h_refs = [next(it) for _ in range(n_h)]
    if has_route:
        eidx_ref, wts_ref = next(it), next(it)
    x = x_ref[...]
    if has_add:
        x = x + gate_ref[...] * m_ref[...]
    if emit_x:
        xo_ref[...] = x
    y = x * lax.rsqrt(jnp.mean(x * x, axis=-1, keepdims=True) + RMS_EPS) * g_ref[...]
    if has_mod:
        y = y * (1.0 + sc_ref[...]) + sh_ref[...]
    for h_ref in h_refs:
        h_ref[...] = y.astype(h_ref.dtype)
    if has_route:
        eidx_ref[...], wts_ref[...] = _route(y, wr_ref[...], br_ref[...])


def _normx(grp, x, g, *, add=None, mod=None, route=None, emit_x=False, out_dtype=BF16, tm=512):
    out_dtypes = out_dtype if isinstance(out_dtype, tuple) else (out_dtype,)
    n, d = x.shape
    row = pl.BlockSpec((tm, d), lambda i: (i, 0))
    vec = pl.BlockSpec((1, d), lambda i: (0, 0))
    args, specs = [x], [row]
    if add is not None:
        m, m_row0, mod3, chunk = add
        args += [m, mod3]
        specs += [pl.BlockSpec((tm, d), lambda i, o=m_row0 // tm: (i + o, 0)),
                  grp.mod_spec(tm, d, lambda idx, c=chunk: c)]
    args.append(g.reshape(1, d))
    specs.append(vec)
    if mod is not None:
        mod3, c_sc, c_sh = mod
        args += [mod3, mod3]
        specs += [grp.mod_spec(tm, d, lambda idx, c=c_sc: c), grp.mod_spec(tm, d, lambda idx, c=c_sh: c)]
    if route is not None:
        wr, br = route
        args += [wr, br.reshape(1, N_EXPERTS)]
        specs += [pl.BlockSpec((d, ROUTER_PAD), lambda i: (0, 0)), pl.BlockSpec((1, N_EXPERTS), lambda i: (0, 0))]
    out_shape, out_specs = [], []
    if emit_x:
        out_shape.append(jax.ShapeDtypeStruct((n, d), F32))
        out_specs.append(row)
    out_shape += [jax.ShapeDtypeStruct((n, d), dt) for dt in out_dtypes]
    out_specs += [row] * len(out_dtypes)
    if route is not None:
        out_shape += [jax.ShapeDtypeStruct((n, TOP_K), jnp.int32), jax.ShapeDtypeStruct((n, TOP_K), F32)]
        out_specs += [pl.BlockSpec((tm, TOP_K), lambda i: (i, 0))] * 2
    return pl.pallas_call(
        functools.partial(_normx_kernel, has_add=add is not None, has_mod=mod is not None,
                          has_route=route is not None, emit_x=emit_x, n_h=len(out_dtypes)),
        grid=(n // tm,), in_specs=specs, out_specs=out_specs, out_shape=out_shape,
        compiler_params=_cp("parallel"), name="normx",
    )(*args)


def _tail_tile(tail):
    return lax.rem(pl.program_id(0) + 1, tail[1]) == 0


def _tail_fix(a, w, tail):
    a_hi, a_lo = _split(a[a.shape[0] - tail[0]:, :])
    w_hi, w_lo = _split(w)
    return _dot2(a_hi, w_lo) + _dot2(a_lo, w_hi)


def _tail_rows(x, rows):
    return x if x.shape[0] == 1 else x[x.shape[0] - rows:, :]


def _tail_spec(tm, k, tail_rows):
    per = tm // tail_rows
    return pl.BlockSpec((tail_rows, k), lambda i, j: ((i + 1) * per - 1, 0))


def _mm_kernel(*refs, tail_rows):
    if tail_rows:
        a_ref, at_ref, w_ref, o_ref = refs
    else:
        a_ref, w_ref, o_ref = refs
    o_ref[...] = _bdot(a_ref[...], w_ref[...])
    if tail_rows:
        lo = o_ref.shape[0] - tail_rows
        o_ref[lo:, :] += _tail_fix(at_ref[...], w_ref[...], (tail_rows, 1))


def _mm(a, w3, layer, *, tm, tn, a_f32=None, tail_rows=0):
    m, k = a.shape
    n = w3.shape[-1]
    tails = [a_f32] if tail_rows else []
    return pl.pallas_call(
        functools.partial(_mm_kernel, tail_rows=tail_rows),
        grid=(m // tm, n // tn),
        in_specs=[pl.BlockSpec((tm, k), lambda i, j: (i, 0))]
        + ([_tail_spec(tm, k, tail_rows)] if tail_rows else [])
        + [pl.BlockSpec((None, k, tn), lambda i, j: (layer, 0, j))],
        out_specs=pl.BlockSpec((tm, tn), lambda i, j: (i, j)),
        out_shape=jax.ShapeDtypeStruct((m, n), F32),
        compiler_params=_cp("parallel", "parallel"), name="mm_in",
    )(a, *tails, w3)


def _mm_res_kernel(*refs, tail_rows):
    if tail_rows:
        a_ref, at_ref, w_ref, x_ref, gate_ref, o_ref = refs
    else:
        a_ref, w_ref, x_ref, gate_ref, o_ref = refs
    o_ref[...] = x_ref[...] + gate_ref[...] * _bdot(a_ref[...], w_ref[...])
    if tail_rows:
        lo = o_ref.shape[0] - tail_rows
        o_ref[lo:, :] += _tail_rows(gate_ref[...], tail_rows) * _tail_fix(at_ref[...], w_ref[...], (tail_rows, 1))


def _mm_res(grp, a, w3, layer, x, mod3, gate_chunk, *, tm, tn, a_f32=None, tail_rows=0):
    m, k = a.shape
    n = w3.shape[-1]
    per_chunk = n // tn
    tails = [a_f32] if tail_rows else []
    return pl.pallas_call(
        functools.partial(_mm_res_kernel, tail_rows=tail_rows),
        grid=(m // tm, n // tn),
        in_specs=[pl.BlockSpec((tm, k), lambda i, j: (i, 0))]
        + ([_tail_spec(tm, k, tail_rows)] if tail_rows else [])
        + [pl.BlockSpec((None, k, tn), lambda i, j: (layer, 0, j)),
           pl.BlockSpec((tm, tn), lambda i, j: (i, j)),
           grp.mod_spec(tm, tn, lambda idx: gate_chunk * per_chunk + idx[1])],
        out_specs=pl.BlockSpec((tm, tn), lambda i, j: (i, j)),
        out_shape=jax.ShapeDtypeStruct((m, n), F32),
        compiler_params=_cp("parallel", "parallel"), name="mm_out",
    )(a, *tails, w3, x, mod3)


def _merge_kernel(a_ref, b_ref, c_ref, d_ref, g0_ref, g1_ref, g2_ref, g3_ref, w_ref, o_ref, *ob_ref, tail):
    parts = []
    lo = 0
    for br_ref, g_ref, width in zip((a_ref, b_ref, c_ref, d_ref), (g0_ref, g1_ref, g2_ref, g3_ref), BRANCH_SECTIONS):
        parts.append((br_ref, g_ref, lo, width))
        lo += width
    acc = None
    for br_ref, g_ref, lo, width in parts:
        t = _sigmoid(g_ref[...]) * _bdot(br_ref[...], w_ref[lo:lo + width, :])
        acc = t if acc is None else acc + t
    o_ref[...] = acc.astype(o_ref.dtype)
    if tail is not None:
        @pl.when(_tail_tile(tail))
        def _():
            row0 = o_ref.shape[0] - tail[0]
            fix = None
            for br_ref, g_ref, lo, width in parts:
                t = _sigmoid(g_ref[row0:, :]) * _tail_fix(br_ref[...], w_ref[lo:lo + width, :], tail)
                fix = t if fix is None else fix + t
            o_ref[row0:, :] += fix

        ob_ref[0][...] = o_ref[...].astype(BF16)


def _merge(branches, p, w_branch, layer, *, tm, tn, tail):
    m = p.shape[0]
    gl_blk = OFF_GL // tn
    per = D_MODEL // tn
    br_specs = [pl.BlockSpec((tm, w), lambda i, j: (i, 0)) for w in BRANCH_SECTIONS]
    gl_specs = [pl.BlockSpec((tm, tn), lambda i, j, b=b: (i, gl_blk + b * per + j)) for b in range(N_BRANCH)]
    return pl.pallas_call(
        functools.partial(_merge_kernel, tail=tail),
        grid=(m // tm, D_MODEL // tn),
        in_specs=br_specs + gl_specs + [pl.BlockSpec((None, MIX_WIDTH, tn), lambda i, j: (layer, 0, j))],
        out_specs=[pl.BlockSpec((tm, tn), lambda i, j: (i, j))] * (1 if tail is None else 2),
        out_shape=[jax.ShapeDtypeStruct((m, D_MODEL), BF16)] if tail is None else
        [jax.ShapeDtypeStruct((m, D_MODEL), F32), jax.ShapeDtypeStruct((m, D_MODEL), BF16)],
        compiler_params=_cp("parallel", "parallel"), name="merge",
    )(*branches, p, p, p, p, w_branch)


def _sink_col(sink_ref, layer, kh, rows_per_head):
    return jnp.concatenate([jnp.full((rows_per_head, 1), sink_ref[layer, kh * GQA_GROUP + g], F32)
                            for g in range(GQA_GROUP)], axis=0)


def _dot_nt(a, b):
    return lax.dot_general(a, b, (((1,), (1,)), ((), ())), preferred_element_type=F32)


def _swa_prompt_kernel(sink_ref, q_ref, kc_ref, kp_ref, vc_ref, vp_ref, o_ref, *, layer, precise):
    n = pl.program_id(1)
    w = WINDOW
    q = q_ref[...]
    qi = jnp.bitwise_and(lax.broadcasted_iota(jnp.int32, (GQA_GROUP * w, 2 * w), 0), w - 1)
    sj = lax.broadcasted_iota(jnp.int32, (GQA_GROUP * w, 2 * w), 1)
    valid = (sj >= qi) & (sj <= qi + w) & ((sj >= w) | (n > 0))

    def run(three_pass):
        outs = []
        for kh in range(N_KV_HEADS):
            sl = slice(kh * HEAD_DIM, (kh + 1) * HEAD_DIM)
            k2 = jnp.concatenate([kp_ref[:, sl], kc_ref[:, sl]], axis=0)
            v2 = jnp.concatenate([vp_ref[:, sl], vc_ref[:, sl]], axis=0)
            q4 = jnp.concatenate([q[:, (kh * GQA_GROUP + g) * HEAD_DIM:(kh * GQA_GROUP + g + 1) * HEAD_DIM]
                                  for g in range(GQA_GROUP)], axis=0) * ATTN_SCALE
            s = _mxu(_dot_nt, q4, k2, three_pass)
            s = jnp.where(valid, s, NEG_INF)
            sk = _sink_col(sink_ref, layer, kh, w)
            m = jnp.maximum(jnp.max(s, axis=-1, keepdims=True), sk)
            e = jnp.exp(s - m)
            p = e / (jnp.sum(e, axis=-1, keepdims=True) + jnp.exp(sk - m))
            o4 = _bdot(p, v2, three_pass)
            outs += [o4[g * w:(g + 1) * w] for g in range(GQA_GROUP)]
        o_ref[...] = jnp.concatenate(outs, axis=1).astype(o_ref.dtype)

    if not precise:
        run(False)
    else:
        last = n >= pl.num_programs(1) - PRECISE_TAIL // w

        @pl.when(last)
        def _():
            run(True)

        @pl.when(jnp.logical_not(last))
        def _():
            run(False)


def _swa_prompt(p, sinks, layer, batch, seq, precise):
    nb = seq // WINDOW
    kblk, vblk = OFF_K // KV_WIDTH, OFF_V // KV_WIDTH

    def cur(col):
        return lambda b, n: (b * nb + n, col)

    def prev(col):
        return lambda b, n: (b * nb + jnp.maximum(n - 1, 0), col)

    return pl.pallas_call(
        functools.partial(_swa_prompt_kernel, layer=layer, precise=precise),
        grid=(batch, nb),
        in_specs=[pl.BlockSpec(memory_space=pltpu.SMEM),
                  pl.BlockSpec((WINDOW, ATTN_WIDTH), cur(0)),
                  pl.BlockSpec((WINDOW, KV_WIDTH), cur(kblk)), pl.BlockSpec((WINDOW, KV_WIDTH), prev(kblk)),
                  pl.BlockSpec((WINDOW, KV_WIDTH), cur(vblk)), pl.BlockSpec((WINDOW, KV_WIDTH), prev(vblk))],
        out_specs=pl.BlockSpec((WINDOW, ATTN_WIDTH), lambda b, n: (b * nb + n, 0)),
        out_shape=jax.ShapeDtypeStruct((batch * seq, ATTN_WIDTH), _act_dtype(precise)),
        compiler_params=_cp("parallel", "parallel"), name="swa_prompt",
    )(sinks, p, p, p, p, p)


def _qk(a, b):
    return jnp.einsum("bqd,bkd->bqk", a, b, preferred_element_type=F32)


def _pv(a, b):
    return jnp.einsum("bqk,bkd->bqd", a, b, preferred_element_type=F32)


def _swa_sample_kernel(sink_ref, q_ref, kn_ref, vn_ref, kc_ref, vc_ref, o_ref, ko_ref, vo_ref, *, layer, steps, wbuf,
                       precise):
    q = q_ref[...]
    kn, vn = kn_ref[...], vn_ref[...]
    kc, vc = kc_ref[...], vc_ref[...]
    ko_ref[:, :wbuf - steps, :] = kc[:, steps:, :]
    ko_ref[:, wbuf - steps:, :] = kn
    vo_ref[:, :wbuf - steps, :] = vc[:, steps:, :]
    vo_ref[:, wbuf - steps:, :] = vn
    bb = q.shape[0]
    rows = GQA_GROUP * steps
    t_c = lax.rem(lax.broadcasted_iota(jnp.int32, (bb, rows, wbuf), 1), steps)
    j_c = lax.broadcasted_iota(jnp.int32, (bb, rows, wbuf), 2)
    dist_c = t_c + wbuf - j_c
    valid_c = (dist_c >= 0) & (dist_c <= WINDOW)
    t_n = lax.rem(lax.broadcasted_iota(jnp.int32, (bb, rows, steps), 1), steps)
    j_n = lax.broadcasted_iota(jnp.int32, (bb, rows, steps), 2)
    valid_n = (t_n - j_n >= 0) & (t_n - j_n <= WINDOW)
    outs = [None] * N_HEADS
    for kh in range(N_KV_HEADS):
        sl = slice(kh * HEAD_DIM, (kh + 1) * HEAD_DIM)
        qg = jnp.concatenate([q[:, :, (kh * GQA_GROUP + g) * HEAD_DIM:(kh * GQA_GROUP + g + 1) * HEAD_DIM]
                              for g in range(GQA_GROUP)], axis=1)
        s_c = _mxu(_qk, qg, kc[:, :, sl], precise) * ATTN_SCALE
        s_n = _mxu(_qk, qg, kn[:, :, sl], precise) * ATTN_SCALE
        s_c = jnp.where(valid_c, s_c, NEG_INF)
        s_n = jnp.where(valid_n, s_n, NEG_INF)
        sk = _sink_col(sink_ref, layer, kh, steps)[None]
        m = jnp.maximum(jnp.maximum(jnp.max(s_c, axis=-1, keepdims=True), jnp.max(s_n, axis=-1, keepdims=True)), sk)
        e_c, e_n = jnp.exp(s_c - m), jnp.exp(s_n - m)
        den = jnp.sum(e_c, axis=-1, keepdims=True) + jnp.sum(e_n, axis=-1, keepdims=True) + jnp.exp(sk - m)
        o = _mxu(_pv, e_c / den, vc[:, :, sl], precise) + _mxu(_pv, e_n / den, vn[:, :, sl], precise)
        for g in range(GQA_GROUP):
            outs[kh * GQA_GROUP + g] = o[:, g * steps:(g + 1) * steps, :]
    o_ref[...] = jnp.concatenate(outs, axis=2).astype(o_ref.dtype)


def _swa_sample(p3, cache_k, cache_v, sinks, layer, precise, *, bb=8):
    batch, steps, _ = p3.shape
    wbuf = cache_k.shape[2]
    kblk, vblk = OFF_K // KV_WIDTH, OFF_V // KV_WIDTH
    cache_spec = pl.BlockSpec((None, bb, wbuf, KV_WIDTH), lambda i: (layer, i, 0, 0))
    new_spec = pl.BlockSpec((bb, wbuf, KV_WIDTH), lambda i: (i, 0, 0))
    return pl.pallas_call(
        functools.partial(_swa_sample_kernel, layer=layer, steps=steps, wbuf=wbuf, precise=precise),
        grid=(batch // bb,),
        in_specs=[pl.BlockSpec(memory_space=pltpu.SMEM),
                  pl.BlockSpec((bb, steps, ATTN_WIDTH), lambda i: (i, 0, 0)),
                  pl.BlockSpec((bb, steps, KV_WIDTH), lambda i: (i, 0, kblk)),
                  pl.BlockSpec((bb, steps, KV_WIDTH), lambda i: (i, 0, vblk)),
                  cache_spec, cache_spec],
        out_specs=[pl.BlockSpec((bb, steps, ATTN_WIDTH), lambda i: (i, 0, 0)), new_spec, new_spec],
        out_shape=[jax.ShapeDtypeStruct((batch, steps, ATTN_WIDTH), _act_dtype(precise)),
                   jax.ShapeDtypeStruct((batch, wbuf, KV_WIDTH), F32),
                   jax.ShapeDtypeStruct((batch, wbuf, KV_WIDTH), F32)],
        compiler_params=_cp("parallel"), name="swa_sample",
    )(sinks, p3, p3, p3, cache_k, cache_v)


def _shift_rows(x, k):
    rows = lax.broadcasted_iota(jnp.int32, x.shape, 0)
    return jnp.where(rows >= k, pltpu.roll(x, k, axis=0), 0.0)


def _pool_prompt_kernel(u_ref, w_ref, ls_ref, o_ref, *, precise):
    u = u_ref[...]
    t = u.shape[0]
    pos1 = (lax.broadcasted_iota(jnp.int32, (t, 1), 0) + 1).astype(F32)
    sums = {1: u}
    win = 1
    while win < max(POOL_WINDOWS):
        sums[2 * win] = sums[win] + _shift_rows(sums[win], win)
        win *= 2
    outs = []
    for gi, win in enumerate(POOL_WINDOWS):
        sl = slice(gi * POOL_GW, (gi + 1) * POOL_GW)
        cnt = jnp.minimum(float(win), pos1)
        d = sums[win][:, sl] / cnt - u[:, sl]
        outs.append(_bdot(d, w_ref[gi], precise))
    o_ref[...] = (jnp.concatenate(outs, axis=1) * ls_ref[...]).astype(o_ref.dtype)


def _pool_prompt(p, w_pool, ls_pool, layer, batch, seq, precise):
    gw = POOL_GW
    return pl.pallas_call(
        functools.partial(_pool_prompt_kernel, precise=precise),
        grid=(batch,),
        in_specs=[pl.BlockSpec((seq, POOL_WIDTH), lambda b: (b, OFF_U // POOL_WIDTH)),
                  pl.BlockSpec((None, len(POOL_WINDOWS), gw, gw), lambda b: (layer, 0, 0, 0)),
                  pl.BlockSpec((None, 1, POOL_WIDTH), lambda b: (layer, 0, 0))],
        out_specs=pl.BlockSpec((seq, POOL_WIDTH), lambda b: (b, 0)),
        out_shape=jax.ShapeDtypeStruct((batch * seq, POOL_WIDTH), _act_dtype(precise)),
        compiler_params=_cp("parallel"), name="pool_prompt",
    )(p, w_pool, ls_pool.reshape(ls_pool.shape[0], 1, POOL_WIDTH))


def _pool_sample_kernel(u_ref, past_ref, w_ref, ls_ref, o_ref, new_ref, *, pos0, precise):
    steps, hist = u_ref.shape[0], past_ref.shape[0]
    full = [past_ref[i] for i in range(hist)] + [u_ref[i] for i in range(steps)]
    for i in range(hist):
        new_ref[i] = full[steps + i]
    ds = [[] for _ in POOL_WINDOWS]
    for t in range(steps):
        for gi, win in enumerate(POOL_WINDOWS):
            sl = slice(gi * POOL_GW, (gi + 1) * POOL_GW)
            wsum = full[hist + t][:, sl]
            for s in range(1, win):
                wsum = wsum + full[hist + t - s][:, sl]
            cnt = float(min(win, pos0 + t + 1))
            ds[gi].append(wsum / cnt - full[hist + t][:, sl])
    ys = [_bdot(jnp.concatenate(ds[gi], axis=0), w_ref[gi], precise) for gi in range(len(POOL_WINDOWS))]
    y = jnp.concatenate(ys, axis=1) * ls_ref[...]
    nb = u_ref.shape[1]
    for t in range(steps):
        o_ref[t] = y[t * nb:(t + 1) * nb].astype(o_ref.dtype)


def _pool_sample(u_t, past_t, w_pool, ls_pool, layer, pos0, precise):
    steps, nb, _ = u_t.shape
    return pl.pallas_call(
        functools.partial(_pool_sample_kernel, pos0=pos0, precise=precise),
        grid=(1,),
        in_specs=[pl.BlockSpec(u_t.shape, lambda i: (0, 0, 0)),
                  pl.BlockSpec(past_t.shape, lambda i: (0, 0, 0)),
                  pl.BlockSpec((None, len(POOL_WINDOWS), POOL_GW, POOL_GW), lambda i: (layer, 0, 0, 0)),
                  pl.BlockSpec((None, 1, POOL_WIDTH), lambda i: (layer, 0, 0))],
        out_specs=[pl.BlockSpec(u_t.shape, lambda i: (0, 0, 0)), pl.BlockSpec(past_t.shape, lambda i: (0, 0, 0))],
        out_shape=[jax.ShapeDtypeStruct(u_t.shape, _act_dtype(precise)), jax.ShapeDtypeStruct(past_t.shape, F32)],
        compiler_params=_cp("arbitrary"), name="pool_sample",
    )(u_t, past_t, w_pool, ls_pool.reshape(ls_pool.shape[0], 1, POOL_WIDTH))


def _conv_prompt_kernel(cb_ref, cc_ref, cx_ref, w_ref, o_ref, new_ref):
    z = cc_ref[...] * cx_ref[...]
    w = w_ref[...]
    y = w[CONV_K - 1:CONV_K] * z
    for j in range(1, CONV_K):
        y = y + w[CONV_K - 1 - j:CONV_K - j] * _shift_rows(z, j)
    o_ref[...] = (cb_ref[...] * y).astype(o_ref.dtype)
    new_ref[...] = z[z.shape[0] - (CONV_K - 1):]


def _conv_prompt(p, conv_w, layer, batch, seq, precise, *, tc=256):
    nc = CONV_WIDTH // tc

    def col(off):
        return lambda b, c: (b, off // tc + c)

    return pl.pallas_call(
        _conv_prompt_kernel,
        grid=(batch, nc),
        in_specs=[pl.BlockSpec((seq, tc), col(OFF_CB)), pl.BlockSpec((seq, tc), col(OFF_CC)),
                  pl.BlockSpec((seq, tc), col(OFF_CX)),
                  pl.BlockSpec((None, CONV_K, tc), lambda b, c: (layer, 0, c))],
        out_specs=[pl.BlockSpec((seq, tc), lambda b, c: (b, c)),
                   pl.BlockSpec((None, CONV_K - 1, tc), lambda b, c: (b, 0, c))],
        out_shape=[jax.ShapeDtypeStruct((batch * seq, CONV_WIDTH), _act_dtype(precise)),
                   jax.ShapeDtypeStruct((batch, CONV_K - 1, CONV_WIDTH), F32)],
        compiler_params=_cp("parallel", "parallel"), name="conv_prompt",
    )(p, p, p, conv_w)


def _conv_sample_kernel(cb_ref, cc_ref, cx_ref, past_ref, w_ref, o_ref, new_ref):
    steps, hist = cb_ref.shape[0], past_ref.shape[0]
    w = w_ref[...]
    full = [past_ref[i] for i in range(hist)] + [cc_ref[t] * cx_ref[t] for t in range(steps)]
    for t in range(steps):
        y = w[0:1] * full[t]
        for j in range(1, CONV_K):
            y = y + w[j:j + 1] * full[t + j]
        o_ref[t] = (cb_ref[t] * y).astype(o_ref.dtype)
    for i in range(hist):
        new_ref[i] = full[steps + i]


def _conv_sample(cb_t, cc_t, cx_t, past_t, conv_w, layer, precise):
    full3 = lambda shape: pl.BlockSpec(shape, lambda i: (0, 0, 0))
    return pl.pallas_call(
        _conv_sample_kernel,
        grid=(1,),
        in_specs=[full3(cb_t.shape), full3(cc_t.shape), full3(cx_t.shape), full3(past_t.shape),
                  pl.BlockSpec((None, CONV_K, CONV_WIDTH), lambda i: (layer, 0, 0))],
        out_specs=[full3(cb_t.shape), full3(past_t.shape)],
        out_shape=[jax.ShapeDtypeStruct(cb_t.shape, _act_dtype(precise)), jax.ShapeDtypeStruct(past_t.shape, F32)],
        compiler_params=_cp("arbitrary"), name="conv_sample",
    )(cb_t, cc_t, cx_t, past_t, conv_w)


def _head_sum(x):
    rows = x.shape[0]
    return jnp.concatenate(
        [jnp.broadcast_to(jnp.sum(x[:, h * RWKV_HEAD:(h + 1) * RWKV_HEAD], axis=-1, keepdims=True), (rows, RWKV_HEAD))
         for h in range(RWKV_HEADS)], axis=1)


def _softplus(x):
    return jnp.maximum(x, 0.0) + jnp.log(1.0 + jnp.exp(-jnp.abs(x)))


def _rwkv_pre_core(cur, sh, mu, w0, w2, a0, a2, g2, k_k, k_a, outs, precise):
    xr, xk, xv, xwa, xg = [c + (s - c) * m for c, s, m in zip(cur, sh, mu)]
    wd, ad = xwa[:, :DECAY_LORA], xwa[:, DECAY_LORA:]
    w_log = -_softplus(-(w0 + _bdot(jnp.tanh(wd), w2, precise))) - 0.5
    log_decay = -jnp.exp(w_log)
    a = _sigmoid(a0 + _bdot(ad, a2, precise))
    g = _bdot(_sigmoid(xg), g2, precise)
    kk = xk * k_k
    kk = kk / jnp.maximum(jnp.sqrt(_head_sum(kk * kk)), 1e-12)
    kf = xk * (1.0 + (a - 1.0) * k_a)
    r_ref, w_ref, k_ref, v_ref, a_ref, b_ref, g_ref = outs
    r_ref[...] = xr
    w_ref[...] = log_decay
    k_ref[...] = kf
    v_ref[...] = xv
    a_ref[...] = -kk
    b_ref[...] = kk * a
    g_ref[...] = g


_PRE_WIDTHS = (RWKV_WIDTH, RWKV_WIDTH, RWKV_WIDTH, DECAY_LORA + AAA_LORA, GATE_LORA)
_PRE_OFFS = (0, RWKV_WIDTH, 2 * RWKV_WIDTH, 3 * RWKV_WIDTH, 3 * RWKV_WIDTH + DECAY_LORA + AAA_LORA)
_HALO = 8


def _rwkv_pre_prompt_kernel(*refs, precise):
    cur_refs, halo_refs, mu_refs = refs[0:5], refs[5:10], refs[10:15]
    w0, w2, a0, a2, g2, k_k, k_a = [r[...] for r in refs[15:22]]
    outs = refs[22:]
    first = pl.program_id(1) == 0
    cur, sh = [], []
    for c_ref, h_ref in zip(cur_refs, halo_refs):
        c = c_ref[...]
        prev_row = jnp.where(first, 0.0, h_ref[_HALO - 1:_HALO, :])
        rows = lax.broadcasted_iota(jnp.int32, c.shape, 0)
        sh.append(jnp.where(rows == 0, prev_row, pltpu.roll(c, 1, axis=0)))
        cur.append(c)
    _rwkv_pre_core(cur, sh, [m[...] for m in mu_refs], w0, w2, a0, a2, g2, k_k, k_a, outs, precise)


def _rwkv_param_specs(layer):
    def spec(shape):
        return pl.BlockSpec((None,) + shape, lambda *idx: (layer,) + (0,) * len(shape))

    return [spec((1, RWKV_WIDTH)), spec((DECAY_LORA, RWKV_WIDTH)), spec((1, RWKV_WIDTH)),
            spec((AAA_LORA, RWKV_WIDTH)), spec((GATE_LORA, RWKV_WIDTH)), spec((1, RWKV_WIDTH)), spec((1, RWKV_WIDTH))]


def _rwkv_params(prm):
    depth = prm["w0"].shape[0]
    r3 = lambda a: a.reshape(depth, 1, RWKV_WIDTH)
    return [r3(prm["w0"]), prm["w2"], r3(prm["a0"]), prm["a2"], prm["g2"], r3(prm["kk"]), r3(prm["ka"])]


def _rwkv_pre_prompt(p, prm, layer, batch, seq, precise, *, tt=512):
    nt = seq // tt
    cur_specs, halo_specs, mu_specs = [], [], []
    for w, off in zip(_PRE_WIDTHS, _PRE_OFFS):
        cb = (OFF_PC + off) // w
        cur_specs.append(pl.BlockSpec((tt, w), lambda b, t, cb=cb: (b * nt + t, cb)))
        halo_specs.append(pl.BlockSpec(
            (_HALO, w), lambda b, t, cb=cb: (jnp.maximum((b * nt + t) * (tt // _HALO) - 1, 0), cb)))
        mu_specs.append(pl.BlockSpec((None, 1, w), lambda b, t, mb=off // w: (layer, 0, mb)))
    out_spec = pl.BlockSpec((tt, RWKV_WIDTH), lambda b, t: (b * nt + t, 0))
    mu3 = prm["mu"].reshape(prm["mu"].shape[0], 1, RWKV_PROJ)
    return pl.pallas_call(
        functools.partial(_rwkv_pre_prompt_kernel, precise=precise),
        grid=(batch, nt),
        in_specs=cur_specs + halo_specs + mu_specs + _rwkv_param_specs(layer),
        out_specs=[out_spec] * 7,
        out_shape=[jax.ShapeDtypeStruct((batch * seq, RWKV_WIDTH), F32)] * 7,
        compiler_params=_cp("parallel", "parallel"), name="rwkv_pre_prompt",
    )(*([p] * 10), *([mu3] * 5), *_rwkv_params(prm))


def _rwkv_pre_sample_kernel(*refs, precise):
    cur = [r[...] for r in refs[0:5]]
    sh = [r[...] for r in refs[5:10]]
    mu = [r[...] for r in refs[10:15]]
    w0, w2, a0, a2, g2, k_k, k_a = [r[...] for r in refs[15:22]]
    _rwkv_pre_core(cur, sh, mu, w0, w2, a0, a2, g2, k_k, k_a, refs[22:], precise)


def _rwkv_pre_sample(pc, pc_shifted, prm, layer, precise):
    rows = pc.shape[0]
    cur_specs, mu_specs = [], []
    for w, off in zip(_PRE_WIDTHS, _PRE_OFFS):
        cur_specs.append(pl.BlockSpec((rows, w), lambda i, cb=off // w: (0, cb)))
        mu_specs.append(pl.BlockSpec((None, 1, w), lambda i, mb=off // w: (layer, 0, mb)))
    out_spec = pl.BlockSpec((rows, RWKV_WIDTH), lambda i: (0, 0))
    mu3 = prm["mu"].reshape(prm["mu"].shape[0], 1, RWKV_PROJ)
    return pl.pallas_call(
        functools.partial(_rwkv_pre_sample_kernel, precise=precise),
        grid=(1,),
        in_specs=cur_specs + cur_specs + mu_specs + _rwkv_param_specs(layer),
        out_specs=[out_spec] * 7,
        out_shape=[jax.ShapeDtypeStruct((rows, RWKV_WIDTH), F32)] * 7,
        compiler_params=_cp("arbitrary"), name="rwkv_pre_sample",
    )(*([pc] * 5), *([pc_shifted] * 5), *([mu3] * 5), *_rwkv_params(prm))


def _split_heads(x):
    return jnp.stack([x[:, h * RWKV_HEAD:(h + 1) * RWKV_HEAD] for h in range(RWKV_HEADS)], axis=0)


def _join_heads(x):
    return jnp.concatenate([x[h] for h in range(RWKV_HEADS)], axis=1)


def _wkv_steps_kernel(r_ref, ld_ref, k_ref, v_ref, a_ref, b_ref, s0_ref, y_ref, sf_ref, *, steps, bb):
    n = RWKV_HEAD
    eye = lax.broadcasted_iota(jnp.int32, (n, n), 0) == lax.broadcasted_iota(jnp.int32, (n, n), 1)
    for i in range(bb):
        s = s0_ref[i]
        seqs = [_split_heads(ref[i]) for ref in (r_ref, ld_ref, k_ref, v_ref, a_ref, b_ref)]
        out_rows = []
        for t in range(steps):
            r, ld, k, v, a, b = [x[:, t:t + 1, :] for x in seqs]
            sa = jnp.sum(s * a, axis=-1, keepdims=True)
            vcol = jnp.sum(jnp.where(eye, v, 0.0), axis=-1, keepdims=True)
            s = s * jnp.exp(ld) + sa * b + vcol * k
            ycol = jnp.sum(s * r, axis=-1, keepdims=True)
            out_rows.append(jnp.sum(jnp.where(eye, ycol, 0.0), axis=1, keepdims=True))
        y_ref[i] = _join_heads(jnp.concatenate(out_rows, axis=1))
        sf_ref[i] = s


def _wkv_steps(seqs, s0, *, bb=4):
    batch, t, _ = seqs[0].shape
    seq_spec = pl.BlockSpec((bb, t, RWKV_WIDTH), lambda i: (i, 0, 0))
    st_spec = pl.BlockSpec((bb, RWKV_HEADS, RWKV_HEAD, RWKV_HEAD), lambda i: (i, 0, 0, 0))
    return pl.pallas_call(
        functools.partial(_wkv_steps_kernel, steps=t, bb=bb),
        grid=(batch // bb,),
        in_specs=[seq_spec] * 6 + [st_spec],
        out_specs=[seq_spec, st_spec],
        out_shape=[jax.ShapeDtypeStruct((batch, t, RWKV_WIDTH), F32), jax.ShapeDtypeStruct(s0.shape, F32)],
        compiler_params=_cp("parallel"), name="wkv_steps",
    )(*seqs, s0)


WKV_CHUNK = 64


def _e_nt(a, b):
    return jnp.einsum("hqd,hkd->hqk", a, b, preferred_element_type=F32)


def _e_nn(a, b):
    return jnp.einsum("hqk,hkd->hqd", a, b, preferred_element_type=F32)


def _wkv_chunk_kernel(r_ref, ld_ref, k_ref, v_ref, a_ref, b_ref, y_ref, sf_ref, s_scr):
    c = WKV_CHUNK

    @pl.when(pl.program_id(1) == 0)
    def _():
        s_scr[...] = jnp.zeros_like(s_scr)

    ld = ld_ref[...]
    cum = ld
    k = 1
    while k < c:
        cum = cum + _shift_rows(cum, k)
        k *= 2
    e_pos, e_prev, e_neg = jnp.exp(cum), jnp.exp(cum - ld), jnp.exp(-cum)
    at = _split_heads(a_ref[...] * e_prev)
    rt = _split_heads(r_ref[...] * e_pos)
    bt = _split_heads(b_ref[...] * e_neg)
    kt = _split_heads(k_ref[...] * e_neg)
    v = _split_heads(v_ref[...])
    lam = _split_heads(e_pos[c - 1:c, :])

    ti = lax.broadcasted_iota(jnp.int32, (c, c), 0)
    si = lax.broadcasted_iota(jnp.int32, (c, c), 1)
    strict = ti > si

    def blockmask(size):
        same = (ti // size) == (si // size)
        return strict & same & ((ti // (size // 2)) != (si // (size // 2)))

    def run(precise):
        dot_nt = functools.partial(_mxu, _e_nt, precise=precise)
        dot_nn = functools.partial(_mxu, _e_nn, precise=precise)

        def dot_tn(x, y):
            return dot_nn(jnp.swapaxes(x, 1, 2), y)

        ar = jnp.concatenate([at, rt], axis=1)
        g_b = dot_nt(ar, bt)
        g_k = dot_nt(ar, kt)
        n_ab = jnp.where(strict, g_b[:, :c, :], 0.0)
        n_ak = jnp.where(strict, g_k[:, :c, :], 0.0)
        m_rb = jnp.where(ti >= si, g_b[:, c:, :], 0.0)
        m_rk = jnp.where(ti >= si, g_k[:, c:, :], 0.0)

        base = 8
        n8 = jnp.where((ti // base) == (si // base), n_ab, 0.0)
        eye = (ti == si).astype(F32)
        n8_2 = dot_nn(n8, n8)
        n8_4 = dot_nn(n8_2, n8_2)
        t_inv = eye + n8
        t_inv = t_inv + dot_nn(t_inv, n8_2)
        t_inv = t_inv + dot_nn(t_inv, n8_4)
        size = 2 * base
        while size <= c:
            off = jnp.where(blockmask(size), n_ab, 0.0)
            t_inv = t_inv + dot_nn(dot_nn(t_inv, off), t_inv)
            size *= 2

        wv = dot_nn(n_ak, v)
        a_bar = dot_nn(t_inv, at)
        u_bar = dot_nn(t_inv, wv)
        r_bar = rt + dot_nn(m_rb, a_bar)
        y_bar = dot_nn(m_rb, u_bar) + dot_nn(m_rk, v)
        phi = dot_tn(a_bar, bt)
        psi = dot_tn(jnp.concatenate([u_bar, v], axis=1), jnp.concatenate([bt, kt], axis=1))

        s0 = s_scr[...]
        y_ref[...] = _join_heads(dot_nt(r_bar, s0) + y_bar)
        s_scr[...] = (s0 + dot_nn(s0, phi) + psi) * lam

    last = pl.program_id(1) >= pl.num_programs(1) - PRECISE_TAIL // c

    @pl.when(last)
    def _():
        run(True)

    @pl.when(jnp.logical_not(last))
    def _():
        run(False)

    @pl.when(pl.program_id(1) == pl.num_programs(1) - 1)
    def _():
        sf_ref[...] = s_scr[...]


def _wkv_chunked(seqs):
    batch, t, _ = seqs[0].shape
    c = WKV_CHUNK
    seq_spec = pl.BlockSpec((None, c, RWKV_WIDTH), lambda b, i: (b, i, 0))
    st_shape = (batch, RWKV_HEADS, RWKV_HEAD, RWKV_HEAD)
    st_spec = pl.BlockSpec((None,) + st_shape[1:], lambda b, i: (b, 0, 0, 0))
    return pl.pallas_call(
        _wkv_chunk_kernel,
        grid=(batch, t // c),
        in_specs=[seq_spec] * 6,
        out_specs=[seq_spec, st_spec],
        out_shape=[jax.ShapeDtypeStruct((batch, t, RWKV_WIDTH), F32), jax.ShapeDtypeStruct(st_shape, F32)],
        scratch_shapes=[pltpu.VMEM(st_shape[1:], F32)],
        compiler_params=_cp("parallel", "arbitrary"), name="wkv_chunk",
    )(*seqs)


def _rwkv_post_kernel(y_ref, r_ref, k_ref, v_ref, g_ref, rk_ref, lg_ref, lb_ref, o_ref):
    y = y_ref[...]
    inv = 1.0 / RWKV_HEAD
    mean = _head_sum(y) * inv
    yc = y - mean
    var = _head_sum(yc * yc) * inv
    yn = yc * lax.rsqrt(var + RWKV_LN_EPS) * lg_ref[...] + lb_ref[...]
    v = v_ref[...]
    bonus = _head_sum(r_ref[...] * k_ref[...] * rk_ref[...]) * v
    o_ref[...] = ((yn + bonus) * g_ref[...]).astype(o_ref.dtype)


def _rwkv_post(y, r, k, v, g, prm, layer, precise, *, tm):
    rows = y.shape[0]
    depth = prm["rk"].shape[0]
    row = pl.BlockSpec((tm, RWKV_WIDTH), lambda i: (i, 0))
    vec = pl.BlockSpec((None, 1, RWKV_WIDTH), lambda i: (layer, 0, 0))
    r3 = lambda a: a.reshape(depth, 1, RWKV_WIDTH)
    return pl.pallas_call(
        _rwkv_post_kernel,
        grid=(rows // tm,),
        in_specs=[row] * 5 + [vec] * 3,
        out_specs=row,
        out_shape=jax.ShapeDtypeStruct((rows, RWKV_WIDTH), _act_dtype(precise)),
        compiler_params=_cp("parallel"), name="rwkv_post",
    )(y, r, k, v, g, r3(prm["rk"]), r3(prm["ln_g"]), r3(prm["ln_b"]))


MOE_TM = 256


def _dispatch_plan(e_idx, tm):
    n = e_idx.shape[0]
    pairs = n * TOP_K
    e_flat = e_idx.reshape(pairs)
    onehot = (e_flat[:, None] == jnp.arange(N_EXPERTS, dtype=jnp.int32)[None, :]).astype(jnp.int32)
    csum = jnp.cumsum(onehot, axis=0)
    counts = csum[-1]
    padded = ((counts + tm - 1) // tm) * tm
    ends = jnp.cumsum(padded)
    starts = ends - padded
    pos = jnp.sum(onehot * (csum - 1 + starts[None, :]), axis=1)
    nt = (pairs + N_EXPERTS * (tm - 1)) // tm
    tile_start = jnp.arange(nt, dtype=jnp.int32) * tm
    tile_e = jnp.minimum(jnp.sum((tile_start[:, None] >= ends[None, :]).astype(jnp.int32), axis=1), N_EXPERTS - 1)
    tile_nv = jnp.clip(jnp.take(starts + counts, tile_e) - tile_start, 0, tm)
    src = jnp.zeros((nt * tm,), jnp.int32).at[pos].set(jnp.arange(pairs, dtype=jnp.int32) // TOP_K)
    return jnp.stack([tile_e, tile_nv]).astype(jnp.int32), src.reshape(nt, 1, tm), pos


def _moe_gather_kernel(meta_ref, src_ref, h_hbm, o_ref, sem):
    nv = meta_ref[1, pl.program_id(0)]
    tm = o_ref.shape[0]

    @pl.when(nv > 0)
    def _():
        def issue(i, c):
            for prio in range(2):
                r = 2 * i + prio
                pltpu.make_async_copy(h_hbm.at[pl.ds(src_ref[0, r], 1)], o_ref.at[pl.ds(r, 1)], sem).start(priority=prio)
            return c

        lax.fori_loop(0, tm // 2, issue, 0, unroll=4)
        pltpu.make_async_copy(h_hbm.at[pl.ds(0, tm)], o_ref, sem).wait()

    @pl.when(nv < tm)
    def _():
        rows = lax.broadcasted_iota(jnp.int32, o_ref.shape, 0)
        o_ref[...] = jnp.where(rows < nv, o_ref[...], 0.0)


def _moe_gather(meta, src, h):
    nt, _, tm = src.shape
    d = h.shape[1]
    gs = pltpu.PrefetchScalarGridSpec(
        num_scalar_prefetch=1, grid=(nt,),
        in_specs=[pl.BlockSpec((None, 1, tm), lambda j, m: (j, 0, 0), memory_space=pltpu.SMEM),
                  pl.BlockSpec(memory_space=pl.ANY)],
        out_specs=pl.BlockSpec((tm, d), lambda j, m: (j, 0)),
        scratch_shapes=[pltpu.SemaphoreType.DMA(())])
    return pl.pallas_call(
        _moe_gather_kernel, grid_spec=gs, out_shape=jax.ShapeDtypeStruct((nt * tm, d), F32),
        compiler_params=_cp("arbitrary"), name="moe_gather",
    )(meta, src, h)


def _moe_experts_kernel(meta_ref, xs_ref, wg_ref, wu_ref, wd_ref, y_ref, wg_bf, wu_bf, wd_bf):
    j = pl.program_id(0)
    e = meta_ref[0, j]
    nv = meta_ref[1, j]
    new_expert = (j == 0) | (e != meta_ref[0, jnp.maximum(j - 1, 0)])

    @pl.when(new_expert)
    def _():
        wg_bf[...] = wg_ref[...].astype(BF16)
        wu_bf[...] = wu_ref[...].astype(BF16)
        wd_bf[...] = wd_ref[...].astype(BF16)

    @pl.when(nv > 0)
    def _():
        x = xs_ref[...].astype(BF16)
        gate = _dot2(x, wg_bf[...])
        act = gate * _sigmoid(gate) * _dot2(x, wu_bf[...])
        y_ref[...] = _dot2(act.astype(BF16), wd_bf[...])

    @pl.when(nv == 0)
    def _():
        y_ref[...] = jnp.zeros_like(y_ref)


def _moe_experts(meta, xs, w_gate, w_up, w_down, layer):
    rows, d = xs.shape
    nt = meta.shape[1]
    tm = rows // nt
    once = dict(pipeline_mode=pl.Buffered(1))
    gs = pltpu.PrefetchScalarGridSpec(
        num_scalar_prefetch=1, grid=(nt,),
        in_specs=[pl.BlockSpec((tm, d), lambda j, m: (j, 0)),
                  pl.BlockSpec((None, None, d, D_EXPERT), lambda j, m: (layer, m[0, j], 0, 0), **once),
                  pl.BlockSpec((None, None, d, D_EXPERT), lambda j, m: (layer, m[0, j], 0, 0), **once),
                  pl.BlockSpec((None, None, D_EXPERT, d), lambda j, m: (layer, m[0, j], 0, 0), **once)],
        out_specs=pl.BlockSpec((tm, d), lambda j, m: (j, 0)),
        scratch_shapes=[pltpu.VMEM((d, D_EXPERT), BF16), pltpu.VMEM((d, D_EXPERT), BF16),
                        pltpu.VMEM((D_EXPERT, d), BF16)])
    return pl.pallas_call(
        _moe_experts_kernel, grid_spec=gs, out_shape=jax.ShapeDtypeStruct((rows, d), F32),
        compiler_params=_cp("arbitrary"), name="moe_experts",
    )(meta, xs, w_gate, w_up, w_down)


def _moe_combine_kernel(pos_ref, w_ref, y_hbm, m_ref, buf, sem):
    tt = m_ref.shape[0]

    def issue(i, c):
        for prio in range(2):
            r = 2 * i + prio
            pltpu.make_async_copy(y_hbm.at[pl.ds(pos_ref[0, r], 1)], buf.at[pl.ds(r, 1)], sem).start(priority=prio)
        return c

    lax.fori_loop(0, TOP_K * tt // 2, issue, 0, unroll=4)
    pltpu.make_async_copy(y_hbm.at[pl.ds(0, TOP_K * tt)], buf, sem).wait()
    w = w_ref[...]
    m_ref[...] = w[:, 0:1] * buf[0:tt, :] + w[:, 1:2] * buf[tt:2 * tt, :]


def _moe_combine(pos, wts, y, *, tt):
    n = wts.shape[0]
    d = y.shape[1]
    pos_t = jnp.swapaxes(pos.reshape(n // tt, tt, TOP_K), 1, 2).reshape(n // tt, 1, TOP_K * tt)
    return pl.pallas_call(
        _moe_combine_kernel,
        grid=(n // tt,),
        in_specs=[pl.BlockSpec((None, 1, TOP_K * tt), lambda i: (i, 0, 0), memory_space=pltpu.SMEM),
                  pl.BlockSpec((tt, TOP_K), lambda i: (i, 0)),
                  pl.BlockSpec(memory_space=pl.ANY)],
        out_specs=pl.BlockSpec((tt, d), lambda i: (i, 0)),
        out_shape=jax.ShapeDtypeStruct((n, d), F32),
        scratch_shapes=[pltpu.VMEM((TOP_K * tt, d), F32), pltpu.SemaphoreType.DMA(())],
        compiler_params=_cp("arbitrary"), name="moe_combine",
    )(pos_t, wts, y)


def _moe(h_all, e_idx, wts, w_gate, w_up, w_down, layer):
    n = h_all.shape[0]
    meta, src, pos = _dispatch_plan(e_idx, MOE_TM)
    xs = _moe_gather(meta, src, h_all)
    y = _moe_experts(meta, xs, w_gate, w_up, w_down, layer)
    tt = 256
    while n % tt:
        tt //= 2
    return _moe_combine(pos, wts, y, tt=tt)


_SH1, _SC1, _GT1, _SH2, _SC2, _GT2 = range(6)


def kernel(x_prompt, x_sample, cache_swa_k, cache_swa_v, state_pool, state_rwkv_shift, state_rwkv_wkv, state_conv, c_prompt, c_sample, w_ada, b_ada, g_norm1, g_norm2, w_in, sinks, w_pool, ls_pool, rwkv_mu, rwkv_w0, rwkv_w2, rwkv_a0, rwkv_a2, rwkv_g2, rwkv_kk, rwkv_ka, rwkv_rk, rwkv_ln_g, rwkv_ln_b, conv_w, w_branch, w_out, w_router, b_router, w_gate, w_up, w_down, g_final):
    depth = w_in.shape[0]
    batch, seq, d = x_prompt.shape
    dbatch, dseq, _ = x_sample.shape
    wbuf = cache_swa_k.shape[2]
    np_rows, ns_rows = batch * seq, dbatch * dseq

    grp_p = _Group(np_rows, seq, 1)
    grp_s = _Group(ns_rows, ns_rows, ns_rows)
    assert seq <= 2048, "the big matmuls take one whole prompt sequence per row tile"
    tm_p, tm_s = seq, ns_rows
    te_p = min(512, seq)

    c_all = jnp.concatenate([c_prompt, c_sample], axis=0)
    wr_pad = jnp.pad(w_router, ((0, 0), (0, ROUTER_PAD - N_EXPERTS)))
    prm = dict(mu=rwkv_mu, w0=rwkv_w0, w2=rwkv_w2, a0=rwkv_a0, a2=rwkv_a2, g2=rwkv_g2, kk=rwkv_kk, ka=rwkv_ka,
               rk=rwkv_rk.reshape(depth, RWKV_WIDTH), ln_g=rwkv_ln_g, ln_b=rwkv_ln_b)
    cache_k = cache_swa_k.reshape(depth, dbatch, wbuf, KV_WIDTH)
    cache_v = cache_swa_v.reshape(depth, dbatch, wbuf, KV_WIDTH)

    xp = x_prompt.reshape(np_rows, d)
    xs = x_sample.reshape(ns_rows, d)
    st_p, st_s = [], []
    pend_p = pend_s = None
    for l in range(depth):
        precise = l == 0
        hd = (BF16, F32) if precise else (BF16,)
        mod = _ada(c_all, w_ada, b_ada, l, precise)
        mod_p = mod[:batch].reshape(batch, 1, 6 * d)
        mod_s = jnp.repeat(mod[batch:], dseq, axis=0).reshape(1, ns_rows, 6 * d)

        def first_norm(grp, x, pend, mod3, tm):
            if pend is None:
                return (x,) + tuple(_normx(grp, x, g_norm1[l], mod=(mod3, _SC1, _SH1), out_dtype=hd, tm=tm))
            return tuple(_normx(grp, x, g_norm1[l], add=pend + (_GT2,), mod=(mod3, _SC1, _SH1), emit_x=True,
                                out_dtype=hd, tm=tm))

        xp, hp, *hp32 = first_norm(grp_p, xp, pend_p, mod_p, te_p)
        xs, hs, *hs32 = first_norm(grp_s, xs, pend_s, mod_s, tm_s)

        def tails(tm, grp):
            if not precise:
                return None
            return (tm, 1) if grp is grp_s else (min(PRECISE_TAIL, tm), seq // tm)

        tail_p, tail_s = (min(PRECISE_TAIL, tm_p), tm_s) if precise else (0, 0)
        pp = _mm(hp, w_in, l, tm=tm_p, tn=512, a_f32=hp32[0] if precise else None, tail_rows=tail_p)
        ps = _mm(hs, w_in, l, tm=tm_s, tn=512, a_f32=hs32[0] if precise else None, tail_rows=tail_s)

        a_p = _swa_prompt(pp, sinks, l, batch, seq, precise)
        b_p = _pool_prompt(pp, w_pool, ls_pool, l, batch, seq, precise)
        d_p, conv_new_p = _conv_prompt(pp, conv_w, l, batch, seq, precise)
        pre_p = _rwkv_pre_prompt(pp, prm, l, batch, seq, precise, tt=te_p)
        r_p, w_p, k_p, v_p, ka_p, kb_p, g_p = pre_p
        as3 = lambda t: t.reshape(batch, seq, RWKV_WIDTH)
        y_p, wkv_new_p = _wkv_chunked([as3(t) for t in (r_p, w_p, k_p, v_p, ka_p, kb_p)])
        c_p = _rwkv_post(y_p.reshape(np_rows, RWKV_WIDTH), r_p, k_p, v_p, g_p, prm, l, precise, tm=te_p)
        pp3 = pp.reshape(batch, seq, IN_WIDTH)
        kw = min(WINDOW, seq)
        st_p.append((pp3[:, seq - kw:, OFF_K:OFF_K + KV_WIDTH].reshape(batch, kw, N_KV_HEADS, HEAD_DIM),
                     pp3[:, seq - kw:, OFF_V:OFF_V + KV_WIDTH].reshape(batch, kw, N_KV_HEADS, HEAD_DIM),
                     pp3[:, seq - POOL_BUF:, OFF_U:OFF_U + POOL_WIDTH],
                     pp3[:, seq - 1, OFF_PC:OFF_PC + RWKV_PROJ],
                     wkv_new_p, conv_new_p))

        ps3 = ps.reshape(dbatch, dseq, IN_WIDTH)
        a_s, k_new_s, v_new_s = _swa_sample(ps3, cache_k, cache_v, sinks, l, precise)
        tmaj = lambda t: jnp.swapaxes(t, 0, 1)
        b_s_t, pool_new_t = _pool_sample(tmaj(ps3[:, :, OFF_U:OFF_U + POOL_WIDTH]), tmaj(state_pool[l]),
                                         w_pool, ls_pool, l, PAST_LEN, precise)
        d_s_t, conv_new_t = _conv_sample(tmaj(ps3[:, :, OFF_CB:OFF_CB + CONV_WIDTH]),
                                         tmaj(ps3[:, :, OFF_CC:OFF_CC + CONV_WIDTH]),
                                         tmaj(ps3[:, :, OFF_CX:OFF_CX + CONV_WIDTH]), tmaj(state_conv[l]), conv_w, l,
                                         precise)
        pc_s3 = ps3[:, :, OFF_PC:OFF_PC + RWKV_PROJ]
        pc_shift = jnp.concatenate([state_rwkv_shift[l][:, None, :], pc_s3[:, :-1, :]], axis=1)
        pre_s = _rwkv_pre_sample(pc_s3.reshape(ns_rows, RWKV_PROJ), pc_shift.reshape(ns_rows, RWKV_PROJ), prm, l,
                                 precise)
        r_s, w_s, k_s, v_s, ka_s, kb_s, g_s = pre_s
        as3s = lambda t: t.reshape(dbatch, dseq, RWKV_WIDTH)
        y_s, wkv_new_s = _wkv_steps([as3s(t) for t in (r_s, w_s, k_s, v_s, ka_s, kb_s)], state_rwkv_wkv[l])
        c_s = _rwkv_post(y_s.reshape(ns_rows, RWKV_WIDTH), r_s, k_s, v_s, g_s, prm, l, precise, tm=ns_rows)
        st_s.append((k_new_s.reshape(dbatch, wbuf, N_KV_HEADS, HEAD_DIM),
                     v_new_s.reshape(dbatch, wbuf, N_KV_HEADS, HEAD_DIM),
                     tmaj(pool_new_t), pc_s3[:, dseq - 1, :], wkv_new_s, tmaj(conv_new_t)))
        br_s = (a_s.reshape(ns_rows, ATTN_WIDTH), tmaj(b_s_t).reshape(ns_rows, POOL_WIDTH), c_s,
                tmaj(d_s_t).reshape(ns_rows, CONV_WIDTH))

        def tail(grp, x, branches, p, mod3, tm_mm, tm_el):
            if precise:
                tm_t = min(tm_mm, 512)
                merged32, merged = _merge(branches, p, w_branch, l, tm=tm_t, tn=512, tail=tails(tm_t, grp))
                x1 = _mm_res(grp, merged, w_out, l, x, mod3, _GT1, tm=tm_mm, tn=512, a_f32=merged32,
                             tail_rows=tail_s if grp is grp_s else tail_p)
            else:
                merged, = _merge(branches, p, w_branch, l, tm=min(tm_mm, 1024), tn=512, tail=None)
                x1 = _mm_res(grp, merged, w_out, l, x, mod3, _GT1, tm=tm_mm, tn=512)
            return (x1,) + tuple(_normx(grp, x1, g_norm2[l], mod=(mod3, _SC2, _SH2), route=(wr_pad, b_router),
                                        out_dtype=F32, tm=tm_el))

        xp, h2_p, e_p, wt_p = tail(grp_p, xp, (a_p, b_p, c_p, d_p), pp, mod_p, tm_p, te_p)
        xs, h2_s, e_s, wt_s = tail(grp_s, xs, br_s, ps, mod_s, tm_s, tm_s)
        m_all = _moe(jnp.concatenate([h2_p, h2_s]), jnp.concatenate([e_p, e_s]), jnp.concatenate([wt_p, wt_s]),
                     w_gate, w_up, w_down, l)
        pend_p, pend_s = (m_all, 0, mod_p), (m_all, np_rows, mod_s)

    y_p = _normx(grp_p, xp, g_final, add=pend_p + (_GT2,), out_dtype=F32, tm=te_p)[0]
    y_s = _normx(grp_s, xs, g_final, add=pend_s + (_GT2,), out_dtype=F32, tm=tm_s)[0]

    def stack(states, i):
        return jnp.stack([s[i] for s in states])

    return ((y_p.reshape(batch, seq, d), y_s.reshape(dbatch, dseq, d))
            + tuple(stack(st_p, i) for i in range(6)) + tuple(stack(st_s, i) for i in range(6)))
```
